```python
import math
import jax, jax.numpy as jnp
from jax import lax
import numpy as np

D_MODEL = 1024
BATCH = 32
SEQ = 2048
DEPTH = 1
DEC_BATCH = 8
DEC_SEQ = 64
PAST_LEN = 2048

CHUNK = 64
N_META = 16
EPS = 1e-6
ATT_HEADS = 8
ATT_KV_HEADS = 4
ATT_HEAD_DIM = 64
ATT_GROUP = ATT_HEADS // ATT_KV_HEADS
ATT_WIDTH = ATT_HEADS * ATT_HEAD_DIM
ATT_KV_WIDTH = ATT_KV_HEADS * ATT_HEAD_DIM
Q_BLOCK = 128
FORGET_BIAS_INIT = 3.0
HG_HEADS = 4
HG_KEY_DIM = 128
HG_VAL_DIM = 128
HG_KEY_WIDTH = HG_HEADS * HG_KEY_DIM
HG_VAL_WIDTH = HG_HEADS * HG_VAL_DIM
MIX_WIDTH = ATT_WIDTH + HG_VAL_WIDTH
IN_SPLITS = (ATT_WIDTH, ATT_KV_WIDTH, ATT_KV_WIDTH, ATT_HEADS,
             HG_KEY_WIDTH, HG_KEY_WIDTH, HG_VAL_WIDTH, HG_VAL_WIDTH)
IN_OFFSETS = tuple(int(o) for o in np.cumsum((0,) + IN_SPLITS))
IN_WIDTH = IN_OFFSETS[-1]
PEER_HEADS = 8
PEER_N_KEYS = 128
PEER_N_EXPERTS = PEER_N_KEYS * PEER_N_KEYS
PEER_TOPK = 16
PEER_HALF = 128
PEER_QUERY_DIM = 2 * PEER_HALF
PEER_TOKEN_BLOCK = 256

kernel_name = "fox_hgrn2_peer_streaming_step"


def rms_norm(x, gain):
    x32 = x.astype(jnp.float32)
    y = x32 * lax.rsqrt(jnp.mean(x32 * x32, axis=-1, keepdims=True) + EPS)
    return (y * gain.astype(jnp.float32)).astype(x.dtype)


def mixer_projections(h, w_in, b_forget, lower_bound):
    B, T, _ = h.shape
    p = jnp.einsum('btd,dp->btp', h, w_in)
    q, k, v, fg, hq, hf, hi, hg = [p[..., IN_OFFSETS[n]:IN_OFFSETS[n + 1]] for n in range(len(IN_SPLITS))]
    q = q.reshape(B, T, ATT_HEADS, ATT_HEAD_DIM)
    k = k.reshape(B, T, ATT_KV_HEADS, ATT_HEAD_DIM)
    v = v.reshape(B, T, ATT_KV_HEADS, ATT_HEAD_DIM)
    logf_att = jax.nn.log_sigmoid((fg + b_forget).astype(jnp.float32))
    hq = jax.nn.silu(hq.astype(jnp.float32)).reshape(B, T, HG_HEADS, HG_KEY_DIM)
    f = lower_bound + (1.0 - lower_bound) * jax.nn.sigmoid(hf.astype(jnp.float32))
    hk = (1.0 - f).reshape(B, T, HG_HEADS, HG_KEY_DIM)
    hlogf = jnp.log(f).reshape(B, T, HG_HEADS, HG_KEY_DIM)
    hi = hi.astype(jnp.float32).reshape(B, T, HG_HEADS, HG_VAL_DIM)
    return q, k, v, logf_att, hq, hk, hlogf, hi, hg


def fox_attend(q, fq, qpos, k, v, fk, kpos):
    B, Tq = q.shape[:2]
    Tk = k.shape[1]
    qg = q.reshape(B, Tq, ATT_KV_HEADS, ATT_GROUP, ATT_HEAD_DIM)
    s = jnp.einsum('bqkgd,bskd->bkgqs', qg, k).astype(jnp.float32) * (ATT_HEAD_DIM ** -0.5)
    fq_g = fq.reshape(B, Tq, ATT_KV_HEADS, ATT_GROUP).transpose(0, 2, 3, 1)[..., None]
    fk_g = fk.reshape(B, Tk, ATT_KV_HEADS, ATT_GROUP).transpose(0, 2, 3, 1)[..., None, :]
    mask = kpos[None, :] <= qpos[:, None]
    s = jnp.where(mask, s + fq_g - fk_g, -jnp.inf)
    p = jax.nn.softmax(s, axis=-1).astype(v.dtype)
    o = jnp.einsum('bkgqs,bskd->bqkgd', p, v)
    return o.reshape(B, Tq, ATT_WIDTH)


def fox_prompt(q, k, v, logf):
    B, L = q.shape[:2]
    F = jnp.cumsum(logf, axis=1)
    nb = -(-L // Q_BLOCK)
    pad = nb * Q_BLOCK - L
    qp = jnp.pad(q, ((0, 0), (0, pad), (0, 0), (0, 0)))
    Fp = jnp.pad(F, ((0, 0), (0, pad), (0, 0)))
    q_blocks = qp.reshape(B, nb, Q_BLOCK, ATT_HEADS, ATT_HEAD_DIM).transpose(1, 0, 2, 3, 4)
    F_blocks = Fp.reshape(B, nb, Q_BLOCK, ATT_HEADS).transpose(1, 0, 2, 3)
    qpos_blocks = jnp.arange(nb * Q_BLOCK).reshape(nb, Q_BLOCK)
    kpos = jnp.arange(L)
    out = lax.map(lambda a: fox_attend(a[0], a[1], a[2], k, v, F, kpos),
                  (q_blocks, F_blocks, qpos_blocks))
    return out.transpose(1, 0, 2, 3).reshape(B, nb * Q_BLOCK, ATT_WIDTH)[:, :L]


def fox_sample(q, k, v, logf, cache_k, cache_v, cache_logf):
    T = q.shape[1]
    P = cache_k.shape[1]
    k_all = jnp.concatenate([cache_k.astype(k.dtype), k], axis=1)
    v_all = jnp.concatenate([cache_v.astype(v.dtype), v], axis=1)
    F = jnp.cumsum(jnp.concatenate([cache_logf.astype(jnp.float32), logf], axis=1), axis=1)
    kpos = jnp.arange(P + T)
    qpos = P + jnp.arange(T)
    return fox_attend(q, F[:, P:], qpos, k_all, v_all, F, kpos)


def hgrn_chunk(S, q, k, logf, i):
    C = q.shape[1]
    G = jnp.cumsum(logf, axis=1)
    t = jnp.arange(C)
    causal = (t[None, :] <= t[:, None])[None, :, :, None, None]
    decay = jnp.exp(jnp.where(causal, G[:, :, None] - G[:, None, :], -jnp.inf))
    scores = jnp.einsum('bthk,btshk,bshk->bhts', q, decay, k)
    o = (jnp.einsum('bhts,bshv->bthv', scores, i)
         + jnp.einsum('bthk,bhkv->bthv', q * jnp.exp(G), S))
    G_end = G[:, -1]
    S_new = (jnp.exp(G_end)[..., None] * S
             + jnp.einsum('bshk,bshv->bhkv', k * jnp.exp(G_end[:, None] - G), i))
    return o, S_new


def hgrn_prompt(q, k, logf, i):
    B = q.shape[0]
    S0 = jnp.zeros((B, HG_HEADS, HG_KEY_DIM, HG_VAL_DIM), jnp.float32)
    o_meta, S = hgrn_chunk(S0, q[:, :N_META], k[:, :N_META], logf[:, :N_META], i[:, :N_META])
    n_real = q.shape[1] - N_META
    nc = n_real // CHUNK

    def to_chunks(a):
        a = a[:, N_META:]
        return jnp.moveaxis(a.reshape(B, nc, CHUNK, *a.shape[2:]), 1, 0)

    def step(S_c, xs):
        o_c, S_c = hgrn_chunk(S_c, *xs)
        return S_c, o_c

    S, o_rest = lax.scan(step, S, (to_chunks(q), to_chunks(k), to_chunks(logf), to_chunks(i)))
    o_rest = jnp.moveaxis(o_rest, 0, 1).reshape(B, n_real, HG_HEADS, HG_VAL_DIM)
    return jnp.concatenate([o_meta, o_rest], axis=1), S


def hgrn_output(o, g_out, norm_gain, dtype):
    B, T = o.shape[:2]
    o = o * lax.rsqrt(jnp.mean(o * o, axis=-1, keepdims=True) + EPS)
    o = o * norm_gain.astype(jnp.float32).reshape(HG_HEADS, HG_VAL_DIM)
    return o.reshape(B, T, HG_VAL_WIDTH).astype(dtype) * jax.nn.silu(g_out)


def peer_ffn(x, w_query, sub_keys, expert_u, expert_v):
    N = x.shape[0]
    nb = -(-N // PEER_TOKEN_BLOCK)
    xb_all = jnp.pad(x, ((0, nb * PEER_TOKEN_BLOCK - N), (0, 0))).reshape(nb, PEER_TOKEN_BLOCK, D_MODEL)

    def block(xb):
        T = xb.shape[0]
        q = (xb @ w_query).reshape(T, PEER_HEADS, 2, PEER_HALF)
        s = jnp.einsum('thcd,hcnd->thcn', q, sub_keys)
        sa, ia = lax.top_k(s[:, :, 0], PEER_TOPK)
        sb, ib = lax.top_k(s[:, :, 1], PEER_TOPK)
        cand = (sa[..., :, None] + sb[..., None, :]).reshape(T, PEER_HEADS, PEER_TOPK * PEER_TOPK)
        cidx = (ia[..., :, None] * PEER_N_KEYS + ib[..., None, :]).reshape(T, PEER_HEADS, PEER_TOPK * PEER_TOPK)
        top_s, top_j = lax.top_k(cand, PEER_TOPK)
        eidx = jnp.take_along_axis(cidx, top_j, axis=-1)
        gate = jax.nn.softmax(top_s.astype(jnp.float32), axis=-1).astype(xb.dtype)
        act = jax.nn.gelu(jnp.einsum('thkd,td->thk', expert_u[eidx], xb), approximate=False)
        return jnp.einsum('thk,thkd->td', gate * act, expert_v[eidx])

    y = lax.map(block, xb_all)
    return y.reshape(nb * PEER_TOKEN_BLOCK, D_MODEL)[:N]


def trunk_layer(x, w_in, b_forget, lower_bound, hg_norm_gain, w_out, norm1_gain, norm2_gain,
                peer_w_query, peer_sub_keys, peer_expert_u, peer_expert_v, cache=None):
    h = rms_norm(x, norm1_gain)
    q, k, v, logf, hq, hk, hlogf, hi, hg = mixer_projections(h, w_in, b_forget, lower_bound)
    if cache is None:
        att = fox_prompt(q, k, v, logf)
        o, S = hgrn_prompt(hq, hk, hlogf, hi)
    else:
        cache_k, cache_v, cache_logf, state = cache
        att = fox_sample(q, k, v, logf, cache_k, cache_v, cache_logf)
        o, S = hgrn_chunk(state.astype(jnp.float32), hq, hk, hlogf, hi)
    mix = jnp.concatenate([att.astype(x.dtype), hgrn_output(o, hg, hg_norm_gain, x.dtype)], axis=-1)
    x = x + jnp.einsum('btm,md->btd', mix, w_out)
    h2 = rms_norm(x, norm2_gain)
    x = x + peer_ffn(h2.reshape(-1, D_MODEL), peer_w_query, peer_sub_keys,
                     peer_expert_u, peer_expert_v).reshape(x.shape)
    return x, (k, v, logf.astype(x.dtype), S.astype(x.dtype))


def setup_inputs(seed: int = 0) -> dict:
    key = jax.random.key(seed)
    ks = jax.random.split(key, 19)
    f32 = jnp.float32

    def nrm(k, shape, scale):
        return jax.random.normal(k, shape, f32) * scale

    return {
        "x_prompt": nrm(ks[0], (BATCH, SEQ, D_MODEL), 1.0),
        "x_sample": nrm(ks[1], (DEC_BATCH, DEC_SEQ, D_MODEL), 1.0),
        "cache_k": nrm(ks[2], (DEPTH, DEC_BATCH, PAST_LEN, ATT_KV_HEADS, ATT_HEAD_DIM), 1.0),
        "cache_v": nrm(ks[3], (DEPTH, DEC_BATCH, PAST_LEN, ATT_KV_HEADS, ATT_HEAD_DIM), 1.0),
        "cache_logf": jax.nn.log_sigmoid(FORGET_BIAS_INIT + nrm(ks[4], (DEPTH, DEC_BATCH, PAST_LEN, ATT_HEADS), 1.0)),
        "state_hgrn": nrm(ks[5], (DEPTH, DEC_BATCH, HG_HEADS, HG_KEY_DIM, HG_VAL_DIM), 0.5),
        "meta_tokens": nrm(ks[6], (N_META, D_MODEL), 1.0),
        "w_in": nrm(ks[7], (DEPTH, D_MODEL, IN_WIDTH), D_MODEL ** -0.5),
        "b_forget": FORGET_BIAS_INIT + nrm(ks[8], (DEPTH, ATT_HEADS), 0.1),
        "hg_lb_logits": nrm(ks[9], (DEPTH + 1, HG_KEY_WIDTH), 0.5),
        "hg_norm_gain": 1.0 + nrm(ks[10], (DEPTH, HG_VAL_WIDTH), 0.02),
        "w_out": nrm(ks[11], (DEPTH, MIX_WIDTH, D_MODEL), MIX_WIDTH ** -0.5),
        "norm1_gain": 1.0 + nrm(ks[12], (DEPTH, D_MODEL), 0.02),
        "norm2_gain": 1.0 + nrm(ks[13], (DEPTH, D_MODEL), 0.02),
        "peer_w_query": nrm(ks[14], (DEPTH, D_MODEL, PEER_HEADS * PEER_QUERY_DIM), D_MODEL ** -0.5),
        "peer_sub_keys": nrm(ks[15], (DEPTH, PEER_HEADS, 2, PEER_N_KEYS, PEER_HALF), PEER_HALF ** -0.5),
        "peer_expert_u": nrm(ks[16], (DEPTH, PEER_N_EXPERTS, D_MODEL), D_MODEL ** -0.5),
        "peer_expert_v": nrm(ks[17], (DEPTH, PEER_N_EXPERTS, D_MODEL), PEER_HEADS ** -0.5),
        "final_norm_gain": 1.0 + nrm(ks[18], (D_MODEL,), 0.02),
    }


def reference(x_prompt, x_sample, cache_k, cache_v, cache_logf, state_hgrn, meta_tokens,
              w_in, b_forget, hg_lb_logits, hg_norm_gain, w_out, norm1_gain, norm2_gain,
              peer_w_query, peer_sub_keys, peer_expert_u, peer_expert_v, final_norm_gain):
    lower_bounds = jnp.cumsum(jax.nn.softmax(hg_lb_logits.astype(jnp.float32), axis=0), axis=0)[:DEPTH]
    B = x_prompt.shape[0]
    meta = jnp.broadcast_to(meta_tokens.astype(x_prompt.dtype)[None], (B, N_META, D_MODEL))
    xp = jnp.concatenate([meta, x_prompt], axis=1)
    xs = x_sample
    prompt_states = [[], [], [], []]
    sample_states = [[], [], [], []]
    for l in range(DEPTH):
        params = (w_in[l], b_forget[l], lower_bounds[l], hg_norm_gain[l], w_out[l], norm1_gain[l],
                  norm2_gain[l], peer_w_query[l], peer_sub_keys[l], peer_expert_u[l], peer_expert_v[l])
        xp, sp = trunk_layer(xp, *params)
        xs, ss = trunk_layer(xs, *params, cache=(cache_k[l], cache_v[l], cache_logf[l], state_hgrn[l]))
        for n in range(4):
            prompt_states[n].append(sp[n])
            sample_states[n].append(ss[n])
    y_prompt = rms_norm(xp, final_norm_gain)[:, N_META:]
    y_sample = rms_norm(xs, final_norm_gain)
    new_k_prompt = jnp.stack(prompt_states[0])
    new_v_prompt = jnp.stack(prompt_states[1])
    new_logf_prompt = jnp.stack(prompt_states[2])
    new_state_prompt = jnp.stack(prompt_states[3])
    new_k_sample = jnp.stack(sample_states[0])
    new_v_sample = jnp.stack(sample_states[1])
    new_logf_sample = jnp.stack(sample_states[2])
    new_state_sample = jnp.stack(sample_states[3])
    return (y_prompt, y_sample, new_k_prompt, new_v_prompt, new_logf_prompt, new_state_prompt,
            new_k_sample, new_v_sample, new_logf_sample, new_state_sample)
```

```python
import functools

import jax
import jax.numpy as jnp
from jax import lax
from jax.experimental import pallas as pl
from jax.experimental.pallas import tpu as pltpu

F32 = jnp.float32
BF16 = jnp.bfloat16
EPS = 1e-6
N_META = 16

D_MODEL = 1024
ATT_HEADS = 8
ATT_KV_HEADS = 4
ATT_HEAD_DIM = 64
ATT_GROUP = ATT_HEADS // ATT_KV_HEADS
ATT_WIDTH = ATT_HEADS * ATT_HEAD_DIM
ATT_KV_WIDTH = ATT_KV_HEADS * ATT_HEAD_DIM
HG_HEADS = 4
HG_DIM = 128
HG_WIDTH = HG_HEADS * HG_DIM
PEER_HEADS = 8
PEER_N_KEYS = 128
PEER_TOPK = 16
PEER_HALF = 128
PEER_SEL = PEER_HEADS * PEER_TOPK

LANES = 128
VMEM_LIMIT = 48 * 1024 * 1024

TOKEN_BLOCK_PROJ = 512
TOKEN_BLOCK_ROUTE = 256
TOKEN_BLOCK_EXPERT = 128
GATHER_SLOTS = 4


def _params(*sem):
    return pltpu.CompilerParams(dimension_semantics=sem, vmem_limit_bytes=VMEM_LIMIT)


def _largest_block(n, cap, mult):
    best = None
    for d in range(mult, cap + 1, mult):
        if n % d == 0:
            best = d
    assert best is not None, (n, cap, mult)
    return best


def _split3(x):
    hi = x.astype(BF16)
    r1 = x - hi.astype(F32)
    mid = r1.astype(BF16)
    lo = (r1 - mid.astype(F32)).astype(BF16)
    return hi, mid, lo


def _tri_cumsum(tri, x):
    hi, mid, lo = _split3(x)
    d = lambda a: jnp.dot(tri, a, preferred_element_type=F32)
    return (d(lo) + d(mid)) + d(hi)


def _lower_tri(c):
    r = lax.broadcasted_iota(jnp.int32, (c, c), 0)
    s = lax.broadcasted_iota(jnp.int32, (c, c), 1)
    return jnp.where(s <= r, 1.0, 0.0).astype(BF16)


def _rms(x, gain):
    return x * lax.rsqrt(jnp.mean(x * x, axis=-1, keepdims=True) + EPS) * gain


def _inproj_body(x_ref, g_ref, wqkv_ref, wfg_ref, bfg_ref, wh_ref,
                 q_ref, k_ref, v_ref, logf_ref, hgrp_ref):
    hb = _rms(x_ref[...], g_ref[...]).astype(BF16)
    qkv = jnp.dot(hb, wqkv_ref[...], preferred_element_type=F32)
    q_ref[...] = (qkv[:, :ATT_WIDTH] * (ATT_HEAD_DIM ** -0.5)).astype(BF16)
    k_ref[...] = qkv[:, ATT_WIDTH:ATT_WIDTH + ATT_KV_WIDTH]
    v_ref[...] = qkv[:, ATT_WIDTH + ATT_KV_WIDTH:]
    fg = jnp.dot(hb, wfg_ref[...], preferred_element_type=F32)[:, :ATT_HEADS] + bfg_ref[...]
    logf_ref[...] = jnp.minimum(fg, 0.0) - jnp.log1p(jnp.exp(-jnp.abs(fg)))
    hgrp_ref[...] = jnp.dot(hb, wh_ref[...], preferred_element_type=F32)


def _inproj(x, gain, wqkv, wfg, bfg, wh):
    n = x.shape[0]
    tb = _largest_block(n, TOKEN_BLOCK_PROJ, 8)
    row = lambda w: pl.BlockSpec((tb, w), lambda i: (i, 0))
    full = lambda a: pl.BlockSpec(a.shape, lambda i: (0,) * a.ndim)
    return pl.pallas_call(
        _inproj_body,
        grid=(n // tb,),
        in_specs=[row(D_MODEL), full(gain), full(wqkv), full(wfg), full(bfg), full(wh)],
        out_specs=[row(ATT_WIDTH), row(ATT_KV_WIDTH), row(ATT_KV_WIDTH), row(ATT_HEADS), row(4 * HG_WIDTH)],
        out_shape=[jax.ShapeDtypeStruct((n, ATT_WIDTH), BF16),
                   jax.ShapeDtypeStruct((n, ATT_KV_WIDTH), F32),
                   jax.ShapeDtypeStruct((n, ATT_KV_WIDTH), F32),
                   jax.ShapeDtypeStruct((n, ATT_HEADS), F32),
                   jax.ShapeDtypeStruct((n, 4 * HG_WIDTH), F32)],
        compiler_params=_params("parallel"),
    )(x, gain, wqkv, wfg, bfg, wh)


def _cumsum_body(x_ref, o_ref, *, cb):
    length, width = x_ref.shape[1], x_ref.shape[2]
    tri = _lower_tri(cb)
    carry = jnp.zeros((1, width), F32)
    for j in range(length // cb):
        f = _tri_cumsum(tri, x_ref[0, j * cb:(j + 1) * cb, :]) + carry
        o_ref[0, j * cb:(j + 1) * cb, :] = f
        carry = f[cb - 1:cb, :]


def _cumsum(logf):
    b, length, h = logf.shape
    cb = _largest_block(length, 512, 8)
    spec = pl.BlockSpec((1, length, h), lambda i: (i, 0, 0))
    return pl.pallas_call(
        functools.partial(_cumsum_body, cb=cb),
        grid=(b,),
        in_specs=[spec],
        out_specs=spec,
        out_shape=jax.ShapeDtypeStruct(logf.shape, F32),
        compiler_params=_params("parallel"),
    )(logf)


def _attn_body(q_ref, k_ref, v_ref, fc_ref, fr_ref, o_ref, *, q_off, tq):
    lk = k_ref.shape[0]
    qpos = q_off + pl.program_id(1) * tq + lax.broadcasted_iota(jnp.int32, (tq, lk), 0)
    kpos = lax.broadcasted_iota(jnp.int32, (tq, lk), 1)
    mask = kpos <= qpos
    kb = k_ref[...].astype(BF16)
    vb = v_ref[...].astype(BF16)
    q = q_ref[...]
    fc = fc_ref[...]
    fr = fr_ref[0]
    outs = []
    for h in range(ATT_HEADS):
        g = h // ATT_GROUP
        hs = slice(h * ATT_HEAD_DIM, (h + 1) * ATT_HEAD_DIM)
        gs = slice(g * ATT_HEAD_DIM, (g + 1) * ATT_HEAD_DIM)
        s = lax.dot_general(q[:, hs], kb[:, gs], (((1,), (1,)), ((), ())), preferred_element_type=F32)
        s = s + fc[:, h:h + 1] - fr[h:h + 1, :]
        s = jnp.where(mask, s, -jnp.inf)
        p = jnp.exp(s - jnp.max(s, axis=-1, keepdims=True))
        l = jnp.sum(p, axis=-1, keepdims=True)
        o = jnp.dot(p.astype(BF16), vb[:, gs], preferred_element_type=F32)
        outs.append(o / l)
    o_ref[...] = jnp.concatenate(outs, axis=-1).astype(BF16)


def _attn(q, k, v, f_col, f_row, *, batch, q_off):
    t_q = q.shape[0] // batch
    lk = k.shape[0] // batch
    tq = _largest_block(t_q, 384, 8)
    nq = t_q // tq
    qspec = lambda w: pl.BlockSpec((tq, w), lambda b, i: (b * nq + i, 0))
    kspec = pl.BlockSpec((lk, ATT_KV_WIDTH), lambda b, i: (b, 0))
    return pl.pallas_call(
        functools.partial(_attn_body, q_off=q_off, tq=tq),
        grid=(batch, nq),
        in_specs=[qspec(ATT_WIDTH), kspec, kspec, qspec(ATT_HEADS),
                  pl.BlockSpec((1, ATT_HEADS, lk), lambda b, i: (b, 0, 0))],
        out_specs=qspec(ATT_WIDTH),
        out_shape=jax.ShapeDtypeStruct((batch * t_q, ATT_WIDTH), BF16),
        compiler_params=_params("parallel", "parallel"),
    )(q, k, v, f_col, f_row)


def _hgrn_body(hq_ref, hf_ref, hi_ref, hg_ref, lbl_ref, gain_ref, s0_ref, o_ref, st_ref,
               g_s, q_s, k_s, i_s, oi_s, *, chunk, layer):
    length = hq_ref.shape[0]
    lbl = lbl_ref[...]
    e = jnp.exp(lbl - jnp.max(lbl, axis=0, keepdims=True))
    lb = jnp.sum(e[:layer + 1, :], axis=0, keepdims=True) / jnp.sum(e, axis=0, keepdims=True)
    gain = gain_ref[...]
    tri = _lower_tri(chunk)
    rows = lax.broadcasted_iota(jnp.int32, (chunk, 1), 0)

    def chunk_step(c, st):
        r0 = pl.multiple_of(c * chunk, 16)
        f = lb + (1.0 - lb) * jax.nn.sigmoid(hf_ref[pl.ds(r0, chunk), :])
        kk = 1.0 - f
        hq = hq_ref[pl.ds(r0, chunk), :]
        qq = hq * jax.nn.sigmoid(hq)
        iv = hi_ref[pl.ds(r0, chunk), :]
        g = _tri_cumsum(tri, jnp.log(f))
        g_s[...] = g
        q_s[...] = qq
        k_s[...] = kk
        i_s[...] = iv
        o_inter = lax.dot_general((qq * jnp.exp(g)).astype(BF16), st.astype(BF16),
                                  (((1,), (1,)), ((), ())), preferred_element_type=F32)

        def row_step(t, carry):
            d = jnp.where(rows <= t, g_s[pl.ds(t, 1), :] - g_s[...], -jnp.inf)
            a = jnp.exp(d) * k_s[...] * q_s[pl.ds(t, 1), :]
            sc = jnp.sum(a, axis=1, keepdims=True)
            oi_s[pl.ds(t, 1), :] = jnp.sum(sc * i_s[...], axis=0, keepdims=True)
            return carry

        lax.fori_loop(0, chunk, row_step, 0)
        o = o_inter + oi_s[...]
        on = _rms(o, gain)
        hg = hg_ref[pl.ds(r0, chunk), :]
        o_ref[pl.ds(r0, chunk), :] = (on * (hg * jax.nn.sigmoid(hg))).astype(BF16)
        g_end = g[chunk - 1:chunk, :]
        kd = kk * jnp.exp(g_end - g)
        upd = lax.dot_general(iv.astype(BF16), kd.astype(BF16),
                              (((0,), (0,)), ((), ())), preferred_element_type=F32)
        return st * jnp.exp(g_end) + upd

    st_ref[0, 0] = lax.fori_loop(0, length // chunk, chunk_step, s0_ref[0, 0])


def _hgrn(hgrp, lb_logits, gain, s0t, *, batch, layer):
    length = hgrp.shape[0] // batch
    chunk = _largest_block(length, 64, 16)
    col = lambda off: pl.BlockSpec((length, HG_DIM), lambda b, h: (b, off + h))
    par = lambda a: pl.BlockSpec((a.shape[0], HG_DIM), lambda b, h: (0, h))
    st = pl.BlockSpec((1, 1, HG_DIM, HG_DIM), lambda b, h: (b, h, 0, 0))
    buf = pltpu.VMEM((chunk, HG_DIM), F32)
    return pl.pallas_call(
        functools.partial(_hgrn_body, chunk=chunk, layer=layer),
        grid=(batch, HG_HEADS),
        in_specs=[col(0), col(HG_HEADS), col(2 * HG_HEADS), col(3 * HG_HEADS), par(lb_logits), par(gain), st],
        out_specs=[pl.BlockSpec((length, HG_DIM), lambda b, h: (b, h)), st],
        out_shape=[jax.ShapeDtypeStruct((batch * length, HG_WIDTH), BF16),
                   jax.ShapeDtypeStruct(s0t.shape, F32)],
        scratch_shapes=[buf, buf, buf, buf, buf],
        compiler_params=_params("parallel", "parallel"),
    )(hgrp, hgrp, hgrp, hgrp, lb_logits, gain, s0t)


def _topk_rows(s, k, payload=None):
    r = s.shape[0]
    iota = lax.broadcasted_iota(jnp.int32, s.shape, 0)
    vals, picks = [], []
    for _ in range(k):
        m = jnp.max(s, axis=0, keepdims=True)
        idx = jnp.min(jnp.where(s == m, iota, r), axis=0, keepdims=True)
        sel = iota == idx
        vals.append(m)
        picks.append(idx if payload is None else jnp.max(jnp.where(sel, payload, -1), axis=0, keepdims=True))
        s = jnp.where(sel, -jnp.inf, s)
    return jnp.concatenate(vals, axis=0), jnp.concatenate(picks, axis=0)


def _route_body(x_ref, att_ref, hgo_ref, woa_ref, wob_ref, g2_ref, wq_ref, keys_ref,
                x1_ref, h2_ref, eidx_ref, gate_ref, qp_s):
    x1 = (x_ref[...] + jnp.dot(att_ref[...], woa_ref[...], preferred_element_type=F32)
          + jnp.dot(hgo_ref[...], wob_ref[...], preferred_element_type=F32))
    x1_ref[...] = x1
    h2 = _rms(x1, g2_ref[...])
    h2_ref[...] = h2
    qp_s[...] = jnp.dot(h2.astype(BF16), wq_ref[...], preferred_element_type=F32)
    nt = (((1,), (1,)), ((), ()))

    def head(h, carry):
        c0 = pl.multiple_of(h * 2 * PEER_HALF, 2 * PEER_HALF)
        qa = qp_s[:, pl.ds(c0, PEER_HALF)].astype(BF16)
        qb = qp_s[:, pl.ds(c0 + PEER_HALF, PEER_HALF)].astype(BF16)
        sa = lax.dot_general(keys_ref[h, 0], qa, nt, preferred_element_type=F32)
        sb = lax.dot_general(keys_ref[h, 1], qb, nt, preferred_element_type=F32)
        va, ia = _topk_rows(sa, PEER_TOPK)
        vb, ib = _topk_rows(sb, PEER_TOPK)
        cand = jnp.concatenate([va[i:i + 1] + vb for i in range(PEER_TOPK)], axis=0)
        cidx = jnp.concatenate([ia[i:i + 1] * PEER_N_KEYS + ib for i in range(PEER_TOPK)], axis=0)
        ts, te = _topk_rows(cand, PEER_TOPK, payload=cidx)
        ex = jnp.exp(ts - ts[0:1])
        r0 = pl.multiple_of(h * PEER_TOPK, PEER_TOPK)
        eidx_ref[0, pl.ds(r0, PEER_TOPK), :] = te
        gate_ref[0, pl.ds(r0, PEER_TOPK), :] = ex / jnp.sum(ex, axis=0, keepdims=True)
        return carry

    lax.fori_loop(0, PEER_HEADS, head, 0)


def _route(x, att, hgo, woa, wob, g2, wq, keys):
    n = x.shape[0]
    tb = _largest_block(n, TOKEN_BLOCK_ROUTE, LANES)
    nb = n // tb
    row = lambda w: pl.BlockSpec((tb, w), lambda i: (i, 0))
    full = lambda a: pl.BlockSpec(a.shape, lambda i: (0,) * a.ndim)
    sel = pl.BlockSpec((1, PEER_SEL, tb), lambda i: (i, 0, 0))
    return pl.pallas_call(
        _route_body,
        grid=(nb,),
        in_specs=[row(D_MODEL), row(ATT_WIDTH), row(HG_WIDTH), full(woa), full(wob), full(g2), full(wq), full(keys)],
        out_specs=[row(D_MODEL), row(D_MODEL), sel, sel],
        out_shape=[jax.ShapeDtypeStruct((n, D_MODEL), F32),
                   jax.ShapeDtypeStruct((n, D_MODEL), F32),
                   jax.ShapeDtypeStruct((nb, PEER_SEL, tb), jnp.int32),
                   jax.ShapeDtypeStruct((nb, PEER_SEL, tb), F32)],
        scratch_shapes=[pltpu.VMEM((tb, PEER_HEADS * 2 * PEER_HALF), F32)],
        compiler_params=_params("parallel"),
    )(x, att, hgo, woa, wob, g2, wq, keys)


def _experts_body(eidx_ref, gate_ref, h2_ref, x1_ref, gf_ref, uv_ref, y_ref,
                  idx_s, buf, idx_sem, row_sem, *, tb):
    idx_cp = pltpu.make_async_copy(eidx_ref.at[0], idx_s, idx_sem)
    idx_cp.start()
    idx_cp.wait()

    def issue(t, slot):
        for r in range(PEER_SEL):
            pltpu.make_async_copy(uv_ref.at[pl.ds(idx_s[r, t], 1), :],
                                  buf.at[slot, pl.ds(r, 1), :], row_sem.at[slot]).start()

    def wait(slot):
        pltpu.make_async_copy(uv_ref.at[pl.ds(0, PEER_SEL), :], buf.at[slot], row_sem.at[slot]).wait()

    for j in range(GATHER_SLOTS):
        issue(j, j)

    lane_t = lax.broadcasted_iota(jnp.int32, (PEER_SEL, tb), 1)
    gf = gf_ref[...]

    def group(g, carry):
        for j in range(GATHER_SLOTS):
            t = g * GATHER_SLOTS + j
            wait(j)
            xrow = h2_ref[pl.ds(t, 1), :]
            hcol = jnp.sum(buf[j, :, :D_MODEL] * xrow, axis=1, keepdims=True)
            act = 0.5 * hcol * (1.0 + lax.erf(hcol * (2.0 ** -0.5)))
            gcol = jnp.sum(jnp.where(lane_t == t, gate_ref[0], 0.0), axis=1, keepdims=True)
            orow = jnp.sum((gcol * act) * buf[j, :, D_MODEL:], axis=0, keepdims=True)
            y_ref[pl.ds(t, 1), :] = _rms(x1_ref[pl.ds(t, 1), :] + orow, gf)

            @pl.when(t + GATHER_SLOTS < tb)
            def _():
                issue(t + GATHER_SLOTS, j)
        return carry

    lax.fori_loop(0, tb // GATHER_SLOTS, group, 0)


def _experts(eidx, gate, h2, x1, gf, uv):
    n = h2.shape[0]
    nbr, _, tbr = eidx.shape
    tb = TOKEN_BLOCK_EXPERT
    assert tbr % tb == 0 and tb % GATHER_SLOTS == 0
    per = tbr // tb
    row = pl.BlockSpec((tb, D_MODEL), lambda i: (i, 0))
    sel = pl.BlockSpec((1, PEER_SEL, tb), lambda i: (i // per, 0, i % per))
    return pl.pallas_call(
        functools.partial(_experts_body, tb=tb),
        grid=(n // tb,),
        in_specs=[sel, sel, row, row, pl.BlockSpec(gf.shape, lambda i: (0, 0)),
                  pl.BlockSpec(memory_space=pl.ANY)],
        out_specs=row,
        out_shape=jax.ShapeDtypeStruct((n, D_MODEL), F32),
        scratch_shapes=[pltpu.SMEM((PEER_SEL, tb), jnp.int32),
                        pltpu.VMEM((GATHER_SLOTS, PEER_SEL, 2 * D_MODEL), F32),
                        pltpu.SemaphoreType.DMA(()),
                        pltpu.SemaphoreType.DMA((GATHER_SLOTS,))],
        compiler_params=_params("arbitrary"),
    )(eidx, gate, h2, x1, gf, uv)


def _mixer(x, batch, params, layer, cache=None):
    q, k, v, logf, hgrp = _inproj(x, params["g1"], params["wqkv"], params["wfg"], params["bfg"], params["wh"])
    length = x.shape[0] // batch
    if cache is None:
        k_all, v_all, logf_all, q_off = k, v, logf.reshape(batch, length, ATT_HEADS), 0
        s0t = jnp.zeros((batch, HG_HEADS, HG_DIM, HG_DIM), F32)
    else:
        cache_k, cache_v, cache_logf, state = cache
        past = cache_k.shape[1]
        cat = lambda c, new: jnp.concatenate(
            [c.reshape(batch, past, -1).astype(F32), new.reshape(batch, length, -1)], axis=1)
        k_all = cat(cache_k, k).reshape(batch * (past + length), ATT_KV_WIDTH)
        v_all = cat(cache_v, v).reshape(batch * (past + length), ATT_KV_WIDTH)
        logf_all, q_off = cat(cache_logf, logf), past
        s0t = jnp.swapaxes(state.astype(F32), -1, -2)
    f_all = _cumsum(logf_all)
    f_col = f_all[:, q_off:].reshape(batch * length, ATT_HEADS)
    f_row = jnp.swapaxes(f_all, 1, 2)
    att = _attn(q, k_all, v_all, f_col, f_row, batch=batch, q_off=q_off)
    hgo, st = _hgrn(hgrp, params["lb_logits"], params["hg_gain"], s0t, batch=batch, layer=layer)
    return att, hgo, (k, v, logf, jnp.swapaxes(st, -1, -2))


def kernel(x_prompt, x_sample, cache_k, cache_v, cache_logf, state_hgrn, meta_tokens,
           w_in, b_forget, hg_lb_logits, hg_norm_gain, w_out, norm1_gain, norm2_gain,
           peer_w_query, peer_sub_keys, peer_expert_u, peer_expert_v, final_norm_gain):
    depth = w_in.shape[0]
    assert depth == 1, "single-layer trunk: the PEER stage fuses the final norm"
    bp, seq, _ = x_prompt.shape
    bs, dseq, _ = x_sample.shape
    lp = N_META + seq
    meta = jnp.broadcast_to(meta_tokens.astype(x_prompt.dtype)[None], (bp, N_META, D_MODEL))
    xp = jnp.concatenate([meta, x_prompt], axis=1).reshape(bp * lp, D_MODEL)
    xs = x_sample.reshape(bs * dseq, D_MODEL)

    l = 0
    o_fg = ATT_WIDTH + 2 * ATT_KV_WIDTH
    o_h = o_fg + ATT_HEADS
    row = lambda a: a.reshape(1, -1).astype(F32)
    params = {
        "g1": row(norm1_gain[l]),
        "wqkv": w_in[l][:, :o_fg].astype(BF16),
        "wfg": jnp.pad(w_in[l][:, o_fg:o_h], ((0, 0), (0, LANES - ATT_HEADS))).astype(BF16),
        "bfg": row(b_forget[l]),
        "wh": w_in[l][:, o_h:].astype(BF16),
        "lb_logits": hg_lb_logits.astype(F32),
        "hg_gain": row(hg_norm_gain[l]),
    }
    att_p, hgo_p, sp = _mixer(xp, bp, params, l)
    att_s, hgo_s, ss = _mixer(xs, bs, params, l,
                              cache=(cache_k[l], cache_v[l], cache_logf[l], state_hgrn[l]))

    x_all = jnp.concatenate([xp, xs], axis=0)
    att = jnp.concatenate([att_p, att_s], axis=0)
    hgo = jnp.concatenate([hgo_p, hgo_s], axis=0)
    wo = w_out[l].astype(BF16)
    x1, h2, eidx, gate = _route(x_all, att, hgo, wo[:ATT_WIDTH], wo[ATT_WIDTH:], row(norm2_gain[l]),
                                peer_w_query[l].astype(BF16), peer_sub_keys[l].astype(BF16))
    uv = jnp.concatenate([peer_expert_u[l], peer_expert_v[l]], axis=1)
    y = _experts(eidx, gate, h2, x1, row(final_norm_gain), uv)

    n_p = bp * lp
    y_prompt = y[:n_p].reshape(bp, lp, D_MODEL)[:, N_META:]
    y_sample = y[n_p:].reshape(bs, dseq, D_MODEL)

    def states(s, batch, length):
        k, v, logf, st = s
        return (k.reshape(1, batch, length, ATT_KV_HEADS, ATT_HEAD_DIM),
                v.reshape(1, batch, length, ATT_KV_HEADS, ATT_HEAD_DIM),
                logf.reshape(1, batch, length, ATT_HEADS),
                st.reshape(1, batch, HG_HEADS, HG_DIM, HG_DIM))

    return (y_prompt, y_sample) + states(sp, bp, lp) + states(ss, bs, dseq)
```

```python
import functools

import jax
import jax.numpy as jnp
from jax import lax
from jax.experimental import pallas as pl
from jax.experimental.pallas import tpu as pltpu

F32 = jnp.float32
BF16 = jnp.bfloat16
EPS = 1e-6
N_META = 16

D_MODEL = 1024
ATT_HEADS = 8
ATT_KV_HEADS = 4
ATT_HEAD_DIM = 64
ATT_GROUP = ATT_HEADS // ATT_KV_HEADS
ATT_WIDTH = ATT_HEADS * ATT_HEAD_DIM
ATT_KV_WIDTH = ATT_KV_HEADS * ATT_HEAD_DIM
HG_HEADS = 4
HG_DIM = 128
HG_WIDTH = HG_HEADS * HG_DIM
PEER_HEADS = 8
PEER_N_KEYS = 128
PEER_TOPK = 16
PEER_HALF = 128
PEER_SEL = PEER_HEADS * PEER_TOPK

LANES = 128
SUBLANES = 8
VMEM_LIMIT = 48 * 1024 * 1024

TOKEN_BLOCK_PROJ = 512
TOKEN_BLOCK_ROUTE = 256
TOKEN_BLOCK_EXPERT = 128
GATHER_SLOTS = 4


def _params(*sem):
    return pltpu.CompilerParams(dimension_semantics=sem, vmem_limit_bytes=VMEM_LIMIT)


def _largest_block(n, cap, mult):
    best = None
    for d in range(mult, cap + 1, mult):
        if n % d == 0:
            best = d
    assert best is not None, (n, cap, mult)
    return best


def _split3(x):
    hi = x.astype(BF16)
    r1 = x - hi.astype(F32)
    mid = r1.astype(BF16)
    lo = (r1 - mid.astype(F32)).astype(BF16)
    return hi, mid, lo


def _tri_cumsum(tri, x):
    hi, mid, lo = _split3(x)
    d = lambda a: jnp.dot(tri, a, preferred_element_type=F32)
    return (d(lo) + d(mid)) + d(hi)


def _lower_tri(c):
    r = lax.broadcasted_iota(jnp.int32, (c, c), 0)
    s = lax.broadcasted_iota(jnp.int32, (c, c), 1)
    return jnp.where(s <= r, 1.0, 0.0).astype(BF16)


def _rms(x, gain):
    return x * lax.rsqrt(jnp.mean(x * x, axis=-1, keepdims=True) + EPS) * gain


def _inproj_body(x_ref, g_ref, wqkv_ref, wfg_ref, bfg_ref, wh_ref,
                 q_ref, k_ref, v_ref, logf_ref, hgrp_ref):
    hb = _rms(x_ref[...], g_ref[...]).astype(BF16)
    qkv = jnp.dot(hb, wqkv_ref[...], preferred_element_type=F32)
    q_ref[...] = (qkv[:, :ATT_WIDTH] * (ATT_HEAD_DIM ** -0.5)).astype(BF16)
    k_ref[...] = qkv[:, ATT_WIDTH:ATT_WIDTH + ATT_KV_WIDTH]
    v_ref[...] = qkv[:, ATT_WIDTH + ATT_KV_WIDTH:]
    fg = jnp.dot(hb, wfg_ref[...], preferred_element_type=F32)[:, :ATT_HEADS] + bfg_ref[...]
    logf_ref[...] = jnp.minimum(fg, 0.0) - jnp.log1p(jnp.exp(-jnp.abs(fg)))
    hgrp_ref[...] = jnp.dot(hb, wh_ref[...], preferred_element_type=F32)


def _inproj(x, gain, wqkv, wfg, bfg, wh):
    n = x.shape[0]
    tb = _largest_block(n, TOKEN_BLOCK_PROJ, 8)
    row = lambda w: pl.BlockSpec((tb, w), lambda i: (i, 0))
    full = lambda a: pl.BlockSpec(a.shape, lambda i: (0,) * a.ndim)
    return pl.pallas_call(
        _inproj_body,
        grid=(n // tb,),
        in_specs=[row(D_MODEL), full(gain), full(wqkv), full(wfg), full(bfg), full(wh)],
        out_specs=[row(ATT_WIDTH), row(ATT_KV_WIDTH), row(ATT_KV_WIDTH), row(ATT_HEADS), row(4 * HG_WIDTH)],
        out_shape=[jax.ShapeDtypeStruct((n, ATT_WIDTH), BF16),
                   jax.ShapeDtypeStruct((n, ATT_KV_WIDTH), F32),
                   jax.ShapeDtypeStruct((n, ATT_KV_WIDTH), F32),
                   jax.ShapeDtypeStruct((n, ATT_HEADS), F32),
                   jax.ShapeDtypeStruct((n, 4 * HG_WIDTH), F32)],
        compiler_params=_params("parallel"),
    )(x, gain, wqkv, wfg, bfg, wh)


def _cumsum_body(x_ref, o_ref, *, cb):
    length, width = x_ref.shape[1], x_ref.shape[2]
    tri = _lower_tri(cb)
    carry = jnp.zeros((1, width), F32)
    for j in range(length // cb):
        f = _tri_cumsum(tri, x_ref[0, j * cb:(j + 1) * cb, :]) + carry
        o_ref[0, j * cb:(j + 1) * cb, :] = f
        carry = f[cb - 1:cb, :]


def _cumsum(logf):
    b, length, h = logf.shape
    cb = _largest_block(length, 512, 8)
    spec = pl.BlockSpec((1, length, h), lambda i: (i, 0, 0))
    return pl.pallas_call(
        functools.partial(_cumsum_body, cb=cb),
        grid=(b,),
        in_specs=[spec],
        out_specs=spec,
        out_shape=jax.ShapeDtypeStruct(logf.shape, F32),
        compiler_params=_params("parallel"),
    )(logf)


def _attn_body(q_ref, k_ref, v_ref, fc_ref, fr_ref, o_ref, *, q_off, tq):
    lk = k_ref.shape[0]
    qpos = q_off + pl.program_id(1) * tq + lax.broadcasted_iota(jnp.int32, (tq, lk), 0)
    kpos = lax.broadcasted_iota(jnp.int32, (tq, lk), 1)
    mask = kpos <= qpos
    kb = k_ref[...].astype(BF16)
    vb = v_ref[...].astype(BF16)
    q = q_ref[...]
    fc = fc_ref[...]
    fr = fr_ref[0]
    outs = []
    for h in range(ATT_HEADS):
        g = h // ATT_GROUP
        hs = slice(h * ATT_HEAD_DIM, (h + 1) * ATT_HEAD_DIM)
        gs = slice(g * ATT_HEAD_DIM, (g + 1) * ATT_HEAD_DIM)
        s = lax.dot_general(q[:, hs], kb[:, gs], (((1,), (1,)), ((), ())), preferred_element_type=F32)
        s = s + fc[:, h:h + 1] - fr[h:h + 1, :]
        s = jnp.where(mask, s, -jnp.inf)
        p = jnp.exp(s - jnp.max(s, axis=-1, keepdims=True))
        l = jnp.sum(p, axis=-1, keepdims=True)
        o = jnp.dot(p.astype(BF16), vb[:, gs], preferred_element_type=F32)
        outs.append(o / l)
    o_ref[...] = jnp.concatenate(outs, axis=-1).astype(BF16)


def _attn(q, k, v, f_col, f_row, *, batch, q_off):
    t_q = q.shape[0] // batch
    lk = k.shape[0] // batch
    tq = _largest_block(t_q, 384, 8)
    nq = t_q // tq
    qspec = lambda w: pl.BlockSpec((tq, w), lambda b, i: (b * nq + i, 0))
    kspec = pl.BlockSpec((lk, ATT_KV_WIDTH), lambda b, i: (b, 0))
    return pl.pallas_call(
        functools.partial(_attn_body, q_off=q_off, tq=tq),
        grid=(batch, nq),
        in_specs=[qspec(ATT_WIDTH), kspec, kspec, qspec(ATT_HEADS),
                  pl.BlockSpec((1, ATT_HEADS, lk), lambda b, i: (b, 0, 0))],
        out_specs=qspec(ATT_WIDTH),
        out_shape=jax.ShapeDtypeStruct((batch * t_q, ATT_WIDTH), BF16),
        compiler_params=_params("parallel", "parallel"),
    )(q, k, v, f_col, f_row)


def _hgrn_body(hq_ref, hf_ref, hi_ref, hg_ref, lbl_ref, gain_ref, s0_ref, o_ref, st_ref,
               oi_s, *, chunk, layer):
    length = hq_ref.shape[0]
    lbl = lbl_ref[...]
    e = jnp.exp(lbl - jnp.max(lbl, axis=0, keepdims=True))
    lb = jnp.sum(e[:layer + 1, :], axis=0, keepdims=True) / jnp.sum(e, axis=0, keepdims=True)
    gain = gain_ref[...]
    tri = _lower_tri(chunk)
    rows = lax.broadcasted_iota(jnp.int32, (chunk, 1), 0)

    def chunk_step(c, st):
        r0 = pl.multiple_of(c * chunk, 16)
        f = lb + (1.0 - lb) * jax.nn.sigmoid(hf_ref[pl.ds(r0, chunk), :])
        kk = 1.0 - f
        hq = hq_ref[pl.ds(r0, chunk), :]
        qq = hq * jax.nn.sigmoid(hq)
        iv = hi_ref[pl.ds(r0, chunk), :]
        g = _tri_cumsum(tri, jnp.log(f))
        o_inter = lax.dot_general((qq * jnp.exp(g)).astype(BF16), st.astype(BF16),
                                  (((1,), (1,)), ((), ())), preferred_element_type=F32)
        for t in range(chunk):
            n = (t // SUBLANES + 1) * SUBLANES
            d = jnp.where(rows[:n] <= t, g[t:t + 1, :] - g[:n, :], -jnp.inf)
            a = jnp.exp(d) * kk[:n, :] * qq[t:t + 1, :]
            sc = jnp.sum(a, axis=1, keepdims=True)
            oi_s[t:t + 1, :] = jnp.sum(sc * iv[:n, :], axis=0, keepdims=True)
        o = o_inter + oi_s[...]
        on = _rms(o, gain)
        hg = hg_ref[pl.ds(r0, chunk), :]
        o_ref[pl.ds(r0, chunk), :] = (on * (hg * jax.nn.sigmoid(hg))).astype(BF16)
        g_end = g[chunk - 1:chunk, :]
        kd = kk * jnp.exp(g_end - g)
        upd = lax.dot_general(iv.astype(BF16), kd.astype(BF16),
                              (((0,), (0,)), ((), ())), preferred_element_type=F32)
        return st * jnp.exp(g_end) + upd

    st_ref[0, 0] = lax.fori_loop(0, length // chunk, chunk_step, s0_ref[0, 0])


def _hgrn(hgrp, lb_logits, gain, s0t, *, batch, layer):
    length = hgrp.shape[0] // batch
    chunk = _largest_block(length, 64, 16)
    col = lambda off: pl.BlockSpec((length, HG_DIM), lambda b, h: (b, off + h))
    par = lambda a: pl.BlockSpec((a.shape[0], HG_DIM), lambda b, h: (0, h))
    st = pl.BlockSpec((1, 1, HG_DIM, HG_DIM), lambda b, h: (b, h, 0, 0))
    buf = pltpu.VMEM((chunk, HG_DIM), F32)
    return pl.pallas_call(
        functools.partial(_hgrn_body, chunk=chunk, layer=layer),
        grid=(batch, HG_HEADS),
        in_specs=[col(0), col(HG_HEADS), col(2 * HG_HEADS), col(3 * HG_HEADS), par(lb_logits), par(gain), st],
        out_specs=[pl.BlockSpec((length, HG_DIM), lambda b, h: (b, h)), st],
        out_shape=[jax.ShapeDtypeStruct((batch * length, HG_WIDTH), BF16),
                   jax.ShapeDtypeStruct(s0t.shape, F32)],
        scratch_shapes=[buf],
        compiler_params=_params("parallel", "parallel"),
    )(hgrp, hgrp, hgrp, hgrp, lb_logits, gain, s0t)


def _topk_rows(s, k, payload=None):
    r = s.shape[0]
    iota = lax.broadcasted_iota(jnp.int32, s.shape, 0)
    vals, picks = [], []
    for _ in range(k):
        m = jnp.max(s, axis=0, keepdims=True)
        idx = jnp.min(jnp.where(s == m, iota, r), axis=0, keepdims=True)
        sel = iota == idx
        vals.append(m)
        picks.append(idx if payload is None else jnp.max(jnp.where(sel, payload, -1), axis=0, keepdims=True))
        s = jnp.where(sel, -jnp.inf, s)
    return jnp.concatenate(vals, axis=0), jnp.concatenate(picks, axis=0)


def _route_body(x_ref, att_ref, hgo_ref, woa_ref, wob_ref, g2_ref, wq_ref, keys_ref,
                x1_ref, h2_ref, eidx_ref, gate_ref, qp_s):
    x1 = (x_ref[...] + jnp.dot(att_ref[...], woa_ref[...], preferred_element_type=F32)
          + jnp.dot(hgo_ref[...], wob_ref[...], preferred_element_type=F32))
    x1_ref[...] = x1
    h2 = _rms(x1, g2_ref[...])
    h2_ref[...] = h2
    qp_s[...] = jnp.dot(h2.astype(BF16), wq_ref[...], preferred_element_type=F32)
    nt = (((1,), (1,)), ((), ()))

    def head(h, carry):
        c0 = pl.multiple_of(h * 2 * PEER_HALF, 2 * PEER_HALF)
        qa = qp_s[:, pl.ds(c0, PEER_HALF)].astype(BF16)
        qb = qp_s[:, pl.ds(c0 + PEER_HALF, PEER_HALF)].astype(BF16)
        sa = lax.dot_general(keys_ref[h, 0], qa, nt, preferred_element_type=F32)
        sb = lax.dot_general(keys_ref[h, 1], qb, nt, preferred_element_type=F32)
        va, ia = _topk_rows(sa, PEER_TOPK)
        vb, ib = _topk_rows(sb, PEER_TOPK)
        cand = jnp.concatenate([va[i:i + 1] + vb for i in range(PEER_TOPK)], axis=0)
        cidx = jnp.concatenate([ia[i:i + 1] * PEER_N_KEYS + ib for i in range(PEER_TOPK)], axis=0)
        ts, te = _topk_rows(cand, PEER_TOPK, payload=cidx)
        ex = jnp.exp(ts - ts[0:1])
        r0 = pl.multiple_of(h * PEER_TOPK, PEER_TOPK)
        eidx_ref[0, pl.ds(r0, PEER_TOPK), :] = te
        gate_ref[0, pl.ds(r0, PEER_TOPK), :] = ex / jnp.sum(ex, axis=0, keepdims=True)
        return carry

    lax.fori_loop(0, PEER_HEADS, head, 0)


def _route(x, att, hgo, woa, wob, g2, wq, keys):
    n = x.shape[0]
    tb = _largest_block(n, TOKEN_BLOCK_ROUTE, LANES)
    nb = n // tb
    row = lambda w: pl.BlockSpec((tb, w), lambda i: (i, 0))
    full = lambda a: pl.BlockSpec(a.shape, lambda i: (0,) * a.ndim)
    sel = pl.BlockSpec((1, PEER_SEL, tb), lambda i: (i, 0, 0))
    return pl.pallas_call(
        _route_body,
        grid=(nb,),
        in_specs=[row(D_MODEL), row(ATT_WIDTH), row(HG_WIDTH), full(woa), full(wob), full(g2), full(wq), full(keys)],
        out_specs=[row(D_MODEL), row(D_MODEL), sel, sel],
        out_shape=[jax.ShapeDtypeStruct((n, D_MODEL), F32),
                   jax.ShapeDtypeStruct((n, D_MODEL), F32),
                   jax.ShapeDtypeStruct((nb, PEER_SEL, tb), jnp.int32),
                   jax.ShapeDtypeStruct((nb, PEER_SEL, tb), F32)],
        scratch_shapes=[pltpu.VMEM((tb, PEER_HEADS * 2 * PEER_HALF), F32)],
        compiler_params=_params("parallel"),
    )(x, att, hgo, woa, wob, g2, wq, keys)


def _experts_body(eidx_ref, gate_ref, h2_ref, x1_ref, gf_ref, uv_ref, y_ref,
                  idx_s, buf, idx_sem, row_sem, *, tb):
    idx_cp = pltpu.make_async_copy(eidx_ref.at[0], idx_s, idx_sem)
    idx_cp.start()
    idx_cp.wait()

    def issue(t, slot):
        for r in range(PEER_SEL):
            pltpu.make_async_copy(uv_ref.at[pl.ds(idx_s[r, t], 1), :],
                                  buf.at[slot, pl.ds(r, 1), :], row_sem.at[slot]).start()

    def wait(slot):
        pltpu.make_async_copy(uv_ref.at[pl.ds(0, PEER_SEL), :], buf.at[slot], row_sem.at[slot]).wait()

    for j in range(GATHER_SLOTS):
        issue(j, j)

    lane_t = lax.broadcasted_iota(jnp.int32, (PEER_SEL, tb), 1)
    gf = gf_ref[...]

    def group(g, carry):
        for j in range(GATHER_SLOTS):
            t = g * GATHER_SLOTS + j
            wait(j)
            xrow = h2_ref[pl.ds(t, 1), :]
            hcol = jnp.sum(buf[j, :, :D_MODEL] * xrow, axis=1, keepdims=True)
            act = 0.5 * hcol * (1.0 + lax.erf(hcol * (2.0 ** -0.5)))
            gcol = jnp.sum(jnp.where(lane_t == t, gate_ref[0], 0.0), axis=1, keepdims=True)
            orow = jnp.sum((gcol * act) * buf[j, :, D_MODEL:], axis=0, keepdims=True)
            y_ref[pl.ds(t, 1), :] = _rms(x1_ref[pl.ds(t, 1), :] + orow, gf)

            @pl.when(t + GATHER_SLOTS < tb)
            def _():
                issue(t + GATHER_SLOTS, j)
        return carry

    lax.fori_loop(0, tb // GATHER_SLOTS, group, 0)


def _experts(eidx, gate, h2, x1, gf, uv):
    n = h2.shape[0]
    nbr, _, tbr = eidx.shape
    tb = TOKEN_BLOCK_EXPERT
    assert tbr % tb == 0 and tb % GATHER_SLOTS == 0
    per = tbr // tb
    row = pl.BlockSpec((tb, D_MODEL), lambda i: (i, 0))
    sel = pl.BlockSpec((1, PEER_SEL, tb), lambda i: (i // per, 0, i % per))
    return pl.pallas_call(
        functools.partial(_experts_body, tb=tb),
        grid=(n // tb,),
        in_specs=[sel, sel, row, row, pl.BlockSpec(gf.shape, lambda i: (0, 0)),
                  pl.BlockSpec(memory_space=pl.ANY)],
        out_specs=row,
        out_shape=jax.ShapeDtypeStruct((n, D_MODEL), F32),
        scratch_shapes=[pltpu.SMEM((PEER_SEL, tb), jnp.int32),
                        pltpu.VMEM((GATHER_SLOTS, PEER_SEL, 2 * D_MODEL), F32),
                        pltpu.SemaphoreType.DMA(()),
                        pltpu.SemaphoreType.DMA((GATHER_SLOTS,))],
        compiler_params=_params("arbitrary"),
    )(eidx, gate, h2, x1, gf, uv)


def _mixer(x, batch, params, layer, cache=None):
    q, k, v, logf, hgrp = _inproj(x, params["g1"], params["wqkv"], params["wfg"], params["bfg"], params["wh"])
    length = x.shape[0] // batch
    if cache is None:
        k_all, v_all, logf_all, q_off = k, v, logf.reshape(batch, length, ATT_HEADS), 0
        s0t = jnp.zeros((batch, HG_HEADS, HG_DIM, HG_DIM), F32)
    else:
        cache_k, cache_v, cache_logf, state = cache
        past = cache_k.shape[1]
        cat = lambda c, new: jnp.concatenate(
            [c.reshape(batch, past, -1).astype(F32), new.reshape(batch, length, -1)], axis=1)
        k_all = cat(cache_k, k).reshape(batch * (past + length), ATT_KV_WIDTH)
        v_all = cat(cache_v, v).reshape(batch * (past + length), ATT_KV_WIDTH)
        logf_all, q_off = cat(cache_logf, logf), past
        s0t = jnp.swapaxes(state.astype(F32), -1, -2)
    f_all = _cumsum(logf_all)
    f_col = f_all[:, q_off:].reshape(batch * length, ATT_HEADS)
    f_row = jnp.swapaxes(f_all, 1, 2)
    att = _attn(q, k_all, v_all, f_col, f_row, batch=batch, q_off=q_off)
    hgo, st = _hgrn(hgrp, params["lb_logits"], params["hg_gain"], s0t, batch=batch, layer=layer)
    return att, hgo, (k, v, logf, jnp.swapaxes(st, -1, -2))


def kernel(x_prompt, x_sample, cache_k, cache_v, cache_logf, state_hgrn, meta_tokens,
           w_in, b_forget, hg_lb_logits, hg_norm_gain, w_out, norm1_gain, norm2_gain,
           peer_w_query, peer_sub_keys, peer_expert_u, peer_expert_v, final_norm_gain):
    depth = w_in.shape[0]
    assert depth == 1, "single-layer trunk: the PEER stage fuses the final norm"
    bp, seq, _ = x_prompt.shape
    bs, dseq, _ = x_sample.shape
    lp = N_META + seq
    meta = jnp.broadcast_to(meta_tokens.astype(x_prompt.dtype)[None], (bp, N_META, D_MODEL))
    xp = jnp.concatenate([meta, x_prompt], axis=1).reshape(bp * lp, D_MODEL)
    xs = x_sample.reshape(bs * dseq, D_MODEL)

    l = 0
    o_fg = ATT_WIDTH + 2 * ATT_KV_WIDTH
    o_h = o_fg + ATT_HEADS
    row = lambda a: a.reshape(1, -1).astype(F32)
    params = {
        "g1": row(norm1_gain[l]),
        "wqkv": w_in[l][:, :o_fg].astype(BF16),
        "wfg": jnp.pad(w_in[l][:, o_fg:o_h], ((0, 0), (0, LANES - ATT_HEADS))).astype(BF16),
        "bfg": row(b_forget[l]),
        "wh": w_in[l][:, o_h:].astype(BF16),
        "lb_logits": hg_lb_logits.astype(F32),
        "hg_gain": row(hg_norm_gain[l]),
    }
    att_p, hgo_p, sp = _mixer(xp, bp, params, l)
    att_s, hgo_s, ss = _mixer(xs, bs, params, l,
                              cache=(cache_k[l], cache_v[l], cache_logf[l], state_hgrn[l]))

    x_all = jnp.concatenate([xp, xs], axis=0)
    att = jnp.concatenate([att_p, att_s], axis=0)
    hgo = jnp.concatenate([hgo_p, hgo_s], axis=0)
    wo = w_out[l].astype(BF16)
    x1, h2, eidx, gate = _route(x_all, att, hgo, wo[:ATT_WIDTH], wo[ATT_WIDTH:], row(norm2_gain[l]),
                                peer_w_query[l].astype(BF16), peer_sub_keys[l].astype(BF16))
    uv = jnp.concatenate([peer_expert_u[l], peer_expert_v[l]], axis=1)
    y = _experts(eidx, gate, h2, x1, row(final_norm_gain), uv)

    n_p = bp * lp
    y_prompt = y[:n_p].reshape(bp, lp, D_MODEL)[:, N_META:]
    y_sample = y[n_p:].reshape(bs, dseq, D_MODEL)

    def states(s, batch, length):
        k, v, logf, st = s
        return (k.reshape(1, batch, length, ATT_KV_HEADS, ATT_HEAD_DIM),
                v.reshape(1, batch, length, ATT_KV_HEADS, ATT_HEAD_DIM),
                logf.reshape(1, batch, length, ATT_HEADS),
                st.reshape(1, batch, HG_HEADS, HG_DIM, HG_DIM))

    return (y_prompt, y_sample) + states(sp, bp, lp) + states(ss, bs, dseq)
```

```python
import functools

import jax
import jax.numpy as jnp
from jax import lax
from jax.experimental import pallas as pl
from jax.experimental.pallas import tpu as pltpu

F32 = jnp.float32
BF16 = jnp.bfloat16
EPS = 1e-6
N_META = 16

D_MODEL = 1024
ATT_HEADS = 8
ATT_KV_HEADS = 4
ATT_HEAD_DIM = 64
ATT_GROUP = ATT_HEADS // ATT_KV_HEADS
ATT_WIDTH = ATT_HEADS * ATT_HEAD_DIM
ATT_KV_WIDTH = ATT_KV_HEADS * ATT_HEAD_DIM
HG_HEADS = 4
HG_DIM = 128
HG_WIDTH = HG_HEADS * HG_DIM
PEER_HEADS = 8
PEER_N_KEYS = 128
PEER_TOPK = 16
PEER_HALF = 128
PEER_SEL = PEER_HEADS * PEER_TOPK

LANES = 128
SUBLANES = 8
EXPERT_ROWS = 2 * SUBLANES
VMEM_LIMIT = 48 * 1024 * 1024

TOKEN_BLOCK_PROJ = 512
TOKEN_BLOCK_ROUTE = 256
TOKEN_BLOCK_EXPERT = 128
GATHER_GROUP = 4
GATHER_SETS = 3


def _params(*sem):
    return pltpu.CompilerParams(dimension_semantics=sem, vmem_limit_bytes=VMEM_LIMIT)


def _largest_block(n, cap, mult):
    best = None
    for d in range(mult, cap + 1, mult):
        if n % d == 0:
            best = d
    assert best is not None, (n, cap, mult)
    return best


def _split3(x):
    hi = x.astype(BF16)
    r1 = x - hi.astype(F32)
    mid = r1.astype(BF16)
    lo = (r1 - mid.astype(F32)).astype(BF16)
    return hi, mid, lo


def _tri_cumsum(tri, x):
    hi, mid, lo = _split3(x)
    d = lambda a: jnp.dot(tri, a, preferred_element_type=F32)
    return (d(lo) + d(mid)) + d(hi)


def _lower_tri(c):
    r = lax.broadcasted_iota(jnp.int32, (c, c), 0)
    s = lax.broadcasted_iota(jnp.int32, (c, c), 1)
    return jnp.where(s <= r, 1.0, 0.0).astype(BF16)


def _rms(x, gain):
    return x * lax.rsqrt(jnp.mean(x * x, axis=-1, keepdims=True) + EPS) * gain


def _inproj_body(x_ref, g_ref, wqkv_ref, wfg_ref, bfg_ref, wh_ref,
                 q_ref, k_ref, v_ref, logf_ref, hgrp_ref):
    hb = _rms(x_ref[...], g_ref[...]).astype(BF16)
    qkv = jnp.dot(hb, wqkv_ref[...], preferred_element_type=F32)
    q_ref[...] = (qkv[:, :ATT_WIDTH] * (ATT_HEAD_DIM ** -0.5)).astype(BF16)
    k_ref[...] = qkv[:, ATT_WIDTH:ATT_WIDTH + ATT_KV_WIDTH]
    v_ref[...] = qkv[:, ATT_WIDTH + ATT_KV_WIDTH:]
    fg = jnp.dot(hb, wfg_ref[...], preferred_element_type=F32)[:, :ATT_HEADS] + bfg_ref[...]
    logf_ref[...] = jnp.minimum(fg, 0.0) - jnp.log1p(jnp.exp(-jnp.abs(fg)))
    hgrp_ref[...] = jnp.dot(hb, wh_ref[...], preferred_element_type=F32)


def _inproj(x, gain, wqkv, wfg, bfg, wh):
    n = x.shape[0]
    tb = _largest_block(n, TOKEN_BLOCK_PROJ, 8)
    row = lambda w: pl.BlockSpec((tb, w), lambda i: (i, 0))
    full = lambda a: pl.BlockSpec(a.shape, lambda i: (0,) * a.ndim)
    return pl.pallas_call(
        _inproj_body,
        grid=(n // tb,),
        in_specs=[row(D_MODEL), full(gain), full(wqkv), full(wfg), full(bfg), full(wh)],
        out_specs=[row(ATT_WIDTH), row(ATT_KV_WIDTH), row(ATT_KV_WIDTH), row(ATT_HEADS), row(4 * HG_WIDTH)],
        out_shape=[jax.ShapeDtypeStruct((n, ATT_WIDTH), BF16),
                   jax.ShapeDtypeStruct((n, ATT_KV_WIDTH), F32),
                   jax.ShapeDtypeStruct((n, ATT_KV_WIDTH), F32),
                   jax.ShapeDtypeStruct((n, ATT_HEADS), F32),
                   jax.ShapeDtypeStruct((n, 4 * HG_WIDTH), F32)],
        compiler_params=_params("parallel"),
    )(x, gain, wqkv, wfg, bfg, wh)


def _cumsum_body(x_ref, o_ref, *, cb):
    length, width = x_ref.shape[1], x_ref.shape[2]
    tri = _lower_tri(cb)
    carry = jnp.zeros((1, width), F32)
    for j in range(length // cb):
        f = _tri_cumsum(tri, x_ref[0, j * cb:(j + 1) * cb, :]) + carry
        o_ref[0, j * cb:(j + 1) * cb, :] = f
        carry = f[cb - 1:cb, :]


def _cumsum(logf):
    b, length, h = logf.shape
    cb = _largest_block(length, 512, 8)
    spec = pl.BlockSpec((1, length, h), lambda i: (i, 0, 0))
    return pl.pallas_call(
        functools.partial(_cumsum_body, cb=cb),
        grid=(b,),
        in_specs=[spec],
        out_specs=spec,
        out_shape=jax.ShapeDtypeStruct(logf.shape, F32),
        compiler_params=_params("parallel"),
    )(logf)


def _attn_body(q_ref, k_ref, v_ref, fc_ref, fr_ref, o_ref, *, q_off, tq):
    lk = k_ref.shape[0]
    qpos = q_off + pl.program_id(1) * tq + lax.broadcasted_iota(jnp.int32, (tq, lk), 0)
    kpos = lax.broadcasted_iota(jnp.int32, (tq, lk), 1)
    mask = kpos <= qpos
    kb = k_ref[...].astype(BF16)
    vb = v_ref[...].astype(BF16)
    q = q_ref[...]
    fc = fc_ref[...]
    fr = fr_ref[0]
    outs = []
    for h in range(ATT_HEADS):
        g = h // ATT_GROUP
        hs = slice(h * ATT_HEAD_DIM, (h + 1) * ATT_HEAD_DIM)
        gs = slice(g * ATT_HEAD_DIM, (g + 1) * ATT_HEAD_DIM)
        s = lax.dot_general(q[:, hs], kb[:, gs], (((1,), (1,)), ((), ())), preferred_element_type=F32)
        s = s + fc[:, h:h + 1] - fr[h:h + 1, :]
        s = jnp.where(mask, s, -jnp.inf)
        p = jnp.exp(s - jnp.max(s, axis=-1, keepdims=True))
        l = jnp.sum(p, axis=-1, keepdims=True)
        o = jnp.dot(p.astype(BF16), vb[:, gs], preferred_element_type=F32)
        outs.append(o / l)
    o_ref[...] = jnp.concatenate(outs, axis=-1).astype(BF16)


def _attn(q, k, v, f_col, f_row, *, batch, q_off):
    t_q = q.shape[0] // batch
    lk = k.shape[0] // batch
    tq = _largest_block(t_q, 384, 8)
    nq = t_q // tq
    qspec = lambda w: pl.BlockSpec((tq, w), lambda b, i: (b * nq + i, 0))
    kspec = pl.BlockSpec((lk, ATT_KV_WIDTH), lambda b, i: (b, 0))
    return pl.pallas_call(
        functools.partial(_attn_body, q_off=q_off, tq=tq),
        grid=(batch, nq),
        in_specs=[qspec(ATT_WIDTH), kspec, kspec, qspec(ATT_HEADS),
                  pl.BlockSpec((1, ATT_HEADS, lk), lambda b, i: (b, 0, 0))],
        out_specs=qspec(ATT_WIDTH),
        out_shape=jax.ShapeDtypeStruct((batch * t_q, ATT_WIDTH), BF16),
        compiler_params=_params("parallel", "parallel"),
    )(q, k, v, f_col, f_row)


def _hgrn_body(hq_ref, hf_ref, hi_ref, hg_ref, lbl_ref, gain_ref, s0_ref, o_ref, st_ref,
               oi_s, *, chunk, layer):
    length = hq_ref.shape[0]
    lbl = lbl_ref[...]
    e = jnp.exp(lbl - jnp.max(lbl, axis=0, keepdims=True))
    lb = jnp.sum(e[:layer + 1, :], axis=0, keepdims=True) / jnp.sum(e, axis=0, keepdims=True)
    gain = gain_ref[...]
    tri = _lower_tri(chunk)
    rows = lax.broadcasted_iota(jnp.int32, (chunk, 1), 0)

    def chunk_step(c, st):
        r0 = pl.multiple_of(c * chunk, 16)
        f = lb + (1.0 - lb) * jax.nn.sigmoid(hf_ref[pl.ds(r0, chunk), :])
        kk = 1.0 - f
        hq = hq_ref[pl.ds(r0, chunk), :]
        qq = hq * jax.nn.sigmoid(hq)
        iv = hi_ref[pl.ds(r0, chunk), :]
        g = _tri_cumsum(tri, jnp.log(f))
        o_inter = lax.dot_general((qq * jnp.exp(g)).astype(BF16), st.astype(BF16),
                                  (((1,), (1,)), ((), ())), preferred_element_type=F32)
        for t in range(chunk):
            n = (t // SUBLANES + 1) * SUBLANES
            d = jnp.where(rows[:n] <= t, g[t:t + 1, :] - g[:n, :], -jnp.inf)
            a = jnp.exp(d) * kk[:n, :] * qq[t:t + 1, :]
            sc = jnp.sum(a, axis=1, keepdims=True)
            oi_s[t:t + 1, :] = jnp.sum(sc * iv[:n, :], axis=0, keepdims=True)
        o = o_inter + oi_s[...]
        on = _rms(o, gain)
        hg = hg_ref[pl.ds(r0, chunk), :]
        o_ref[pl.ds(r0, chunk), :] = (on * (hg * jax.nn.sigmoid(hg))).astype(BF16)
        g_end = g[chunk - 1:chunk, :]
        kd = kk * jnp.exp(g_end - g)
        upd = lax.dot_general(iv.astype(BF16), kd.astype(BF16),
                              (((0,), (0,)), ((), ())), preferred_element_type=F32)
        return st * jnp.exp(g_end) + upd

    st_ref[0, 0] = lax.fori_loop(0, length // chunk, chunk_step, s0_ref[0, 0])


def _hgrn(hgrp, lb_logits, gain, s0t, *, batch, layer):
    length = hgrp.shape[0] // batch
    chunk = _largest_block(length, 64, 16)
    col = lambda off: pl.BlockSpec((length, HG_DIM), lambda b, h: (b, off + h))
    par = lambda a: pl.BlockSpec((a.shape[0], HG_DIM), lambda b, h: (0, h))
    st = pl.BlockSpec((1, 1, HG_DIM, HG_DIM), lambda b, h: (b, h, 0, 0))
    buf = pltpu.VMEM((chunk, HG_DIM), F32)
    return pl.pallas_call(
        functools.partial(_hgrn_body, chunk=chunk, layer=layer),
        grid=(batch, HG_HEADS),
        in_specs=[col(0), col(HG_HEADS), col(2 * HG_HEADS), col(3 * HG_HEADS), par(lb_logits), par(gain), st],
        out_specs=[pl.BlockSpec((length, HG_DIM), lambda b, h: (b, h)), st],
        out_shape=[jax.ShapeDtypeStruct((batch * length, HG_WIDTH), BF16),
                   jax.ShapeDtypeStruct(s0t.shape, F32)],
        scratch_shapes=[buf],
        compiler_params=_params("parallel", "parallel"),
    )(hgrp, hgrp, hgrp, hgrp, lb_logits, gain, s0t)


def _topk_rows(s, k, payload=None):
    r = s.shape[0]
    iota = lax.broadcasted_iota(jnp.int32, s.shape, 0)
    vals, picks = [], []
    for _ in range(k):
        m = jnp.max(s, axis=0, keepdims=True)
        idx = jnp.min(jnp.where(s == m, iota, r), axis=0, keepdims=True)
        sel = iota == idx
        vals.append(m)
        picks.append(idx if payload is None else jnp.max(jnp.where(sel, payload, -1), axis=0, keepdims=True))
        s = jnp.where(sel, -jnp.inf, s)
    return jnp.concatenate(vals, axis=0), jnp.concatenate(picks, axis=0)


def _route_body(x_ref, att_ref, hgo_ref, woa_ref, wob_ref, g2_ref, wq_ref, keys_ref,
                x1_ref, h2_ref, eidx_ref, gate_ref, qp_s):
    x1 = (x_ref[...] + jnp.dot(att_ref[...], woa_ref[...], preferred_element_type=F32)
          + jnp.dot(hgo_ref[...], wob_ref[...], preferred_element_type=F32))
    x1_ref[...] = x1
    h2 = _rms(x1, g2_ref[...])
    h2_ref[...] = h2
    qp_s[...] = jnp.dot(h2.astype(BF16), wq_ref[...], preferred_element_type=F32)
    nt = (((1,), (1,)), ((), ()))

    def head(h, carry):
        c0 = pl.multiple_of(h * 2 * PEER_HALF, 2 * PEER_HALF)
        qa = qp_s[:, pl.ds(c0, PEER_HALF)].astype(BF16)
        qb = qp_s[:, pl.ds(c0 + PEER_HALF, PEER_HALF)].astype(BF16)
        sa = lax.dot_general(keys_ref[h, 0], qa, nt, preferred_element_type=F32)
        sb = lax.dot_general(keys_ref[h, 1], qb, nt, preferred_element_type=F32)
        va, ia = _topk_rows(sa, PEER_TOPK)
        vb, ib = _topk_rows(sb, PEER_TOPK)
        cand = jnp.concatenate([va[i:i + 1] + vb for i in range(PEER_TOPK)], axis=0)
        cidx = jnp.concatenate([ia[i:i + 1] * PEER_N_KEYS + ib for i in range(PEER_TOPK)], axis=0)
        ts, te = _topk_rows(cand, PEER_TOPK, payload=cidx)
        ex = jnp.exp(ts - ts[0:1])
        r0 = pl.multiple_of(h * PEER_TOPK, PEER_TOPK)
        eidx_ref[0, pl.ds(r0, PEER_TOPK), :] = te
        gate_ref[0, pl.ds(r0, PEER_TOPK), :] = ex / jnp.sum(ex, axis=0, keepdims=True)
        return carry

    lax.fori_loop(0, PEER_HEADS, head, 0)


def _route(x, att, hgo, woa, wob, g2, wq, keys):
    n = x.shape[0]
    tb = _largest_block(n, TOKEN_BLOCK_ROUTE, LANES)
    nb = n // tb
    row = lambda w: pl.BlockSpec((tb, w), lambda i: (i, 0))
    full = lambda a: pl.BlockSpec(a.shape, lambda i: (0,) * a.ndim)
    sel = pl.BlockSpec((1, PEER_SEL, tb), lambda i: (i, 0, 0))
    return pl.pallas_call(
        _route_body,
        grid=(nb,),
        in_specs=[row(D_MODEL), row(ATT_WIDTH), row(HG_WIDTH), full(woa), full(wob), full(g2), full(wq), full(keys)],
        out_specs=[row(D_MODEL), row(D_MODEL), sel, sel],
        out_shape=[jax.ShapeDtypeStruct((n, D_MODEL), F32),
                   jax.ShapeDtypeStruct((n, D_MODEL), F32),
                   jax.ShapeDtypeStruct((nb, PEER_SEL, tb), jnp.int32),
                   jax.ShapeDtypeStruct((nb, PEER_SEL, tb), F32)],
        scratch_shapes=[pltpu.VMEM((tb, PEER_HEADS * 2 * PEER_HALF), F32)],
        compiler_params=_params("parallel"),
    )(x, att, hgo, woa, wob, g2, wq, keys)


def _experts_body(eidx_ref, gate_ref, h2_ref, x1_ref, gf_ref, uv_ref, y_ref,
                  idx_s, buf0, buf1, buf2, idx_sem, row_sem, *, tb):
    bufs = (buf0, buf1, buf2)
    idx_cp = pltpu.make_async_copy(eidx_ref, idx_s, idx_sem)
    idx_cp.start()
    idx_cp.wait()

    def issue_group(g, slot_set):
        for j in range(GATHER_GROUP):
            t = g * GATHER_GROUP + j
            for r in range(PEER_SEL):
                pltpu.make_async_copy(uv_ref.at[idx_s[t, r]], bufs[slot_set].at[j, :, r, :],
                                      row_sem.at[slot_set, j]).start(priority=r % 2)

    def wait(slot_set, j):
        dst = bufs[slot_set].at[j]
        pltpu.make_async_copy(dst, dst, row_sem.at[slot_set, j]).wait()

    gf = gf_ref[...]
    lane_t = lax.broadcasted_iota(jnp.int32, (PEER_SEL, tb), 1)
    n_groups = tb // GATHER_GROUP

    def group(g, slot_set, prefetch):
        for j in range(GATHER_GROUP):
            wait(slot_set, j)
        if prefetch:
            issue_group(g + GATHER_SETS - 1, (slot_set + GATHER_SETS - 1) % GATHER_SETS)
        buf = bufs[slot_set]
        for j in range(GATHER_GROUP):
            t = g * GATHER_GROUP + j
            xrow = h2_ref[pl.ds(t, 1), :]
            p = buf[j, 0] * xrow[:, :LANES]
            for s in range(1, SUBLANES):
                p = p + buf[j, s] * xrow[:, s * LANES:(s + 1) * LANES]
            hcol = jnp.sum(p, axis=1, keepdims=True)
            act = 0.5 * hcol * (1.0 + lax.erf(hcol * (2.0 ** -0.5)))
            gcol = jnp.sum(jnp.where(lane_t == t, gate_ref[0], 0.0), axis=1, keepdims=True)
            w = gcol * act
            o = [jnp.sum(w * buf[j, SUBLANES + s], axis=0, keepdims=True) for s in range(SUBLANES)]
            orow = jnp.concatenate(o, axis=1)
            y_ref[pl.ds(t, 1), :] = _rms(x1_ref[pl.ds(t, 1), :] + orow, gf)

    for s in range(GATHER_SETS - 1):
        issue_group(s, s)
    n_main = (n_groups - (GATHER_SETS - 1)) // GATHER_SETS

    def main(k, carry):
        for s in range(GATHER_SETS):
            group(k * GATHER_SETS + s, s, True)
        return carry

    lax.fori_loop(0, n_main, main, 0)
    for g in range(n_main * GATHER_SETS, n_groups):
        group(g, g % GATHER_SETS, g + GATHER_SETS - 1 < n_groups)


def _experts(eidx, gate, h2, x1, gf, uv):
    n = h2.shape[0]
    tb = TOKEN_BLOCK_EXPERT
    tbr = gate.shape[2]
    assert GATHER_SETS == 3 and tbr % tb == 0 and tb % GATHER_GROUP == 0 and tb // GATHER_GROUP >= GATHER_SETS
    per = tbr // tb
    row = lambda w: pl.BlockSpec((tb, w), lambda i: (i, 0))
    sel = pl.BlockSpec((1, PEER_SEL, tb), lambda i: (i // per, 0, i % per))
    slots = pltpu.VMEM((GATHER_GROUP, EXPERT_ROWS, PEER_SEL, LANES), F32)
    return pl.pallas_call(
        functools.partial(_experts_body, tb=tb),
        grid=(n // tb,),
        in_specs=[row(PEER_SEL), sel, row(D_MODEL), row(D_MODEL), pl.BlockSpec(gf.shape, lambda i: (0, 0)),
                  pl.BlockSpec(memory_space=pl.ANY)],
        out_specs=row(D_MODEL),
        out_shape=jax.ShapeDtypeStruct((n, D_MODEL), F32),
        scratch_shapes=[pltpu.SMEM((tb, PEER_SEL), jnp.int32), slots, slots, slots,
                        pltpu.SemaphoreType.DMA(()),
                        pltpu.SemaphoreType.DMA((GATHER_SETS, GATHER_GROUP))],
        compiler_params=_params("arbitrary"),
    )(eidx, gate, h2, x1, gf, uv)


def _mixer(x, batch, params, layer, cache=None):
    q, k, v, logf, hgrp = _inproj(x, params["g1"], params["wqkv"], params["wfg"], params["bfg"], params["wh"])
    length = x.shape[0] // batch
    if cache is None:
        k_all, v_all, logf_all, q_off = k, v, logf.reshape(batch, length, ATT_HEADS), 0
        s0t = jnp.zeros((batch, HG_HEADS, HG_DIM, HG_DIM), F32)
    else:
        cache_k, cache_v, cache_logf, state = cache
        past = cache_k.shape[1]
        cat = lambda c, new: jnp.concatenate(
            [c.reshape(batch, past, -1).astype(F32), new.reshape(batch, length, -1)], axis=1)
        k_all = cat(cache_k, k).reshape(batch * (past + length), ATT_KV_WIDTH)
        v_all = cat(cache_v, v).reshape(batch * (past + length), ATT_KV_WIDTH)
        logf_all, q_off = cat(cache_logf, logf), past
        s0t = jnp.swapaxes(state.astype(F32), -1, -2)
    f_all = _cumsum(logf_all)
    f_col = f_all[:, q_off:].reshape(batch * length, ATT_HEADS)
    f_row = jnp.swapaxes(f_all, 1, 2)
    att = _attn(q, k_all, v_all, f_col, f_row, batch=batch, q_off=q_off)
    hgo, st = _hgrn(hgrp, params["lb_logits"], params["hg_gain"], s0t, batch=batch, layer=layer)
    return att, hgo, (k, v, logf, jnp.swapaxes(st, -1, -2))


def kernel(x_prompt, x_sample, cache_k, cache_v, cache_logf, state_hgrn, meta_tokens,
           w_in, b_forget, hg_lb_logits, hg_norm_gain, w_out, norm1_gain, norm2_gain,
           peer_w_query, peer_sub_keys, peer_expert_u, peer_expert_v, final_norm_gain):
    depth = w_in.shape[0]
    assert depth == 1, "single-layer trunk: the PEER stage fuses the final norm"
    bp, seq, _ = x_prompt.shape
    bs, dseq, _ = x_sample.shape
    lp = N_META + seq
    meta = jnp.broadcast_to(meta_tokens.astype(x_prompt.dtype)[None], (bp, N_META, D_MODEL))
    xp = jnp.concatenate([meta, x_prompt], axis=1).reshape(bp * lp, D_MODEL)
    xs = x_sample.reshape(bs * dseq, D_MODEL)

    l = 0
    o_fg = ATT_WIDTH + 2 * ATT_KV_WIDTH
    o_h = o_fg + ATT_HEADS
    row = lambda a: a.reshape(1, -1).astype(F32)
    params = {
        "g1": row(norm1_gain[l]),
        "wqkv": w_in[l][:, :o_fg].astype(BF16),
        "wfg": jnp.pad(w_in[l][:, o_fg:o_h], ((0, 0), (0, LANES - ATT_HEADS))).astype(BF16),
        "bfg": row(b_forget[l]),
        "wh": w_in[l][:, o_h:].astype(BF16),
        "lb_logits": hg_lb_logits.astype(F32),
        "hg_gain": row(hg_norm_gain[l]),
    }
    att_p, hgo_p, sp = _mixer(xp, bp, params, l)
    att_s, hgo_s, ss = _mixer(xs, bs, params, l,
                              cache=(cache_k[l], cache_v[l], cache_logf[l], state_hgrn[l]))

    x_all = jnp.concatenate([xp, xs], axis=0)
    att = jnp.concatenate([att_p, att_s], axis=0)
    hgo = jnp.concatenate([hgo_p, hgo_s], axis=0)
    wo = w_out[l].astype(BF16)
    x1, h2, eidx, gate = _route(x_all, att, hgo, wo[:ATT_WIDTH], wo[ATT_WIDTH:], row(norm2_gain[l]),
                                peer_w_query[l].astype(BF16), peer_sub_keys[l].astype(BF16))
    n_exp = peer_expert_u.shape[1]
    slab = lambda w: w.reshape(n_exp, SUBLANES, LANES)
    uv = jnp.concatenate([slab(peer_expert_u[l]), slab(peer_expert_v[l])], axis=1)
    eidx_t = jnp.swapaxes(eidx, 1, 2).reshape(-1, PEER_SEL)
    y = _experts(eidx_t, gate, h2, x1, row(final_norm_gain), uv)

    n_p = bp * lp
    y_prompt = y[:n_p].reshape(bp, lp, D_MODEL)[:, N_META:]
    y_sample = y[n_p:].reshape(bs, dseq, D_MODEL)

    def states(s, batch, length):
        k, v, logf, st = s
        return (k.reshape(1, batch, length, ATT_KV_HEADS, ATT_HEAD_DIM),
                v.reshape(1, batch, length, ATT_KV_HEADS, ATT_HEAD_DIM),
                logf.reshape(1, batch, length, ATT_HEADS),
                st.reshape(1, batch, HG_HEADS, HG_DIM, HG_DIM))

    return (y_prompt, y_sample) + states(sp, bp, lp) + states(ss, bs, dseq)
```

```python
import functools

import jax
import jax.numpy as jnp
from jax import lax
from jax.experimental import pallas as pl
from jax.experimental.pallas import tpu as pltpu

F32 = jnp.float32
BF16 = jnp.bfloat16
EPS = 1e-6
N_META = 16

D_MODEL = 1024
ATT_HEADS = 8
ATT_KV_HEADS = 4
ATT_HEAD_DIM = 64
ATT_GROUP = ATT_HEADS // ATT_KV_HEADS
ATT_WIDTH = ATT_HEADS * ATT_HEAD_DIM
ATT_KV_WIDTH = ATT_KV_HEADS * ATT_HEAD_DIM
HG_HEADS = 4
HG_DIM = 128
HG_WIDTH = HG_HEADS * HG_DIM
PEER_HEADS = 8
PEER_N_KEYS = 128
PEER_TOPK = 16
PEER_HALF = 128
PEER_SEL = PEER_HEADS * PEER_TOPK

LANES = 128
SUBLANES = 8
VMEM_LIMIT = 48 * 1024 * 1024

TOKEN_BLOCK_PROJ = 512
TOKEN_BLOCK_ROUTE = 256
TOKEN_BLOCK_EXPERT = 128
GATHER_GROUP = 4
GATHER_SETS = 3


def _params(*sem):
    return pltpu.CompilerParams(dimension_semantics=sem, vmem_limit_bytes=VMEM_LIMIT)


def _largest_block(n, cap, mult):
    best = None
    for d in range(mult, cap + 1, mult):
        if n % d == 0:
            best = d
    assert best is not None, (n, cap, mult)
    return best


def _split3(x):
    hi = x.astype(BF16)
    r1 = x - hi.astype(F32)
    mid = r1.astype(BF16)
    lo = (r1 - mid.astype(F32)).astype(BF16)
    return hi, mid, lo


def _tri_cumsum(tri, x):
    hi, mid, lo = _split3(x)
    d = lambda a: jnp.dot(tri, a, preferred_element_type=F32)
    return (d(lo) + d(mid)) + d(hi)


def _lower_tri(c):
    r = lax.broadcasted_iota(jnp.int32, (c, c), 0)
    s = lax.broadcasted_iota(jnp.int32, (c, c), 1)
    return jnp.where(s <= r, 1.0, 0.0).astype(BF16)


def _rms(x, gain):
    return x * lax.rsqrt(jnp.mean(x * x, axis=-1, keepdims=True) + EPS) * gain


def _inproj_body(x_ref, g_ref, wqkv_ref, wfg_ref, bfg_ref, wh_ref,
                 q_ref, k_ref, v_ref, logf_ref, hgrp_ref):
    hb = _rms(x_ref[...], g_ref[...]).astype(BF16)
    qkv = jnp.dot(hb, wqkv_ref[...], preferred_element_type=F32)
    q_ref[...] = (qkv[:, :ATT_WIDTH] * (ATT_HEAD_DIM ** -0.5)).astype(BF16)
    k_ref[...] = qkv[:, ATT_WIDTH:ATT_WIDTH + ATT_KV_WIDTH]
    v_ref[...] = qkv[:, ATT_WIDTH + ATT_KV_WIDTH:]
    fg = jnp.dot(hb, wfg_ref[...], preferred_element_type=F32)[:, :ATT_HEADS] + bfg_ref[...]
    logf_ref[...] = jnp.minimum(fg, 0.0) - jnp.log1p(jnp.exp(-jnp.abs(fg)))
    hgrp_ref[...] = jnp.dot(hb, wh_ref[...], preferred_element_type=F32)


def _inproj(x, gain, wqkv, wfg, bfg, wh):
    n = x.shape[0]
    tb = _largest_block(n, TOKEN_BLOCK_PROJ, 8)
    row = lambda w: pl.BlockSpec((tb, w), lambda i: (i, 0))
    full = lambda a: pl.BlockSpec(a.shape, lambda i: (0,) * a.ndim)
    return pl.pallas_call(
        _inproj_body,
        grid=(n // tb,),
        in_specs=[row(D_MODEL), full(gain), full(wqkv), full(wfg), full(bfg), full(wh)],
        out_specs=[row(ATT_WIDTH), row(ATT_KV_WIDTH), row(ATT_KV_WIDTH), row(ATT_HEADS), row(4 * HG_WIDTH)],
        out_shape=[jax.ShapeDtypeStruct((n, ATT_WIDTH), BF16),
                   jax.ShapeDtypeStruct((n, ATT_KV_WIDTH), F32),
                   jax.ShapeDtypeStruct((n, ATT_KV_WIDTH), F32),
                   jax.ShapeDtypeStruct((n, ATT_HEADS), F32),
                   jax.ShapeDtypeStruct((n, 4 * HG_WIDTH), F32)],
        compiler_params=_params("parallel"),
    )(x, gain, wqkv, wfg, bfg, wh)


def _cumsum_body(x_ref, o_ref, *, cb):
    length, width = x_ref.shape[1], x_ref.shape[2]
    tri = _lower_tri(cb)
    carry = jnp.zeros((1, width), F32)
    for j in range(length // cb):
        f = _tri_cumsum(tri, x_ref[0, j * cb:(j + 1) * cb, :]) + carry
        o_ref[0, j * cb:(j + 1) * cb, :] = f
        carry = f[cb - 1:cb, :]


def _cumsum(logf):
    b, length, h = logf.shape
    cb = _largest_block(length, 512, 8)
    spec = pl.BlockSpec((1, length, h), lambda i: (i, 0, 0))
    return pl.pallas_call(
        functools.partial(_cumsum_body, cb=cb),
        grid=(b,),
        in_specs=[spec],
        out_specs=spec,
        out_shape=jax.ShapeDtypeStruct(logf.shape, F32),
        compiler_params=_params("parallel"),
    )(logf)


def _attn_body(q_ref, k_ref, v_ref, fc_ref, fr_ref, o_ref, *, q_off, tq):
    lk = k_ref.shape[0]
    qpos = q_off + pl.program_id(1) * tq + lax.broadcasted_iota(jnp.int32, (tq, lk), 0)
    kpos = lax.broadcasted_iota(jnp.int32, (tq, lk), 1)
    mask = kpos <= qpos
    kb = k_ref[...].astype(BF16)
    vb = v_ref[...].astype(BF16)
    q = q_ref[...]
    fc = fc_ref[...]
    fr = fr_ref[0]
    outs = []
    for h in range(ATT_HEADS):
        g = h // ATT_GROUP
        hs = slice(h * ATT_HEAD_DIM, (h + 1) * ATT_HEAD_DIM)
        gs = slice(g * ATT_HEAD_DIM, (g + 1) * ATT_HEAD_DIM)
        s = lax.dot_general(q[:, hs], kb[:, gs], (((1,), (1,)), ((), ())), preferred_element_type=F32)
        s = s + fc[:, h:h + 1] - fr[h:h + 1, :]
        s = jnp.where(mask, s, -jnp.inf)
        p = jnp.exp(s - jnp.max(s, axis=-1, keepdims=True))
        l = jnp.sum(p, axis=-1, keepdims=True)
        o = jnp.dot(p.astype(BF16), vb[:, gs], preferred_element_type=F32)
        outs.append(o / l)
    o_ref[...] = jnp.concatenate(outs, axis=-1).astype(BF16)


def _attn(q, k, v, f_col, f_row, *, batch, q_off):
    t_q = q.shape[0] // batch
    lk = k.shape[0] // batch
    tq = _largest_block(t_q, 384, 8)
    nq = t_q // tq
    qspec = lambda w: pl.BlockSpec((tq, w), lambda b, i: (b * nq + i, 0))
    kspec = pl.BlockSpec((lk, ATT_KV_WIDTH), lambda b, i: (b, 0))
    return pl.pallas_call(
        functools.partial(_attn_body, q_off=q_off, tq=tq),
        grid=(batch, nq),
        in_specs=[qspec(ATT_WIDTH), kspec, kspec, qspec(ATT_HEADS),
                  pl.BlockSpec((1, ATT_HEADS, lk), lambda b, i: (b, 0, 0))],
        out_specs=qspec(ATT_WIDTH),
        out_shape=jax.ShapeDtypeStruct((batch * t_q, ATT_WIDTH), BF16),
        compiler_params=_params("parallel", "parallel"),
    )(q, k, v, f_col, f_row)


def _hgrn_body(hq_ref, hf_ref, hi_ref, hg_ref, lbl_ref, gain_ref, s0_ref, o_ref, st_ref,
               oi_s, *, chunk, layer):
    length = hq_ref.shape[0]
    lbl = lbl_ref[...]
    e = jnp.exp(lbl - jnp.max(lbl, axis=0, keepdims=True))
    lb = jnp.sum(e[:layer + 1, :], axis=0, keepdims=True) / jnp.sum(e, axis=0, keepdims=True)
    gain = gain_ref[...]
    tri = _lower_tri(chunk)
    rows = lax.broadcasted_iota(jnp.int32, (chunk, 1), 0)

    def chunk_step(c, st):
        r0 = pl.multiple_of(c * chunk, 16)
        f = lb + (1.0 - lb) * jax.nn.sigmoid(hf_ref[pl.ds(r0, chunk), :])
        kk = 1.0 - f
        hq = hq_ref[pl.ds(r0, chunk), :]
        qq = hq * jax.nn.sigmoid(hq)
        iv = hi_ref[pl.ds(r0, chunk), :]
        g = _tri_cumsum(tri, jnp.log(f))
        o_inter = lax.dot_general((qq * jnp.exp(g)).astype(BF16), st.astype(BF16),
                                  (((1,), (1,)), ((), ())), preferred_element_type=F32)
        for t in range(chunk):
            n = (t // SUBLANES + 1) * SUBLANES
            d = jnp.where(rows[:n] <= t, g[t:t + 1, :] - g[:n, :], -jnp.inf)
            a = jnp.exp(d) * kk[:n, :] * qq[t:t + 1, :]
            sc = jnp.sum(a, axis=1, keepdims=True)
            oi_s[t:t + 1, :] = jnp.sum(sc * iv[:n, :], axis=0, keepdims=True)
        o = o_inter + oi_s[...]
        on = _rms(o, gain)
        hg = hg_ref[pl.ds(r0, chunk), :]
        o_ref[pl.ds(r0, chunk), :] = (on * (hg * jax.nn.sigmoid(hg))).astype(BF16)
        g_end = g[chunk - 1:chunk, :]
        kd = kk * jnp.exp(g_end - g)
        upd = lax.dot_general(iv.astype(BF16), kd.astype(BF16),
                              (((0,), (0,)), ((), ())), preferred_element_type=F32)
        return st * jnp.exp(g_end) + upd

    st_ref[0, 0] = lax.fori_loop(0, length // chunk, chunk_step, s0_ref[0, 0])


def _hgrn(hgrp, lb_logits, gain, s0t, *, batch, layer):
    length = hgrp.shape[0] // batch
    chunk = _largest_block(length, 64, 16)
    col = lambda off: pl.BlockSpec((length, HG_DIM), lambda b, h: (b, off + h))
    par = lambda a: pl.BlockSpec((a.shape[0], HG_DIM), lambda b, h: (0, h))
    st = pl.BlockSpec((1, 1, HG_DIM, HG_DIM), lambda b, h: (b, h, 0, 0))
    buf = pltpu.VMEM((chunk, HG_DIM), F32)
    return pl.pallas_call(
        functools.partial(_hgrn_body, chunk=chunk, layer=layer),
        grid=(batch, HG_HEADS),
        in_specs=[col(0), col(HG_HEADS), col(2 * HG_HEADS), col(3 * HG_HEADS), par(lb_logits), par(gain), st],
        out_specs=[pl.BlockSpec((length, HG_DIM), lambda b, h: (b, h)), st],
        out_shape=[jax.ShapeDtypeStruct((batch * length, HG_WIDTH), BF16),
                   jax.ShapeDtypeStruct(s0t.shape, F32)],
        scratch_shapes=[buf],
        compiler_params=_params("parallel", "parallel"),
    )(hgrp, hgrp, hgrp, hgrp, lb_logits, gain, s0t)


def _topk_rows(s, k, payload=None):
    r = s.shape[0]
    iota = lax.broadcasted_iota(jnp.int32, s.shape, 0)
    vals, picks = [], []
    for _ in range(k):
        m = jnp.max(s, axis=0, keepdims=True)
        idx = jnp.min(jnp.where(s == m, iota, r), axis=0, keepdims=True)
        sel = iota == idx
        vals.append(m)
        picks.append(idx if payload is None else jnp.max(jnp.where(sel, payload, -1), axis=0, keepdims=True))
        s = jnp.where(sel, -jnp.inf, s)
    return jnp.concatenate(vals, axis=0), jnp.concatenate(picks, axis=0)


def _pair_candidates(va, ia, vb, ib):
    k, t = va.shape
    vals, ids = [], []
    i = 0
    while i < k:
        n = k // (i + 1)
        run = 1
        if n == 1:
            run = k - i
        vals.append(va[i:i + run] + vb[:n])
        ids.append(ia[i:i + run] * PEER_N_KEYS + ib[:n])
        i += run
    pad = -sum(v.shape[0] for v in vals) % SUBLANES
    if pad:
        vals.append(jnp.full((pad, t), -jnp.inf, F32))
        ids.append(jnp.full((pad, t), -1, jnp.int32))
    return jnp.concatenate(vals, axis=0), jnp.concatenate(ids, axis=0)


def _route_body(x_ref, att_ref, hgo_ref, woa_ref, wob_ref, g2_ref, wq_ref, keys_ref,
                x1_ref, h2_ref, eidx_ref, gate_ref, qp_s):
    x1 = (x_ref[...] + jnp.dot(att_ref[...], woa_ref[...], preferred_element_type=F32)
          + jnp.dot(hgo_ref[...], wob_ref[...], preferred_element_type=F32))
    x1_ref[...] = x1
    h2 = _rms(x1, g2_ref[...])
    h2_ref[...] = h2
    qp_s[...] = jnp.dot(h2.astype(BF16), wq_ref[...], preferred_element_type=F32)
    nt = (((1,), (1,)), ((), ()))

    def head(h, carry):
        c0 = pl.multiple_of(h * 2 * PEER_HALF, 2 * PEER_HALF)
        qa = qp_s[:, pl.ds(c0, PEER_HALF)].astype(BF16)
        qb = qp_s[:, pl.ds(c0 + PEER_HALF, PEER_HALF)].astype(BF16)
        sa = lax.dot_general(keys_ref[h, 0], qa, nt, preferred_element_type=F32)
        sb = lax.dot_general(keys_ref[h, 1], qb, nt, preferred_element_type=F32)
        r0 = pl.multiple_of(h * PEER_TOPK, PEER_TOPK)
        for c in range(sa.shape[1] // LANES):
            ls = slice(c * LANES, (c + 1) * LANES)
            va, ia = _topk_rows(sa[:, ls], PEER_TOPK)
            vb, ib = _topk_rows(sb[:, ls], PEER_TOPK)
            cand, cidx = _pair_candidates(va, ia, vb, ib)
            ts, te = _topk_rows(cand, PEER_TOPK, payload=cidx)
            ex = jnp.exp(ts - ts[0:1])
            eidx_ref[0, pl.ds(r0, PEER_TOPK), pl.ds(c * LANES, LANES)] = te
            gate_ref[0, pl.ds(r0, PEER_TOPK), pl.ds(c * LANES, LANES)] = ex / jnp.sum(ex, axis=0, keepdims=True)
        return carry

    lax.fori_loop(0, PEER_HEADS, head, 0)


def _route(x, att, hgo, woa, wob, g2, wq, keys):
    n = x.shape[0]
    tb = _largest_block(n, TOKEN_BLOCK_ROUTE, LANES)
    nb = n // tb
    row = lambda w: pl.BlockSpec((tb, w), lambda i: (i, 0))
    full = lambda a: pl.BlockSpec(a.shape, lambda i: (0,) * a.ndim)
    sel = pl.BlockSpec((1, PEER_SEL, tb), lambda i: (i, 0, 0))
    return pl.pallas_call(
        _route_body,
        grid=(nb,),
        in_specs=[row(D_MODEL), row(ATT_WIDTH), row(HG_WIDTH), full(woa), full(wob), full(g2), full(wq), full(keys)],
        out_specs=[row(D_MODEL), row(D_MODEL), sel, sel],
        out_shape=[jax.ShapeDtypeStruct((n, D_MODEL), F32),
                   jax.ShapeDtypeStruct((n, D_MODEL), F32),
                   jax.ShapeDtypeStruct((nb, PEER_SEL, tb), jnp.int32),
                   jax.ShapeDtypeStruct((nb, PEER_SEL, tb), F32)],
        scratch_shapes=[pltpu.VMEM((tb, PEER_HEADS * 2 * PEER_HALF), F32)],
        compiler_params=_params("parallel"),
    )(x, att, hgo, woa, wob, g2, wq, keys)


def _pack_bf16_pair(hi, lo):
    bits = lambda a: lax.bitcast_convert_type(a.astype(BF16), jnp.uint16).astype(jnp.uint32)
    return (bits(hi) << 16) | bits(lo)


def _packed_hi(word):
    return lax.bitcast_convert_type(word & jnp.uint32(0xFFFF0000), F32)


def _packed_lo(word):
    return lax.bitcast_convert_type(word << 16, F32)


def _experts_body(eidx_ref, gate_ref, h2_ref, x1_ref, gf_ref, uv_ref, y_ref,
                  idx_s, buf0, buf1, buf2, idx_sem, row_sem, *, tb):
    bufs = (buf0, buf1, buf2)
    idx_cp = pltpu.make_async_copy(eidx_ref, idx_s, idx_sem)
    idx_cp.start()
    idx_cp.wait()

    def issue_group(g, slot_set):
        for j in range(GATHER_GROUP):
            t = g * GATHER_GROUP + j
            for r in range(PEER_SEL):
                pltpu.make_async_copy(uv_ref.at[idx_s[t, r]], bufs[slot_set].at[j, :, r, :],
                                      row_sem.at[slot_set, j]).start(priority=r % 2)

    def wait(slot_set, j):
        dst = bufs[slot_set].at[j]
        pltpu.make_async_copy(dst, dst, row_sem.at[slot_set, j]).wait()

    gf = gf_ref[...]
    lane_t = lax.broadcasted_iota(jnp.int32, (PEER_SEL, tb), 1)
    n_groups = tb // GATHER_GROUP

    def group(g, slot_set, prefetch):
        for j in range(GATHER_GROUP):
            wait(slot_set, j)
        if prefetch:
            issue_group(g + GATHER_SETS - 1, (slot_set + GATHER_SETS - 1) % GATHER_SETS)
        buf = bufs[slot_set]
        for j in range(GATHER_GROUP):
            t = g * GATHER_GROUP + j
            xrow = h2_ref[pl.ds(t, 1), :]
            p = _packed_hi(buf[j, 0]) * xrow[:, :LANES]
            for s in range(1, SUBLANES):
                p = p + _packed_hi(buf[j, s]) * xrow[:, s * LANES:(s + 1) * LANES]
            hcol = jnp.sum(p, axis=1, keepdims=True)
            act = 0.5 * hcol * (1.0 + lax.erf(hcol * (2.0 ** -0.5)))
            gcol = jnp.sum(jnp.where(lane_t == t, gate_ref[0], 0.0), axis=1, keepdims=True)
            w = gcol * act
            o = [jnp.sum(w * _packed_lo(buf[j, s]), axis=0, keepdims=True) for s in range(SUBLANES)]
            orow = jnp.concatenate(o, axis=1)
            y_ref[pl.ds(t, 1), :] = _rms(x1_ref[pl.ds(t, 1), :] + orow, gf)

    for s in range(GATHER_SETS - 1):
        issue_group(s, s)
    n_main = (n_groups - (GATHER_SETS - 1)) // GATHER_SETS

    def main(k, carry):
        for s in range(GATHER_SETS):
            group(k * GATHER_SETS + s, s, True)
        return carry

    lax.fori_loop(0, n_main, main, 0)
    for g in range(n_main * GATHER_SETS, n_groups):
        group(g, g % GATHER_SETS, g + GATHER_SETS - 1 < n_groups)


def _experts(eidx, gate, h2, x1, gf, uv):
    n = h2.shape[0]
    tb = TOKEN_BLOCK_EXPERT
    tbr = gate.shape[2]
    assert GATHER_SETS == 3 and tbr % tb == 0 and tb % GATHER_GROUP == 0 and tb // GATHER_GROUP >= GATHER_SETS
    per = tbr // tb
    row = lambda w: pl.BlockSpec((tb, w), lambda i: (i, 0))
    sel = pl.BlockSpec((1, PEER_SEL, tb), lambda i: (i // per, 0, i % per))
    slots = pltpu.VMEM((GATHER_GROUP, SUBLANES, PEER_SEL, LANES), jnp.uint32)
    return pl.pallas_call(
        functools.partial(_experts_body, tb=tb),
        grid=(n // tb,),
        in_specs=[row(PEER_SEL), sel, row(D_MODEL), row(D_MODEL), pl.BlockSpec(gf.shape, lambda i: (0, 0)),
                  pl.BlockSpec(memory_space=pl.ANY)],
        out_specs=row(D_MODEL),
        out_shape=jax.ShapeDtypeStruct((n, D_MODEL), F32),
        scratch_shapes=[pltpu.SMEM((tb, PEER_SEL), jnp.int32), slots, slots, slots,
                        pltpu.SemaphoreType.DMA(()),
                        pltpu.SemaphoreType.DMA((GATHER_SETS, GATHER_GROUP))],
        compiler_params=_params("arbitrary"),
    )(eidx, gate, h2, x1, gf, uv)


def _mixer(x, batch, params, layer, cache=None):
    q, k, v, logf, hgrp = _inproj(x, params["g1"], params["wqkv"], params["wfg"], params["bfg"], params["wh"])
    length = x.shape[0] // batch
    if cache is None:
        k_all, v_all, logf_all, q_off = k, v, logf.reshape(batch, length, ATT_HEADS), 0
        s0t = jnp.zeros((batch, HG_HEADS, HG_DIM, HG_DIM), F32)
    else:
        cache_k, cache_v, cache_logf, state = cache
        past = cache_k.shape[1]
        cat = lambda c, new: jnp.concatenate(
            [c.reshape(batch, past, -1).astype(F32), new.reshape(batch, length, -1)], axis=1)
        k_all = cat(cache_k, k).reshape(batch * (past + length), ATT_KV_WIDTH)
        v_all = cat(cache_v, v).reshape(batch * (past + length), ATT_KV_WIDTH)
        logf_all, q_off = cat(cache_logf, logf), past
        s0t = jnp.swapaxes(state.astype(F32), -1, -2)
    f_all = _cumsum(logf_all)
    f_col = f_all[:, q_off:].reshape(batch * length, ATT_HEADS)
    f_row = jnp.swapaxes(f_all, 1, 2)
    att = _attn(q, k_all, v_all, f_col, f_row, batch=batch, q_off=q_off)
    hgo, st = _hgrn(hgrp, params["lb_logits"], params["hg_gain"], s0t, batch=batch, layer=layer)
    return att, hgo, (k, v, logf, jnp.swapaxes(st, -1, -2))


def kernel(x_prompt, x_sample, cache_k, cache_v, cache_logf, state_hgrn, meta_tokens,
           w_in, b_forget, hg_lb_logits, hg_norm_gain, w_out, norm1_gain, norm2_gain,
           peer_w_query, peer_sub_keys, peer_expert_u, peer_expert_v, final_norm_gain):
    depth = w_in.shape[0]
    assert depth == 1, "single-layer trunk: the PEER stage fuses the final norm"
    bp, seq, _ = x_prompt.shape
    bs, dseq, _ = x_sample.shape
    lp = N_META + seq
    meta = jnp.broadcast_to(meta_tokens.astype(x_prompt.dtype)[None], (bp, N_META, D_MODEL))
    xp = jnp.concatenate([meta, x_prompt], axis=1).reshape(bp * lp, D_MODEL)
    xs = x_sample.reshape(bs * dseq, D_MODEL)

    l = 0
    o_fg = ATT_WIDTH + 2 * ATT_KV_WIDTH
    o_h = o_fg + ATT_HEADS
    row = lambda a: a.reshape(1, -1).astype(F32)
    params = {
        "g1": row(norm1_gain[l]),
        "wqkv": w_in[l][:, :o_fg].astype(BF16),
        "wfg": jnp.pad(w_in[l][:, o_fg:o_h], ((0, 0), (0, LANES - ATT_HEADS))).astype(BF16),
        "bfg": row(b_forget[l]),
        "wh": w_in[l][:, o_h:].astype(BF16),
        "lb_logits": hg_lb_logits.astype(F32),
        "hg_gain": row(hg_norm_gain[l]),
    }
    att_p, hgo_p, sp = _mixer(xp, bp, params, l)
    att_s, hgo_s, ss = _mixer(xs, bs, params, l,
                              cache=(cache_k[l], cache_v[l], cache_logf[l], state_hgrn[l]))

    x_all = jnp.concatenate([xp, xs], axis=0)
    att = jnp.concatenate([att_p, att_s], axis=0)
    hgo = jnp.concatenate([hgo_p, hgo_s], axis=0)
    wo = w_out[l].astype(BF16)
    x1, h2, eidx, gate = _route(x_all, att, hgo, wo[:ATT_WIDTH], wo[ATT_WIDTH:], row(norm2_gain[l]),
                                peer_w_query[l].astype(BF16), peer_sub_keys[l].astype(BF16))
    n_exp = peer_expert_u.shape[1]
    uv = _pack_bf16_pair(peer_expert_u[l], peer_expert_v[l]).reshape(n_exp, SUBLANES, LANES)
    eidx_t = jnp.swapaxes(eidx, 1, 2).reshape(-1, PEER_SEL)
    y = _experts(eidx_t, gate, h2, x1, row(final_norm_gain), uv)

    n_p = bp * lp
    y_prompt = y[:n_p].reshape(bp, lp, D_MODEL)[:, N_META:]
    y_sample = y[n_p:].reshape(bs, dseq, D_MODEL)

    def states(s, batch, length):
        k, v, logf, st = s
        return (k.reshape(1, batch, length, ATT_KV_HEADS, ATT_HEAD_DIM),
                v.reshape(1, batch, length, ATT_KV_HEADS, ATT_HEAD_DIM),
                logf.reshape(1, batch, length, ATT_HEADS),
                st.reshape(1, batch, HG_HEADS, HG_DIM, HG_DIM))

    return (y_prompt, y_sample) + states(sp, bp, lp) + states(ss, bs, dseq)
```

```python
import functools

import jax
import jax.numpy as jnp
from jax import lax
from jax.experimental import pallas as pl
from jax.experimental.pallas import tpu as pltpu

F32 = jnp.float32
BF16 = jnp.bfloat16
EPS = 1e-6
N_META = 16

D_MODEL = 1024
ATT_HEADS = 8
ATT_KV_HEADS = 4
ATT_HEAD_DIM = 64
ATT_GROUP = ATT_HEADS // ATT_KV_HEADS
ATT_WIDTH = ATT_HEADS * ATT_HEAD_DIM
ATT_KV_WIDTH = ATT_KV_HEADS * ATT_HEAD_DIM
HG_HEADS = 4
HG_DIM = 128
HG_WIDTH = HG_HEADS * HG_DIM
PEER_HEADS = 8
PEER_N_KEYS = 128
PEER_TOPK = 16
PEER_HALF = 128
PEER_SEL = PEER_HEADS * PEER_TOPK

LANES = 128
SUBLANES = 8
VMEM_LIMIT = 48 * 1024 * 1024

TOKEN_BLOCK_PROJ = 512
TOKEN_BLOCK_ROUTE = 256
TOKEN_BLOCK_EXPERT = 128
GATHER_GROUP = 4
GATHER_SETS = 3


def _params(*sem):
    return pltpu.CompilerParams(dimension_semantics=sem, vmem_limit_bytes=VMEM_LIMIT)


def _largest_block(n, cap, mult):
    best = None
    for d in range(mult, cap + 1, mult):
        if n % d == 0:
            best = d
    assert best is not None, (n, cap, mult)
    return best


def _split3(x):
    hi = x.astype(BF16)
    r1 = x - hi.astype(F32)
    mid = r1.astype(BF16)
    lo = (r1 - mid.astype(F32)).astype(BF16)
    return hi, mid, lo


def _tri_cumsum(tri, x):
    hi, mid, lo = _split3(x)
    d = lambda a: jnp.dot(tri, a, preferred_element_type=F32)
    return (d(lo) + d(mid)) + d(hi)


def _lower_tri(c):
    r = lax.broadcasted_iota(jnp.int32, (c, c), 0)
    s = lax.broadcasted_iota(jnp.int32, (c, c), 1)
    return jnp.where(s <= r, 1.0, 0.0).astype(BF16)


def _rms(x, gain):
    return x * lax.rsqrt(jnp.mean(x * x, axis=-1, keepdims=True) + EPS) * gain


def _inproj_body(x_ref, g_ref, wqkv_ref, wfg_ref, bfg_ref, wh_ref,
                 q_ref, k_ref, v_ref, logf_ref, hgrp_ref):
    hb = _rms(x_ref[...], g_ref[...]).astype(BF16)
    qkv = jnp.dot(hb, wqkv_ref[...], preferred_element_type=F32)
    q_ref[...] = (qkv[:, :ATT_WIDTH] * (ATT_HEAD_DIM ** -0.5)).astype(BF16)
    k_ref[...] = qkv[:, ATT_WIDTH:ATT_WIDTH + ATT_KV_WIDTH]
    v_ref[...] = qkv[:, ATT_WIDTH + ATT_KV_WIDTH:]
    fg = jnp.dot(hb, wfg_ref[...], preferred_element_type=F32)[:, :ATT_HEADS] + bfg_ref[...]
    logf_ref[...] = jnp.minimum(fg, 0.0) - jnp.log1p(jnp.exp(-jnp.abs(fg)))
    hgrp_ref[...] = jnp.dot(hb, wh_ref[...], preferred_element_type=F32)


def _inproj(x, gain, wqkv, wfg, bfg, wh):
    n = x.shape[0]
    tb = _largest_block(n, TOKEN_BLOCK_PROJ, 8)
    row = lambda w: pl.BlockSpec((tb, w), lambda i: (i, 0))
    full = lambda a: pl.BlockSpec(a.shape, lambda i: (0,) * a.ndim)
    return pl.pallas_call(
        _inproj_body,
        grid=(n // tb,),
        in_specs=[row(D_MODEL), full(gain), full(wqkv), full(wfg), full(bfg), full(wh)],
        out_specs=[row(ATT_WIDTH), row(ATT_KV_WIDTH), row(ATT_KV_WIDTH), row(ATT_HEADS), row(4 * HG_WIDTH)],
        out_shape=[jax.ShapeDtypeStruct((n, ATT_WIDTH), BF16),
                   jax.ShapeDtypeStruct((n, ATT_KV_WIDTH), F32),
                   jax.ShapeDtypeStruct((n, ATT_KV_WIDTH), F32),
                   jax.ShapeDtypeStruct((n, ATT_HEADS), F32),
                   jax.ShapeDtypeStruct((n, 4 * HG_WIDTH), F32)],
        compiler_params=_params("parallel"),
    )(x, gain, wqkv, wfg, bfg, wh)


def _cumsum_body(x_ref, o_ref, *, cb):
    length, width = x_ref.shape[1], x_ref.shape[2]
    tri = _lower_tri(cb)
    carry = jnp.zeros((1, width), F32)
    for j in range(length // cb):
        f = _tri_cumsum(tri, x_ref[0, j * cb:(j + 1) * cb, :]) + carry
        o_ref[0, j * cb:(j + 1) * cb, :] = f
        carry = f[cb - 1:cb, :]


def _cumsum(logf):
    b, length, h = logf.shape
    cb = _largest_block(length, 512, 8)
    spec = pl.BlockSpec((1, length, h), lambda i: (i, 0, 0))
    return pl.pallas_call(
        functools.partial(_cumsum_body, cb=cb),
        grid=(b,),
        in_specs=[spec],
        out_specs=spec,
        out_shape=jax.ShapeDtypeStruct(logf.shape, F32),
        compiler_params=_params("parallel"),
    )(logf)


def _attn_block(q_ref, k_ref, v_ref, fc_ref, fr_ref, o_ref, *, q_start, kmax):
    tq = q_ref.shape[0]
    qpos = q_start + lax.broadcasted_iota(jnp.int32, (tq, kmax), 0)
    kpos = lax.broadcasted_iota(jnp.int32, (tq, kmax), 1)
    mask = kpos <= qpos
    kb = k_ref[:kmax, :].astype(BF16)
    vb = v_ref[:kmax, :].astype(BF16)
    q = q_ref[...]
    fc = fc_ref[...]
    fr = fr_ref[0]
    outs = []
    for h in range(ATT_HEADS):
        g = h // ATT_GROUP
        hs = slice(h * ATT_HEAD_DIM, (h + 1) * ATT_HEAD_DIM)
        gs = slice(g * ATT_HEAD_DIM, (g + 1) * ATT_HEAD_DIM)
        s = lax.dot_general(q[:, hs], kb[:, gs], (((1,), (1,)), ((), ())), preferred_element_type=F32)
        s = s + fc[:, h:h + 1] - fr[h:h + 1, :kmax]
        s = jnp.where(mask, s, -jnp.inf)
        p = jnp.exp(s - jnp.max(s, axis=-1, keepdims=True))
        l = jnp.sum(p, axis=-1, keepdims=True)
        o = jnp.dot(p.astype(BF16), vb[:, gs], preferred_element_type=F32)
        outs.append(o / l)
    o_ref[...] = jnp.concatenate(outs, axis=-1).astype(BF16)


def _attn_body(*refs, q_off, tq, nq):
    lk = refs[1].shape[0]
    for c in range(nq):
        q_start = q_off + c * tq
        kmax = min(lk, -(-(q_start + tq) // LANES) * LANES)

        @pl.when(pl.program_id(1) == c)
        def _(q_start=q_start, kmax=kmax):
            _attn_block(*refs, q_start=q_start, kmax=kmax)


def _attn(q, k, v, f_col, f_row, *, batch, q_off):
    t_q = q.shape[0] // batch
    lk = k.shape[0] // batch
    tq = _largest_block(t_q, 384, 8)
    nq = t_q // tq
    qspec = lambda w: pl.BlockSpec((tq, w), lambda b, i: (b * nq + i, 0))
    kspec = pl.BlockSpec((lk, ATT_KV_WIDTH), lambda b, i: (b, 0))
    return pl.pallas_call(
        functools.partial(_attn_body, q_off=q_off, tq=tq, nq=nq),
        grid=(batch, nq),
        in_specs=[qspec(ATT_WIDTH), kspec, kspec, qspec(ATT_HEADS),
                  pl.BlockSpec((1, ATT_HEADS, lk), lambda b, i: (b, 0, 0))],
        out_specs=qspec(ATT_WIDTH),
        out_shape=jax.ShapeDtypeStruct((batch * t_q, ATT_WIDTH), BF16),
        compiler_params=_params("parallel", "parallel"),
    )(q, k, v, f_col, f_row)


def _hgrn_body(hq_ref, hf_ref, hi_ref, hg_ref, lbl_ref, gain_ref, s0_ref, o_ref, st_ref,
               oi_s, *, chunk, layer):
    length = hq_ref.shape[0]
    lbl = lbl_ref[...]
    e = jnp.exp(lbl - jnp.max(lbl, axis=0, keepdims=True))
    lb = jnp.sum(e[:layer + 1, :], axis=0, keepdims=True) / jnp.sum(e, axis=0, keepdims=True)
    gain = gain_ref[...]
    tri = _lower_tri(chunk)
    rows = lax.broadcasted_iota(jnp.int32, (chunk, 1), 0)

    def chunk_step(c, st):
        r0 = pl.multiple_of(c * chunk, 16)
        f = lb + (1.0 - lb) * jax.nn.sigmoid(hf_ref[pl.ds(r0, chunk), :])
        kk = 1.0 - f
        hq = hq_ref[pl.ds(r0, chunk), :]
        qq = hq * jax.nn.sigmoid(hq)
        iv = hi_ref[pl.ds(r0, chunk), :]
        g = _tri_cumsum(tri, jnp.log(f))
        o_inter = lax.dot_general((qq * jnp.exp(g)).astype(BF16), st.astype(BF16),
                                  (((1,), (1,)), ((), ())), preferred_element_type=F32)
        for t in range(chunk):
            n = (t // SUBLANES + 1) * SUBLANES
            d = jnp.where(rows[:n] <= t, g[t:t + 1, :] - g[:n, :], -jnp.inf)
            a = jnp.exp(d) * kk[:n, :] * qq[t:t + 1, :]
            sc = jnp.sum(a, axis=1, keepdims=True)
            oi_s[t:t + 1, :] = jnp.sum(sc * iv[:n, :], axis=0, keepdims=True)
        o = o_inter + oi_s[...]
        on = _rms(o, gain)
        hg = hg_ref[pl.ds(r0, chunk), :]
        o_ref[pl.ds(r0, chunk), :] = (on * (hg * jax.nn.sigmoid(hg))).astype(BF16)
        g_end = g[chunk - 1:chunk, :]
        kd = kk * jnp.exp(g_end - g)
        upd = lax.dot_general(iv.astype(BF16), kd.astype(BF16),
                              (((0,), (0,)), ((), ())), preferred_element_type=F32)
        return st * jnp.exp(g_end) + upd

    st_ref[0, 0] = lax.fori_loop(0, length // chunk, chunk_step, s0_ref[0, 0])


def _hgrn(hgrp, lb_logits, gain, s0t, *, batch, layer):
    length = hgrp.shape[0] // batch
    chunk = _largest_block(length, 64, 16)
    col = lambda off: pl.BlockSpec((length, HG_DIM), lambda b, h: (b, off + h))
    par = lambda a: pl.BlockSpec((a.shape[0], HG_DIM), lambda b, h: (0, h))
    st = pl.BlockSpec((1, 1, HG_DIM, HG_DIM), lambda b, h: (b, h, 0, 0))
    buf = pltpu.VMEM((chunk, HG_DIM), F32)
    return pl.pallas_call(
        functools.partial(_hgrn_body, chunk=chunk, layer=layer),
        grid=(batch, HG_HEADS),
        in_specs=[col(0), col(HG_HEADS), col(2 * HG_HEADS), col(3 * HG_HEADS), par(lb_logits), par(gain), st],
        out_specs=[pl.BlockSpec((length, HG_DIM), lambda b, h: (b, h)), st],
        out_shape=[jax.ShapeDtypeStruct((batch * length, HG_WIDTH), BF16),
                   jax.ShapeDtypeStruct(s0t.shape, F32)],
        scratch_shapes=[buf],
        compiler_params=_params("parallel", "parallel"),
    )(hgrp, hgrp, hgrp, hgrp, lb_logits, gain, s0t)


def _topk_rows(s, k, payload=None):
    r = s.shape[0]
    iota = lax.broadcasted_iota(jnp.int32, s.shape, 0)
    vals, picks = [], []
    for _ in range(k):
        m = jnp.max(s, axis=0, keepdims=True)
        idx = jnp.min(jnp.where(s == m, iota, r), axis=0, keepdims=True)
        sel = iota == idx
        vals.append(m)
        picks.append(idx if payload is None else jnp.max(jnp.where(sel, payload, -1), axis=0, keepdims=True))
        s = jnp.where(sel, -jnp.inf, s)
    return jnp.concatenate(vals, axis=0), jnp.concatenate(picks, axis=0)


def _pair_candidates(va, ia, vb, ib):
    k, t = va.shape
    vals, ids = [], []
    i = 0
    while i < k:
        n = k // (i + 1)
        run = 1
        if n == 1:
            run = k - i
        vals.append(va[i:i + run] + vb[:n])
        ids.append(ia[i:i + run] * PEER_N_KEYS + ib[:n])
        i += run
    pad = -sum(v.shape[0] for v in vals) % SUBLANES
    if pad:
        vals.append(jnp.full((pad, t), -jnp.inf, F32))
        ids.append(jnp.full((pad, t), -1, jnp.int32))
    return jnp.concatenate(vals, axis=0), jnp.concatenate(ids, axis=0)


def _route_body(x_ref, att_ref, hgo_ref, woa_ref, wob_ref, g2_ref, wq_ref, keys_ref,
                x1_ref, h2_ref, eidx_ref, gate_ref, qp_s):
    x1 = (x_ref[...] + jnp.dot(att_ref[...], woa_ref[...], preferred_element_type=F32)
          + jnp.dot(hgo_ref[...], wob_ref[...], preferred_element_type=F32))
    x1_ref[...] = x1
    h2 = _rms(x1, g2_ref[...])
    h2_ref[...] = h2
    qp_s[...] = jnp.dot(h2.astype(BF16), wq_ref[...], preferred_element_type=F32)
    nt = (((1,), (1,)), ((), ()))

    def head(h, carry):
        c0 = pl.multiple_of(h * 2 * PEER_HALF, 2 * PEER_HALF)
        qa = qp_s[:, pl.ds(c0, PEER_HALF)].astype(BF16)
        qb = qp_s[:, pl.ds(c0 + PEER_HALF, PEER_HALF)].astype(BF16)
        sa = lax.dot_general(keys_ref[h, 0], qa, nt, preferred_element_type=F32)
        sb = lax.dot_general(keys_ref[h, 1], qb, nt, preferred_element_type=F32)
        r0 = pl.multiple_of(h * PEER_TOPK, PEER_TOPK)
        for c in range(sa.shape[1] // LANES):
            ls = slice(c * LANES, (c + 1) * LANES)
            va, ia = _topk_rows(sa[:, ls], PEER_TOPK)
            vb, ib = _topk_rows(sb[:, ls], PEER_TOPK)
            cand, cidx = _pair_candidates(va, ia, vb, ib)
            ts, te = _topk_rows(cand, PEER_TOPK, payload=cidx)
            ex = jnp.exp(ts - ts[0:1])
            eidx_ref[0, pl.ds(r0, PEER_TOPK), pl.ds(c * LANES, LANES)] = te
            gate_ref[0, pl.ds(r0, PEER_TOPK), pl.ds(c * LANES, LANES)] = ex / jnp.sum(ex, axis=0, keepdims=True)
        return carry

    lax.fori_loop(0, PEER_HEADS, head, 0)


def _route(x, att, hgo, woa, wob, g2, wq, keys):
    n = x.shape[0]
    tb = _largest_block(n, TOKEN_BLOCK_ROUTE, LANES)
    nb = n // tb
    row = lambda w: pl.BlockSpec((tb, w), lambda i: (i, 0))
    full = lambda a: pl.BlockSpec(a.shape, lambda i: (0,) * a.ndim)
    sel = pl.BlockSpec((1, PEER_SEL, tb), lambda i: (i, 0, 0))
    return pl.pallas_call(
        _route_body,
        grid=(nb,),
        in_specs=[row(D_MODEL), row(ATT_WIDTH), row(HG_WIDTH), full(woa), full(wob), full(g2), full(wq), full(keys)],
        out_specs=[row(D_MODEL), row(D_MODEL), sel, sel],
        out_shape=[jax.ShapeDtypeStruct((n, D_MODEL), F32),
                   jax.ShapeDtypeStruct((n, D_MODEL), F32),
                   jax.ShapeDtypeStruct((nb, PEER_SEL, tb), jnp.int32),
                   jax.ShapeDtypeStruct((nb, PEER_SEL, tb), F32)],
        scratch_shapes=[pltpu.VMEM((tb, PEER_HEADS * 2 * PEER_HALF), F32)],
        compiler_params=_params("parallel"),
    )(x, att, hgo, woa, wob, g2, wq, keys)


def _pack_bf16_pair(hi, lo):
    bits = lambda a: lax.bitcast_convert_type(a.astype(BF16), jnp.uint16).astype(jnp.uint32)
    return (bits(hi) << 16) | bits(lo)


def _packed_hi(word):
    return lax.bitcast_convert_type(word & jnp.uint32(0xFFFF0000), F32)


def _packed_lo(word):
    return lax.bitcast_convert_type(word << 16, F32)


def _experts_body(eidx_ref, gate_ref, h2_ref, x1_ref, gf_ref, uv_ref, y_ref,
                  idx_s, buf0, buf1, buf2, idx_sem, row_sem, *, tb):
    bufs = (buf0, buf1, buf2)
    idx_cp = pltpu.make_async_copy(eidx_ref, idx_s, idx_sem)
    idx_cp.start()
    idx_cp.wait()

    def issue_group(g, slot_set):
        for j in range(GATHER_GROUP):
            t = g * GATHER_GROUP + j
            for r in range(PEER_SEL):
                pltpu.make_async_copy(uv_ref.at[idx_s[t, r]],
                                      bufs[slot_set].at[j, pl.ds(r * SUBLANES, SUBLANES), :],
                                      row_sem.at[slot_set, j]).start(priority=r % 2)

    def wait(slot_set, j):
        dst = bufs[slot_set].at[j]
        pltpu.make_async_copy(dst, dst, row_sem.at[slot_set, j]).wait()

    gf = gf_ref[...]
    lane_t = lax.broadcasted_iota(jnp.int32, (PEER_SEL, tb), 1)
    n_groups = tb // GATHER_GROUP

    def group(g, slot_set, prefetch):
        for j in range(GATHER_GROUP):
            wait(slot_set, j)
        if prefetch:
            issue_group(g + GATHER_SETS - 1, (slot_set + GATHER_SETS - 1) % GATHER_SETS)
        buf = bufs[slot_set]
        for j in range(GATHER_GROUP):
            t = g * GATHER_GROUP + j
            xrow = h2_ref[pl.ds(t, 1), :]
            tiles = [buf[j, pl.ds(s, PEER_SEL, stride=SUBLANES), :] for s in range(SUBLANES)]
            p = _packed_hi(tiles[0]) * xrow[:, :LANES]
            for s in range(1, SUBLANES):
                p = p + _packed_hi(tiles[s]) * xrow[:, s * LANES:(s + 1) * LANES]
            hcol = jnp.sum(p, axis=1, keepdims=True)
            act = 0.5 * hcol * (1.0 + lax.erf(hcol * (2.0 ** -0.5)))
            gcol = jnp.sum(jnp.where(lane_t == t, gate_ref[0], 0.0), axis=1, keepdims=True)
            w = gcol * act
            o = [jnp.sum(w * _packed_lo(tiles[s]), axis=0, keepdims=True) for s in range(SUBLANES)]
            orow = jnp.concatenate(o, axis=1)
            y_ref[pl.ds(t, 1), :] = _rms(x1_ref[pl.ds(t, 1), :] + orow, gf)

    for s in range(GATHER_SETS - 1):
        issue_group(s, s)
    n_main = (n_groups - (GATHER_SETS - 1)) // GATHER_SETS

    def main(k, carry):
        for s in range(GATHER_SETS):
            group(k * GATHER_SETS + s, s, True)
        return carry

    lax.fori_loop(0, n_main, main, 0)
    for g in range(n_main * GATHER_SETS, n_groups):
        group(g, g % GATHER_SETS, g + GATHER_SETS - 1 < n_groups)


def _experts(eidx, gate, h2, x1, gf, uv):
    n = h2.shape[0]
    tb = TOKEN_BLOCK_EXPERT
    tbr = gate.shape[2]
    assert GATHER_SETS == 3 and tbr % tb == 0 and tb % GATHER_GROUP == 0 and tb // GATHER_GROUP >= GATHER_SETS
    per = tbr // tb
    row = lambda w: pl.BlockSpec((tb, w), lambda i: (i, 0))
    sel = pl.BlockSpec((1, PEER_SEL, tb), lambda i: (i // per, 0, i % per))
    slots = pltpu.VMEM((GATHER_GROUP, PEER_SEL * SUBLANES, LANES), jnp.uint32)
    return pl.pallas_call(
        functools.partial(_experts_body, tb=tb),
        grid=(n // tb,),
        in_specs=[row(PEER_SEL), sel, row(D_MODEL), row(D_MODEL), pl.BlockSpec(gf.shape, lambda i: (0, 0)),
                  pl.BlockSpec(memory_space=pl.ANY)],
        out_specs=row(D_MODEL),
        out_shape=jax.ShapeDtypeStruct((n, D_MODEL), F32),
        scratch_shapes=[pltpu.SMEM((tb, PEER_SEL), jnp.int32), slots, slots, slots,
                        pltpu.SemaphoreType.DMA(()),
                        pltpu.SemaphoreType.DMA((GATHER_SETS, GATHER_GROUP))],
        compiler_params=_params("arbitrary"),
    )(eidx, gate, h2, x1, gf, uv)


def _mixer(x, batch, params, layer, cache=None):
    q, k, v, logf, hgrp = _inproj(x, params["g1"], params["wqkv"], params["wfg"], params["bfg"], params["wh"])
    length = x.shape[0] // batch
    if cache is None:
        k_all, v_all, logf_all, q_off = k, v, logf.reshape(batch, length, ATT_HEADS), 0
        s0t = jnp.zeros((batch, HG_HEADS, HG_DIM, HG_DIM), F32)
    else:
        cache_k, cache_v, cache_logf, state = cache
        past = cache_k.shape[1]
        cat = lambda c, new: jnp.concatenate(
            [c.reshape(batch, past, -1).astype(F32), new.reshape(batch, length, -1)], axis=1)
        k_all = cat(cache_k, k).reshape(batch * (past + length), ATT_KV_WIDTH)
        v_all = cat(cache_v, v).reshape(batch * (past + length), ATT_KV_WIDTH)
        logf_all, q_off = cat(cache_logf, logf), past
        s0t = jnp.swapaxes(state.astype(F32), -1, -2)
    f_all = _cumsum(logf_all)
    f_col = f_all[:, q_off:].reshape(batch * length, ATT_HEADS)
    f_row = jnp.swapaxes(f_all, 1, 2)
    att = _attn(q, k_all, v_all, f_col, f_row, batch=batch, q_off=q_off)
    hgo, st = _hgrn(hgrp, params["lb_logits"], params["hg_gain"], s0t, batch=batch, layer=layer)
    return att, hgo, (k, v, logf, jnp.swapaxes(st, -1, -2))


def kernel(x_prompt, x_sample, cache_k, cache_v, cache_logf, state_hgrn, meta_tokens,
           w_in, b_forget, hg_lb_logits, hg_norm_gain, w_out, norm1_gain, norm2_gain,
           peer_w_query, peer_sub_keys, peer_expert_u, peer_expert_v, final_norm_gain):
    depth = w_in.shape[0]
    assert depth == 1, "single-layer trunk: the PEER stage fuses the final norm"
    bp, seq, _ = x_prompt.shape
    bs, dseq, _ = x_sample.shape
    lp = N_META + seq
    meta = jnp.broadcast_to(meta_tokens.astype(x_prompt.dtype)[None], (bp, N_META, D_MODEL))
    xp = jnp.concatenate([meta, x_prompt], axis=1).reshape(bp * lp, D_MODEL)
    xs = x_sample.reshape(bs * dseq, D_MODEL)

    l = 0
    o_fg = ATT_WIDTH + 2 * ATT_KV_WIDTH
    o_h = o_fg + ATT_HEADS
    row = lambda a: a.reshape(1, -1).astype(F32)
    params = {
        "g1": row(norm1_gain[l]),
        "wqkv": w_in[l][:, :o_fg].astype(BF16),
        "wfg": jnp.pad(w_in[l][:, o_fg:o_h], ((0, 0), (0, LANES - ATT_HEADS))).astype(BF16),
        "bfg": row(b_forget[l]),
        "wh": w_in[l][:, o_h:].astype(BF16),
        "lb_logits": hg_lb_logits.astype(F32),
        "hg_gain": row(hg_norm_gain[l]),
    }
    att_p, hgo_p, sp = _mixer(xp, bp, params, l)
    att_s, hgo_s, ss = _mixer(xs, bs, params, l,
                              cache=(cache_k[l], cache_v[l], cache_logf[l], state_hgrn[l]))

    x_all = jnp.concatenate([xp, xs], axis=0)
    att = jnp.concatenate([att_p, att_s], axis=0)
    hgo = jnp.concatenate([hgo_p, hgo_s], axis=0)
    wo = w_out[l].astype(BF16)
    x1, h2, eidx, gate = _route(x_all, att, hgo, wo[:ATT_WIDTH], wo[ATT_WIDTH:], row(norm2_gain[l]),
                                peer_w_query[l].astype(BF16), peer_sub_keys[l].astype(BF16))
    n_exp = peer_expert_u.shape[1]
    uv = _pack_bf16_pair(peer_expert_u[l], peer_expert_v[l]).reshape(n_exp, SUBLANES, LANES)
    eidx_t = jnp.swapaxes(eidx, 1, 2).reshape(-1, PEER_SEL)
    y = _experts(eidx_t, gate, h2, x1, row(final_norm_gain), uv)

    n_p = bp * lp
    y_prompt = y[:n_p].reshape(bp, lp, D_MODEL)[:, N_META:]
    y_sample = y[n_p:].reshape(bs, dseq, D_MODEL)

    def states(s, batch, length):
        k, v, logf, st = s
        return (k.reshape(1, batch, length, ATT_KV_HEADS, ATT_HEAD_DIM),
                v.reshape(1, batch, length, ATT_KV_HEADS, ATT_HEAD_DIM),
                logf.reshape(1, batch, length, ATT_HEADS),
                st.reshape(1, batch, HG_HEADS, HG_DIM, HG_DIM))

    return (y_prompt, y_sample) + states(sp, bp, lp) + states(ss, bs, dseq)
```

```python
import functools

import jax
import jax.numpy as jnp
from jax import lax
from jax.experimental import pallas as pl
from jax.experimental.pallas import tpu as pltpu

F32 = jnp.float32
BF16 = jnp.bfloat16
EPS = 1e-6
N_META = 16

D_MODEL = 1024
ATT_HEADS = 8
ATT_KV_HEADS = 4
ATT_HEAD_DIM = 64
ATT_GROUP = ATT_HEADS // ATT_KV_HEADS
ATT_WIDTH = ATT_HEADS * ATT_HEAD_DIM
ATT_KV_WIDTH = ATT_KV_HEADS * ATT_HEAD_DIM
HG_HEADS = 4
HG_DIM = 128
HG_WIDTH = HG_HEADS * HG_DIM
PEER_HEADS = 8
PEER_N_KEYS = 128
PEER_TOPK = 16
PEER_HALF = 128
PEER_SEL = PEER_HEADS * PEER_TOPK

LANES = 128
SUBLANES = 8
VMEM_LIMIT = 48 * 1024 * 1024

TOKEN_BLOCK_PROJ = 512
TOKEN_BLOCK_ROUTE = 256
TOKEN_BLOCK_EXPERT = 128
GATHER_GROUP = 4
GATHER_SETS = 3


def _params(*sem):
    return pltpu.CompilerParams(dimension_semantics=sem, vmem_limit_bytes=VMEM_LIMIT)


def _largest_block(n, cap, mult):
    best = None
    for d in range(mult, cap + 1, mult):
        if n % d == 0:
            best = d
    assert best is not None, (n, cap, mult)
    return best


def _split3(x):
    hi = x.astype(BF16)
    r1 = x - hi.astype(F32)
    mid = r1.astype(BF16)
    lo = (r1 - mid.astype(F32)).astype(BF16)
    return hi, mid, lo


def _tri_cumsum(tri, x):
    hi, mid, lo = _split3(x)
    d = lambda a: jnp.dot(tri, a, preferred_element_type=F32)
    return (d(lo) + d(mid)) + d(hi)


def _lower_tri(c):
    r = lax.broadcasted_iota(jnp.int32, (c, c), 0)
    s = lax.broadcasted_iota(jnp.int32, (c, c), 1)
    return jnp.where(s <= r, 1.0, 0.0).astype(BF16)


def _rms(x, gain):
    return x * lax.rsqrt(jnp.mean(x * x, axis=-1, keepdims=True) + EPS) * gain


def _inproj_body(x_ref, g_ref, wqkv_ref, wfg_ref, bfg_ref, wh_ref,
                 q_ref, k_ref, v_ref, logf_ref, hgrp_ref):
    hb = _rms(x_ref[...], g_ref[...]).astype(BF16)
    qkv = jnp.dot(hb, wqkv_ref[...], preferred_element_type=F32)
    q_ref[...] = (qkv[:, :ATT_WIDTH] * (ATT_HEAD_DIM ** -0.5)).astype(BF16)
    k_ref[...] = qkv[:, ATT_WIDTH:ATT_WIDTH + ATT_KV_WIDTH]
    v_ref[...] = qkv[:, ATT_WIDTH + ATT_KV_WIDTH:]
    fg = jnp.dot(hb, wfg_ref[...], preferred_element_type=F32)[:, :ATT_HEADS] + bfg_ref[...]
    logf_ref[...] = jnp.minimum(fg, 0.0) - jnp.log1p(jnp.exp(-jnp.abs(fg)))
    hgrp_ref[...] = jnp.dot(hb, wh_ref[...], preferred_element_type=F32)


def _inproj(x, gain, wqkv, wfg, bfg, wh):
    n = x.shape[0]
    tb = _largest_block(n, TOKEN_BLOCK_PROJ, 8)
    row = lambda w: pl.BlockSpec((tb, w), lambda i: (i, 0))
    full = lambda a: pl.BlockSpec(a.shape, lambda i: (0,) * a.ndim)
    return pl.pallas_call(
        _inproj_body,
        grid=(n // tb,),
        in_specs=[row(D_MODEL), full(gain), full(wqkv), full(wfg), full(bfg), full(wh)],
        out_specs=[row(ATT_WIDTH), row(ATT_KV_WIDTH), row(ATT_KV_WIDTH), row(ATT_HEADS), row(4 * HG_WIDTH)],
        out_shape=[jax.ShapeDtypeStruct((n, ATT_WIDTH), BF16),
                   jax.ShapeDtypeStruct((n, ATT_KV_WIDTH), F32),
                   jax.ShapeDtypeStruct((n, ATT_KV_WIDTH), F32),
                   jax.ShapeDtypeStruct((n, ATT_HEADS), F32),
                   jax.ShapeDtypeStruct((n, 4 * HG_WIDTH), F32)],
        compiler_params=_params("parallel"),
    )(x, gain, wqkv, wfg, bfg, wh)


def _cumsum_body(x_ref, o_ref, *, cb):
    length, width = x_ref.shape[1], x_ref.shape[2]
    tri = _lower_tri(cb)
    carry = jnp.zeros((1, width), F32)
    for j in range(length // cb):
        f = _tri_cumsum(tri, x_ref[0, j * cb:(j + 1) * cb, :]) + carry
        o_ref[0, j * cb:(j + 1) * cb, :] = f
        carry = f[cb - 1:cb, :]


def _cumsum(logf):
    b, length, h = logf.shape
    cb = _largest_block(length, 512, 8)
    spec = pl.BlockSpec((1, length, h), lambda i: (i, 0, 0))
    return pl.pallas_call(
        functools.partial(_cumsum_body, cb=cb),
        grid=(b,),
        in_specs=[spec],
        out_specs=spec,
        out_shape=jax.ShapeDtypeStruct(logf.shape, F32),
        compiler_params=_params("parallel"),
    )(logf)


def _attn_body(q_ref, k_ref, v_ref, fc_ref, fr_ref, o_ref, *, q_start):
    tq, kmax = q_ref.shape[0], k_ref.shape[1]
    qpos = q_start + lax.broadcasted_iota(jnp.int32, (tq, kmax), 0)
    kpos = lax.broadcasted_iota(jnp.int32, (tq, kmax), 1)
    mask = kpos <= qpos
    kb = k_ref[0].astype(BF16)
    vb = v_ref[0].astype(BF16)
    q = q_ref[...]
    fc = fc_ref[...]
    fr = fr_ref[0]
    outs = []
    for h in range(ATT_HEADS):
        g = h // ATT_GROUP
        hs = slice(h * ATT_HEAD_DIM, (h + 1) * ATT_HEAD_DIM)
        gs = slice(g * ATT_HEAD_DIM, (g + 1) * ATT_HEAD_DIM)
        s = lax.dot_general(q[:, hs], kb[:, gs], (((1,), (1,)), ((), ())), preferred_element_type=F32)
        s = s + fc[:, h:h + 1] - fr[h:h + 1, :]
        s = jnp.where(mask, s, -jnp.inf)
        p = jnp.exp(s - jnp.max(s, axis=-1, keepdims=True))
        l = jnp.sum(p, axis=-1, keepdims=True)
        o = jnp.dot(p.astype(BF16), vb[:, gs], preferred_element_type=F32)
        outs.append(o / l)
    o_ref[0] = jnp.concatenate(outs, axis=-1).astype(BF16)


def _attn(q, k, v, f_col, f_row, *, batch, q_off):
    t_q = q.shape[0] // batch
    lk = k.shape[1]
    tq = _largest_block(t_q, 384, 8)
    nq = t_q // tq
    outs = []
    for c in range(nq):
        q_start = q_off + c * tq
        kmax = min(lk, -(-(q_start + tq) // LANES) * LANES)
        qspec = lambda w, c=c: pl.BlockSpec((tq, w), lambda b: (b * nq + c, 0))
        kspec = pl.BlockSpec((1, kmax, ATT_KV_WIDTH), lambda b: (b, 0, 0))
        outs.append(pl.pallas_call(
            functools.partial(_attn_body, q_start=q_start),
            grid=(batch,),
            in_specs=[qspec(ATT_WIDTH), kspec, kspec, qspec(ATT_HEADS),
                      pl.BlockSpec((1, ATT_HEADS, kmax), lambda b: (b, 0, 0))],
            out_specs=pl.BlockSpec((1, tq, ATT_WIDTH), lambda b: (b, 0, 0)),
            out_shape=jax.ShapeDtypeStruct((batch, tq, ATT_WIDTH), BF16),
            compiler_params=_params("parallel"),
        )(q, k, v, f_col, f_row))
    return jnp.concatenate(outs, axis=1).reshape(batch * t_q, ATT_WIDTH)


def _hgrn_body(hq_ref, hf_ref, hi_ref, hg_ref, lbl_ref, gain_ref, s0_ref, o_ref, st_ref,
               oi_s, *, chunk, layer):
    length = hq_ref.shape[0]
    lbl = lbl_ref[...]
    e = jnp.exp(lbl - jnp.max(lbl, axis=0, keepdims=True))
    lb = jnp.sum(e[:layer + 1, :], axis=0, keepdims=True) / jnp.sum(e, axis=0, keepdims=True)
    gain = gain_ref[...]
    tri = _lower_tri(chunk)
    rows = lax.broadcasted_iota(jnp.int32, (chunk, 1), 0)

    def chunk_step(c, st):
        r0 = pl.multiple_of(c * chunk, 16)
        f = lb + (1.0 - lb) * jax.nn.sigmoid(hf_ref[pl.ds(r0, chunk), :])
        kk = 1.0 - f
        hq = hq_ref[pl.ds(r0, chunk), :]
        qq = hq * jax.nn.sigmoid(hq)
        iv = hi_ref[pl.ds(r0, chunk), :]
        g = _tri_cumsum(tri, jnp.log(f))
        o_inter = lax.dot_general((qq * jnp.exp(g)).astype(BF16), st.astype(BF16),
                                  (((1,), (1,)), ((), ())), preferred_element_type=F32)
        for t in range(chunk):
            n = (t // SUBLANES + 1) * SUBLANES
            d = jnp.where(rows[:n] <= t, g[t:t + 1, :] - g[:n, :], -jnp.inf)
            a = jnp.exp(d) * kk[:n, :] * qq[t:t + 1, :]
            sc = jnp.sum(a, axis=1, keepdims=True)
            oi_s[t:t + 1, :] = jnp.sum(sc * iv[:n, :], axis=0, keepdims=True)
        o = o_inter + oi_s[...]
        on = _rms(o, gain)
        hg = hg_ref[pl.ds(r0, chunk), :]
        o_ref[pl.ds(r0, chunk), :] = (on * (hg * jax.nn.sigmoid(hg))).astype(BF16)
        g_end = g[chunk - 1:chunk, :]
        kd = kk * jnp.exp(g_end - g)
        upd = lax.dot_general(iv.astype(BF16), kd.astype(BF16),
                              (((0,), (0,)), ((), ())), preferred_element_type=F32)
        return st * jnp.exp(g_end) + upd

    st_ref[0, 0] = lax.fori_loop(0, length // chunk, chunk_step, s0_ref[0, 0])


def _hgrn(hgrp, lb_logits, gain, s0t, *, batch, layer):
    length = hgrp.shape[0] // batch
    chunk = _largest_block(length, 64, 16)
    col = lambda off: pl.BlockSpec((length, HG_DIM), lambda b, h: (b, off + h))
    par = lambda a: pl.BlockSpec((a.shape[0], HG_DIM), lambda b, h: (0, h))
    st = pl.BlockSpec((1, 1, HG_DIM, HG_DIM), lambda b, h: (b, h, 0, 0))
    buf = pltpu.VMEM((chunk, HG_DIM), F32)
    return pl.pallas_call(
        functools.partial(_hgrn_body, chunk=chunk, layer=layer),
        grid=(batch, HG_HEADS),
        in_specs=[col(0), col(HG_HEADS), col(2 * HG_HEADS), col(3 * HG_HEADS), par(lb_logits), par(gain), st],
        out_specs=[pl.BlockSpec((length, HG_DIM), lambda b, h: (b, h)), st],
        out_shape=[jax.ShapeDtypeStruct((batch * length, HG_WIDTH), BF16),
                   jax.ShapeDtypeStruct(s0t.shape, F32)],
        scratch_shapes=[buf],
        compiler_params=_params("parallel", "parallel"),
    )(hgrp, hgrp, hgrp, hgrp, lb_logits, gain, s0t)


def _topk_rows(s, k, payload=None):
    r = s.shape[0]
    iota = lax.broadcasted_iota(jnp.int32, s.shape, 0)
    vals, picks = [], []
    for _ in range(k):
        m = jnp.max(s, axis=0, keepdims=True)
        idx = jnp.min(jnp.where(s == m, iota, r), axis=0, keepdims=True)
        sel = iota == idx
        vals.append(m)
        picks.append(idx if payload is None else jnp.max(jnp.where(sel, payload, -1), axis=0, keepdims=True))
        s = jnp.where(sel, -jnp.inf, s)
    return jnp.concatenate(vals, axis=0), jnp.concatenate(picks, axis=0)


def _pair_candidates(va, ia, vb, ib):
    k, t = va.shape
    vals, ids = [], []
    i = 0
    while i < k:
        n = k // (i + 1)
        run = 1
        if n == 1:
            run = k - i
        vals.append(va[i:i + run] + vb[:n])
        ids.append(ia[i:i + run] * PEER_N_KEYS + ib[:n])
        i += run
    pad = -sum(v.shape[0] for v in vals) % SUBLANES
    if pad:
        vals.append(jnp.full((pad, t), -jnp.inf, F32))
        ids.append(jnp.full((pad, t), -1, jnp.int32))
    return jnp.concatenate(vals, axis=0), jnp.concatenate(ids, axis=0)


def _route_body(x_ref, att_ref, hgo_ref, woa_ref, wob_ref, g2_ref, wq_ref, keys_ref,
                x1_ref, h2_ref, eidx_ref, gate_ref, qp_s):
    x1 = (x_ref[...] + jnp.dot(att_ref[...], woa_ref[...], preferred_element_type=F32)
          + jnp.dot(hgo_ref[...], wob_ref[...], preferred_element_type=F32))
    x1_ref[...] = x1
    h2 = _rms(x1, g2_ref[...])
    h2_ref[...] = h2
    qp_s[...] = jnp.dot(h2.astype(BF16), wq_ref[...], preferred_element_type=F32)
    nt = (((1,), (1,)), ((), ()))

    def head(h, carry):
        c0 = pl.multiple_of(h * 2 * PEER_HALF, 2 * PEER_HALF)
        qa = qp_s[:, pl.ds(c0, PEER_HALF)].astype(BF16)
        qb = qp_s[:, pl.ds(c0 + PEER_HALF, PEER_HALF)].astype(BF16)
        sa = lax.dot_general(keys_ref[h, 0], qa, nt, preferred_element_type=F32)
        sb = lax.dot_general(keys_ref[h, 1], qb, nt, preferred_element_type=F32)
        r0 = pl.multiple_of(h * PEER_TOPK, PEER_TOPK)
        for c in range(sa.shape[1] // LANES):
            ls = slice(c * LANES, (c + 1) * LANES)
            va, ia = _topk_rows(sa[:, ls], PEER_TOPK)
            vb, ib = _topk_rows(sb[:, ls], PEER_TOPK)
            cand, cidx = _pair_candidates(va, ia, vb, ib)
            ts, te = _topk_rows(cand, PEER_TOPK, payload=cidx)
            ex = jnp.exp(ts - ts[0:1])
            eidx_ref[0, pl.ds(r0, PEER_TOPK), pl.ds(c * LANES, LANES)] = te
            gate_ref[0, pl.ds(r0, PEER_TOPK), pl.ds(c * LANES, LANES)] = ex / jnp.sum(ex, axis=0, keepdims=True)
        return carry

    lax.fori_loop(0, PEER_HEADS, head, 0)


def _route(x, att, hgo, woa, wob, g2, wq, keys):
    n = x.shape[0]
    tb = _largest_block(n, TOKEN_BLOCK_ROUTE, LANES)
    nb = n // tb
    row = lambda w: pl.BlockSpec((tb, w), lambda i: (i, 0))
    full = lambda a: pl.BlockSpec(a.shape, lambda i: (0,) * a.ndim)
    sel = pl.BlockSpec((1, PEER_SEL, tb), lambda i: (i, 0, 0))
    return pl.pallas_call(
        _route_body,
        grid=(nb,),
        in_specs=[row(D_MODEL), row(ATT_WIDTH), row(HG_WIDTH), full(woa), full(wob), full(g2), full(wq), full(keys)],
        out_specs=[row(D_MODEL), row(D_MODEL), sel, sel],
        out_shape=[jax.ShapeDtypeStruct((n, D_MODEL), F32),
                   jax.ShapeDtypeStruct((n, D_MODEL), F32),
                   jax.ShapeDtypeStruct((nb, PEER_SEL, tb), jnp.int32),
                   jax.ShapeDtypeStruct((nb, PEER_SEL, tb), F32)],
        scratch_shapes=[pltpu.VMEM((tb, PEER_HEADS * 2 * PEER_HALF), F32)],
        compiler_params=_params("parallel"),
    )(x, att, hgo, woa, wob, g2, wq, keys)


def _pack_bf16_pair(hi, lo):
    bits = lambda a: lax.bitcast_convert_type(a.astype(BF16), jnp.uint16).astype(jnp.uint32)
    return (bits(hi) << 16) | bits(lo)


def _packed_hi(word):
    return lax.bitcast_convert_type(word & jnp.uint32(0xFFFF0000), F32)


def _packed_lo(word):
    return lax.bitcast_convert_type(word << 16, F32)


def _experts_body(eidx_ref, gate_ref, h2_ref, x1_ref, gf_ref, uv_ref, y_ref,
                  idx_s, buf0, buf1, buf2, idx_sem, row_sem, *, tb):
    bufs = (buf0, buf1, buf2)
    idx_cp = pltpu.make_async_copy(eidx_ref, idx_s, idx_sem)
    idx_cp.start()
    idx_cp.wait()

    def issue_group(g, slot_set):
        for j in range(GATHER_GROUP):
            t = g * GATHER_GROUP + j
            for r in range(PEER_SEL):
                pltpu.make_async_copy(uv_ref.at[idx_s[t, r]],
                                      bufs[slot_set].at[j, pl.ds(r * SUBLANES, SUBLANES), :],
                                      row_sem.at[slot_set, j]).start(priority=r % 2)

    def wait(slot_set, j):
        dst = bufs[slot_set].at[j]
        pltpu.make_async_copy(dst, dst, row_sem.at[slot_set, j]).wait()

    gf = gf_ref[...]
    lane_t = lax.broadcasted_iota(jnp.int32, (PEER_SEL, tb), 1)
    n_groups = tb // GATHER_GROUP

    def group(g, slot_set, prefetch):
        for j in range(GATHER_GROUP):
            wait(slot_set, j)
        if prefetch:
            issue_group(g + GATHER_SETS - 1, (slot_set + GATHER_SETS - 1) % GATHER_SETS)
        buf = bufs[slot_set]
        for j in range(GATHER_GROUP):
            t = g * GATHER_GROUP + j
            xrow = h2_ref[pl.ds(t, 1), :]
            tiles = [buf[j, pl.ds(s, PEER_SEL, stride=SUBLANES), :] for s in range(SUBLANES)]
            p = _packed_hi(tiles[0]) * xrow[:, :LANES]
            for s in range(1, SUBLANES):
                p = p + _packed_hi(tiles[s]) * xrow[:, s * LANES:(s + 1) * LANES]
            hcol = jnp.sum(p, axis=1, keepdims=True)
            act = 0.5 * hcol * (1.0 + lax.erf(hcol * (2.0 ** -0.5)))
            gcol = jnp.sum(jnp.where(lane_t == t, gate_ref[0], 0.0), axis=1, keepdims=True)
            w = gcol * act
            o = [jnp.sum(w * _packed_lo(tiles[s]), axis=0, keepdims=True) for s in range(SUBLANES)]
            orow = jnp.concatenate(o, axis=1)
            y_ref[pl.ds(t, 1), :] = _rms(x1_ref[pl.ds(t, 1), :] + orow, gf)

    for s in range(GATHER_SETS - 1):
        issue_group(s, s)
    n_main = (n_groups - (GATHER_SETS - 1)) // GATHER_SETS

    def main(k, carry):
        for s in range(GATHER_SETS):
            group(k * GATHER_SETS + s, s, True)
        return carry

    lax.fori_loop(0, n_main, main, 0)
    for g in range(n_main * GATHER_SETS, n_groups):
        group(g, g % GATHER_SETS, g + GATHER_SETS - 1 < n_groups)


def _experts(eidx, gate, h2, x1, gf, uv):
    n = h2.shape[0]
    tb = TOKEN_BLOCK_EXPERT
    tbr = gate.shape[2]
    assert GATHER_SETS == 3 and tbr % tb == 0 and tb % GATHER_GROUP == 0 and tb // GATHER_GROUP >= GATHER_SETS
    per = tbr // tb
    row = lambda w: pl.BlockSpec((tb, w), lambda i: (i, 0))
    sel = pl.BlockSpec((1, PEER_SEL, tb), lambda i: (i // per, 0, i % per))
    slots = pltpu.VMEM((GATHER_GROUP, PEER_SEL * SUBLANES, LANES), jnp.uint32)
    return pl.pallas_call(
        functools.partial(_experts_body, tb=tb),
        grid=(n // tb,),
        in_specs=[row(PEER_SEL), sel, row(D_MODEL), row(D_MODEL), pl.BlockSpec(gf.shape, lambda i: (0, 0)),
                  pl.BlockSpec(memory_space=pl.ANY)],
        out_specs=row(D_MODEL),
        out_shape=jax.ShapeDtypeStruct((n, D_MODEL), F32),
        scratch_shapes=[pltpu.SMEM((tb, PEER_SEL), jnp.int32), slots, slots, slots,
                        pltpu.SemaphoreType.DMA(()),
                        pltpu.SemaphoreType.DMA((GATHER_SETS, GATHER_GROUP))],
        compiler_params=_params("arbitrary"),
    )(eidx, gate, h2, x1, gf, uv)


def _mixer(x, batch, params, layer, cache=None):
    q, k, v, logf, hgrp = _inproj(x, params["g1"], params["wqkv"], params["wfg"], params["bfg"], params["wh"])
    length = x.shape[0] // batch
    if cache is None:
        k_all = k.reshape(batch, length, ATT_KV_WIDTH)
        v_all = v.reshape(batch, length, ATT_KV_WIDTH)
        logf_all, q_off = logf.reshape(batch, length, ATT_HEADS), 0
        s0t = jnp.zeros((batch, HG_HEADS, HG_DIM, HG_DIM), F32)
    else:
        cache_k, cache_v, cache_logf, state = cache
        past = cache_k.shape[1]
        cat = lambda c, new: jnp.concatenate(
            [c.reshape(batch, past, -1).astype(F32), new.reshape(batch, length, -1)], axis=1)
        k_all, v_all = cat(cache_k, k), cat(cache_v, v)
        logf_all, q_off = cat(cache_logf, logf), past
        s0t = jnp.swapaxes(state.astype(F32), -1, -2)
    f_all = _cumsum(logf_all)
    f_col = f_all[:, q_off:].reshape(batch * length, ATT_HEADS)
    f_row = jnp.swapaxes(f_all, 1, 2)
    att = _attn(q, k_all, v_all, f_col, f_row, batch=batch, q_off=q_off)
    hgo, st = _hgrn(hgrp, params["lb_logits"], params["hg_gain"], s0t, batch=batch, layer=layer)
    return att, hgo, (k, v, logf, jnp.swapaxes(st, -1, -2))


def kernel(x_prompt, x_sample, cache_k, cache_v, cache_logf, state_hgrn, meta_tokens,
           w_in, b_forget, hg_lb_logits, hg_norm_gain, w_out, norm1_gain, norm2_gain,
           peer_w_query, peer_sub_keys, peer_expert_u, peer_expert_v, final_norm_gain):
    depth = w_in.shape[0]
    assert depth == 1, "single-layer trunk: the PEER stage fuses the final norm"
    bp, seq, _ = x_prompt.shape
    bs, dseq, _ = x_sample.shape
    lp = N_META + seq
    meta = jnp.broadcast_to(meta_tokens.astype(x_prompt.dtype)[None], (bp, N_META, D_MODEL))
    xp = jnp.concatenate([meta, x_prompt], axis=1).reshape(bp * lp, D_MODEL)
    xs = x_sample.reshape(bs * dseq, D_MODEL)

    l = 0
    o_fg = ATT_WIDTH + 2 * ATT_KV_WIDTH
    o_h = o_fg + ATT_HEADS
    row = lambda a: a.reshape(1, -1).astype(F32)
    params = {
        "g1": row(norm1_gain[l]),
        "wqkv": w_in[l][:, :o_fg].astype(BF16),
        "wfg": jnp.pad(w_in[l][:, o_fg:o_h], ((0, 0), (0, LANES - ATT_HEADS))).astype(BF16),
        "bfg": row(b_forget[l]),
        "wh": w_in[l][:, o_h:].astype(BF16),
        "lb_logits": hg_lb_logits.astype(F32),
        "hg_gain": row(hg_norm_gain[l]),
    }
    att_p, hgo_p, sp = _mixer(xp, bp, params, l)
    att_s, hgo_s, ss = _mixer(xs, bs, params, l,
                              cache=(cache_k[l], cache_v[l], cache_logf[l], state_hgrn[l]))

    x_all = jnp.concatenate([xp, xs], axis=0)
    att = jnp.concatenate([att_p, att_s], axis=0)
    hgo = jnp.concatenate([hgo_p, hgo_s], axis=0)
    wo = w_out[l].astype(BF16)
    x1, h2, eidx, gate = _route(x_all, att, hgo, wo[:ATT_WIDTH], wo[ATT_WIDTH:], row(norm2_gain[l]),
                                peer_w_query[l].astype(BF16), peer_sub_keys[l].astype(BF16))
    n_exp = peer_expert_u.shape[1]
    uv = _pack_bf16_pair(peer_expert_u[l], peer_expert_v[l]).reshape(n_exp, SUBLANES, LANES)
    eidx_t = jnp.swapaxes(eidx, 1, 2).reshape(-1, PEER_SEL)
    y = _experts(eidx_t, gate, h2, x1, row(final_norm_gain), uv)

    n_p = bp * lp
    y_prompt = y[:n_p].reshape(bp, lp, D_MODEL)[:, N_META:]
    y_sample = y[n_p:].reshape(bs, dseq, D_MODEL)

    def states(s, batch, length):
        k, v, logf, st = s
        return (k.reshape(1, batch, length, ATT_KV_HEADS, ATT_HEAD_DIM),
                v.reshape(1, batch, length, ATT_KV_HEADS, ATT_HEAD_DIM),
                logf.reshape(1, batch, length, ATT_HEADS),
                st.reshape(1, batch, HG_HEADS, HG_DIM, HG_DIM))

    return (y_prompt, y_sample) + states(sp, bp, lp) + states(ss, bs, dseq)
```

```python
import functools

import jax
import jax.numpy as jnp
from jax import lax
from jax.experimental import pallas as pl
from jax.experimental.pallas import tpu as pltpu

F32 = jnp.float32
BF16 = jnp.bfloat16
EPS = 1e-6
N_META = 16

D_MODEL = 1024
ATT_HEADS = 8
ATT_KV_HEADS = 4
ATT_HEAD_DIM = 64
ATT_GROUP = ATT_HEADS // ATT_KV_HEADS
ATT_WIDTH = ATT_HEADS * ATT_HEAD_DIM
ATT_KV_WIDTH = ATT_KV_HEADS * ATT_HEAD_DIM
HG_HEADS = 4
HG_DIM = 128
HG_WIDTH = HG_HEADS * HG_DIM
PEER_HEADS = 8
PEER_N_KEYS = 128
PEER_TOPK = 16
PEER_HALF = 128
PEER_SEL = PEER_HEADS * PEER_TOPK

LANES = 128
SUBLANES = 8
VMEM_LIMIT = 48 * 1024 * 1024

TOKEN_BLOCK_PROJ = 512
ROUTE_TOKENS = SUBLANES * LANES
TOKEN_BLOCK_EXPERT = 128
GATHER_GROUP = 4
GATHER_SETS = 3


def _params(*sem):
    return pltpu.CompilerParams(dimension_semantics=sem, vmem_limit_bytes=VMEM_LIMIT)


def _largest_block(n, cap, mult):
    best = None
    for d in range(mult, cap + 1, mult):
        if n % d == 0:
            best = d
    assert best is not None, (n, cap, mult)
    return best


def _split3(x):
    hi = x.astype(BF16)
    r1 = x - hi.astype(F32)
    mid = r1.astype(BF16)
    lo = (r1 - mid.astype(F32)).astype(BF16)
    return hi, mid, lo


def _tri_cumsum(tri, x):
    hi, mid, lo = _split3(x)
    d = lambda a: jnp.dot(tri, a, preferred_element_type=F32)
    return (d(lo) + d(mid)) + d(hi)


def _lower_tri(c):
    r = lax.broadcasted_iota(jnp.int32, (c, c), 0)
    s = lax.broadcasted_iota(jnp.int32, (c, c), 1)
    return jnp.where(s <= r, 1.0, 0.0).astype(BF16)


def _rms(x, gain):
    return x * lax.rsqrt(jnp.mean(x * x, axis=-1, keepdims=True) + EPS) * gain


def _inproj_body(x_ref, g_ref, wqkv_ref, wfg_ref, bfg_ref, wh_ref,
                 q_ref, k_ref, v_ref, logf_ref, hgrp_ref):
    hb = _rms(x_ref[...], g_ref[...]).astype(BF16)
    qkv = jnp.dot(hb, wqkv_ref[...], preferred_element_type=F32)
    q_ref[...] = (qkv[:, :ATT_WIDTH] * (ATT_HEAD_DIM ** -0.5)).astype(BF16)
    k_ref[...] = qkv[:, ATT_WIDTH:ATT_WIDTH + ATT_KV_WIDTH]
    v_ref[...] = qkv[:, ATT_WIDTH + ATT_KV_WIDTH:]
    fg = jnp.dot(hb, wfg_ref[...], preferred_element_type=F32)[:, :ATT_HEADS] + bfg_ref[...]
    logf_ref[...] = jnp.minimum(fg, 0.0) - jnp.log1p(jnp.exp(-jnp.abs(fg)))
    hgrp_ref[...] = jnp.dot(hb, wh_ref[...], preferred_element_type=F32)


def _inproj(x, gain, wqkv, wfg, bfg, wh):
    n = x.shape[0]
    tb = _largest_block(n, TOKEN_BLOCK_PROJ, 8)
    row = lambda w: pl.BlockSpec((tb, w), lambda i: (i, 0))
    full = lambda a: pl.BlockSpec(a.shape, lambda i: (0,) * a.ndim)
    return pl.pallas_call(
        _inproj_body,
        grid=(n // tb,),
        in_specs=[row(D_MODEL), full(gain), full(wqkv), full(wfg), full(bfg), full(wh)],
        out_specs=[row(ATT_WIDTH), row(ATT_KV_WIDTH), row(ATT_KV_WIDTH), row(ATT_HEADS), row(4 * HG_WIDTH)],
        out_shape=[jax.ShapeDtypeStruct((n, ATT_WIDTH), BF16),
                   jax.ShapeDtypeStruct((n, ATT_KV_WIDTH), F32),
                   jax.ShapeDtypeStruct((n, ATT_KV_WIDTH), F32),
                   jax.ShapeDtypeStruct((n, ATT_HEADS), F32),
                   jax.ShapeDtypeStruct((n, 4 * HG_WIDTH), F32)],
        compiler_params=_params("parallel"),
    )(x, gain, wqkv, wfg, bfg, wh)


def _cumsum_body(x_ref, o_ref, *, cb):
    length, width = x_ref.shape[1], x_ref.shape[2]
    tri = _lower_tri(cb)
    carry = jnp.zeros((1, width), F32)
    for j in range(length // cb):
        f = _tri_cumsum(tri, x_ref[0, j * cb:(j + 1) * cb, :]) + carry
        o_ref[0, j * cb:(j + 1) * cb, :] = f
        carry = f[cb - 1:cb, :]


def _cumsum(logf):
    b, length, h = logf.shape
    cb = _largest_block(length, 512, 8)
    spec = pl.BlockSpec((1, length, h), lambda i: (i, 0, 0))
    return pl.pallas_call(
        functools.partial(_cumsum_body, cb=cb),
        grid=(b,),
        in_specs=[spec],
        out_specs=spec,
        out_shape=jax.ShapeDtypeStruct(logf.shape, F32),
        compiler_params=_params("parallel"),
    )(logf)


def _attn_body(q_ref, k_ref, v_ref, fc_ref, fr_ref, o_ref, *, q_start):
    tq, kmax = q_ref.shape[0], k_ref.shape[1]
    qpos = q_start + lax.broadcasted_iota(jnp.int32, (tq, kmax), 0)
    kpos = lax.broadcasted_iota(jnp.int32, (tq, kmax), 1)
    mask = kpos <= qpos
    kb = k_ref[0].astype(BF16)
    vb = v_ref[0].astype(BF16)
    q = q_ref[...]
    fc = fc_ref[...]
    fr = fr_ref[0]
    outs = []
    for h in range(ATT_HEADS):
        g = h // ATT_GROUP
        hs = slice(h * ATT_HEAD_DIM, (h + 1) * ATT_HEAD_DIM)
        gs = slice(g * ATT_HEAD_DIM, (g + 1) * ATT_HEAD_DIM)
        s = lax.dot_general(q[:, hs], kb[:, gs], (((1,), (1,)), ((), ())), preferred_element_type=F32)
        s = s + fc[:, h:h + 1] - fr[h:h + 1, :]
        s = jnp.where(mask, s, -jnp.inf)
        p = jnp.exp(s - jnp.max(s, axis=-1, keepdims=True))
        l = jnp.sum(p, axis=-1, keepdims=True)
        o = jnp.dot(p.astype(BF16), vb[:, gs], preferred_element_type=F32)
        outs.append(o / l)
    o_ref[0] = jnp.concatenate(outs, axis=-1).astype(BF16)


def _attn(q, k, v, f_col, f_row, *, batch, q_off):
    t_q = q.shape[0] // batch
    lk = k.shape[1]
    tq = _largest_block(t_q, 384, 8)
    nq = t_q // tq
    outs = []
    for c in range(nq):
        q_start = q_off + c * tq
        kmax = min(lk, -(-(q_start + tq) // LANES) * LANES)
        qspec = lambda w, c=c: pl.BlockSpec((tq, w), lambda b: (b * nq + c, 0))
        kspec = pl.BlockSpec((1, kmax, ATT_KV_WIDTH), lambda b: (b, 0, 0))
        outs.append(pl.pallas_call(
            functools.partial(_attn_body, q_start=q_start),
            grid=(batch,),
            in_specs=[qspec(ATT_WIDTH), kspec, kspec, qspec(ATT_HEADS),
                      pl.BlockSpec((1, ATT_HEADS, kmax), lambda b: (b, 0, 0))],
            out_specs=pl.BlockSpec((1, tq, ATT_WIDTH), lambda b: (b, 0, 0)),
            out_shape=jax.ShapeDtypeStruct((batch, tq, ATT_WIDTH), BF16),
            compiler_params=_params("parallel"),
        )(q, k, v, f_col, f_row))
    return jnp.concatenate(outs, axis=1).reshape(batch * t_q, ATT_WIDTH)


def _hgrn_body(hq_ref, hf_ref, hi_ref, hg_ref, lbl_ref, gain_ref, s0_ref, o_ref, st_ref,
               oi_s, *, chunk, layer):
    length = hq_ref.shape[0]
    lbl = lbl_ref[...]
    e = jnp.exp(lbl - jnp.max(lbl, axis=0, keepdims=True))
    lb = jnp.sum(e[:layer + 1, :], axis=0, keepdims=True) / jnp.sum(e, axis=0, keepdims=True)
    gain = gain_ref[...]
    tri = _lower_tri(chunk)
    rows = lax.broadcasted_iota(jnp.int32, (chunk, 1), 0)

    def chunk_step(c, st):
        r0 = pl.multiple_of(c * chunk, 16)
        f = lb + (1.0 - lb) * jax.nn.sigmoid(hf_ref[pl.ds(r0, chunk), :])
        kk = 1.0 - f
        hq = hq_ref[pl.ds(r0, chunk), :]
        qq = hq * jax.nn.sigmoid(hq)
        iv = hi_ref[pl.ds(r0, chunk), :]
        g = _tri_cumsum(tri, jnp.log(f))
        o_inter = lax.dot_general((qq * jnp.exp(g)).astype(BF16), st.astype(BF16),
                                  (((1,), (1,)), ((), ())), preferred_element_type=F32)
        for t in range(chunk):
            n = (t // SUBLANES + 1) * SUBLANES
            d = jnp.where(rows[:n] <= t, g[t:t + 1, :] - g[:n, :], -jnp.inf)
            a = jnp.exp(d) * kk[:n, :] * qq[t:t + 1, :]
            sc = jnp.sum(a, axis=1, keepdims=True)
            oi_s[t:t + 1, :] = jnp.sum(sc * iv[:n, :], axis=0, keepdims=True)
        o = o_inter + oi_s[...]
        on = _rms(o, gain)
        hg = hg_ref[pl.ds(r0, chunk), :]
        o_ref[pl.ds(r0, chunk), :] = (on * (hg * jax.nn.sigmoid(hg))).astype(BF16)
        g_end = g[chunk - 1:chunk, :]
        kd = kk * jnp.exp(g_end - g)
        upd = lax.dot_general(iv.astype(BF16), kd.astype(BF16),
                              (((0,), (0,)), ((), ())), preferred_element_type=F32)
        return st * jnp.exp(g_end) + upd

    st_ref[0, 0] = lax.fori_loop(0, length // chunk, chunk_step, s0_ref[0, 0])


def _hgrn(hgrp, lb_logits, gain, s0t, *, batch, layer):
    length = hgrp.shape[0] // batch
    chunk = _largest_block(length, 64, 16)
    col = lambda off: pl.BlockSpec((length, HG_DIM), lambda b, h: (b, off + h))
    par = lambda a: pl.BlockSpec((a.shape[0], HG_DIM), lambda b, h: (0, h))
    st = pl.BlockSpec((1, 1, HG_DIM, HG_DIM), lambda b, h: (b, h, 0, 0))
    buf = pltpu.VMEM((chunk, HG_DIM), F32)
    return pl.pallas_call(
        functools.partial(_hgrn_body, chunk=chunk, layer=layer),
        grid=(batch, HG_HEADS),
        in_specs=[col(0), col(HG_HEADS), col(2 * HG_HEADS), col(3 * HG_HEADS), par(lb_logits), par(gain), st],
        out_specs=[pl.BlockSpec((length, HG_DIM), lambda b, h: (b, h)), st],
        out_shape=[jax.ShapeDtypeStruct((batch * length, HG_WIDTH), BF16),
                   jax.ShapeDtypeStruct(s0t.shape, F32)],
        scratch_shapes=[buf],
        compiler_params=_params("parallel", "parallel"),
    )(hgrp, hgrp, hgrp, hgrp, lb_logits, gain, s0t)


def _proj_body(x_ref, att_ref, hgo_ref, woa_ref, wob_ref, g2_ref, wq_ref, keys_ref, x1_ref, h2_ref, s_ref):
    x1 = (x_ref[...] + jnp.dot(att_ref[...], woa_ref[...], preferred_element_type=F32)
          + jnp.dot(hgo_ref[...], wob_ref[...], preferred_element_type=F32))
    x1_ref[...] = x1
    h2 = _rms(x1, g2_ref[...])
    h2_ref[...] = h2
    qp = jnp.dot(h2.astype(BF16), wq_ref[...], preferred_element_type=F32)
    for h in range(PEER_HEADS):
        for c in range(2):
            col = (2 * h + c) * PEER_HALF
            s_ref[2 * h + c] = lax.dot_general(keys_ref[h, c], qp[:, col:col + PEER_HALF].astype(BF16),
                                               (((1,), (1,)), ((), ())), preferred_element_type=F32)


def _proj(x, att, hgo, woa, wob, g2, wq, keys):
    n = x.shape[0]
    tb = _largest_block(n, TOKEN_BLOCK_PROJ, LANES)
    row = lambda w: pl.BlockSpec((tb, w), lambda i: (i, 0))
    full = lambda a: pl.BlockSpec(a.shape, lambda i: (0,) * a.ndim)
    return pl.pallas_call(
        _proj_body,
        grid=(n // tb,),
        in_specs=[row(D_MODEL), row(ATT_WIDTH), row(HG_WIDTH), full(woa), full(wob), full(g2), full(wq), full(keys)],
        out_specs=[row(D_MODEL), row(D_MODEL), pl.BlockSpec((2 * PEER_HEADS, PEER_N_KEYS, tb), lambda i: (0, 0, i))],
        out_shape=[jax.ShapeDtypeStruct((n, D_MODEL), F32),
                   jax.ShapeDtypeStruct((n, D_MODEL), F32),
                   jax.ShapeDtypeStruct((2 * PEER_HEADS, PEER_N_KEYS, n), F32)],
        compiler_params=_params("parallel"),
    )(x, att, hgo, woa, wob, g2, wq, keys)


def _stream_topk(w_ref, n, k_out, emit, payload_ref=None):
    n_acc = 4
    shape = w_ref.shape[1:]

    def step(j, prev):
        accs = [(jnp.full(shape, -jnp.inf, F32), jnp.full(shape, n, jnp.int32), jnp.full(shape, -1, jnp.int32))
                for _ in range(n_acc)]
        for k in range(n):
            v = jnp.where(prev == k, -jnp.inf, w_ref[k])
            w_ref[k] = v
            m, pos, pay = accs[k % n_acc]
            new = v > m
            pay = pay if payload_ref is None else jnp.where(new, payload_ref[k], pay)
            accs[k % n_acc] = (jnp.where(new, v, m), jnp.where(new, k, pos), pay)
        while len(accs) > 1:
            a, b = accs[0], accs[1]
            take = (b[0] > a[0]) | ((b[0] == a[0]) & (b[1] < a[1]))
            accs = accs[2:] + [tuple(jnp.where(take, y, x) for x, y in zip(a, b))]
        m, pos, pay = accs[0]
        emit(j, m, pos if payload_ref is None else pay)
        return pos

    lax.fori_loop(0, k_out, step, jnp.full(shape, n, jnp.int32))


def _pair_list(k):
    return [(i, j) for i in range(k) for j in range(k // (i + 1))]


def _topk_body(s_ref, eidx_ref, gate_ref, stage_s, w_s, va_s, ia_s, vb_s, ib_s, cand_s, cid_s, ts_s):
    tiles = ROUTE_TOKENS // LANES
    for c, (v_s, i_s) in enumerate(((va_s, ia_s), (vb_s, ib_s))):
        for lt in range(tiles):
            stage_s[lt * PEER_N_KEYS:(lt + 1) * PEER_N_KEYS, :] = s_ref[c, :, lt * LANES:(lt + 1) * LANES]
        for k in range(PEER_N_KEYS):
            w_s[k] = stage_s[pl.ds(k, tiles, stride=PEER_N_KEYS), :]

        def emit(j, m, pos, v_s=v_s, i_s=i_s):
            v_s[j] = m
            i_s[j] = pos

        _stream_topk(w_s, PEER_N_KEYS, PEER_TOPK, emit)
    pairs = _pair_list(PEER_TOPK)
    for p, (i, j) in enumerate(pairs):
        cand_s[p] = va_s[i] + vb_s[j]
        cid_s[p] = ia_s[i] * PEER_N_KEYS + ib_s[j]

    def emit_final(j, m, pay):
        ts_s[j] = m
        eidx_ref[0, j] = pay

    _stream_topk(cand_s, len(pairs), PEER_TOPK, emit_final, payload_ref=cid_s)
    ex = [jnp.exp(ts_s[j] - ts_s[0]) for j in range(PEER_TOPK)]
    tot = ex[0]
    for e in ex[1:]:
        tot = tot + e
    for j in range(PEER_TOPK):
        gate_ref[0, j] = ex[j] / tot


def _route_topk(scores):
    n = scores.shape[2]
    assert n % ROUTE_TOKENS == 0 and ROUTE_TOKENS // LANES == SUBLANES
    nb = n // ROUTE_TOKENS
    n_pairs = len(_pair_list(PEER_TOPK))
    tile = lambda rows, dt: pltpu.VMEM((rows, SUBLANES, LANES), dt)
    sel = pl.BlockSpec((1, PEER_TOPK, SUBLANES, LANES), lambda i, h: (i, h, 0, 0))
    return pl.pallas_call(
        _topk_body,
        grid=(nb, PEER_HEADS),
        in_specs=[pl.BlockSpec((2, PEER_N_KEYS, ROUTE_TOKENS), lambda i, h: (h, 0, i))],
        out_specs=[sel, sel],
        out_shape=[jax.ShapeDtypeStruct((nb, PEER_SEL, SUBLANES, LANES), jnp.int32),
                   jax.ShapeDtypeStruct((nb, PEER_SEL, SUBLANES, LANES), F32)],
        scratch_shapes=[pltpu.VMEM((SUBLANES * PEER_N_KEYS, LANES), F32), tile(PEER_N_KEYS, F32),
                        tile(PEER_TOPK, F32), tile(PEER_TOPK, jnp.int32),
                        tile(PEER_TOPK, F32), tile(PEER_TOPK, jnp.int32),
                        tile(n_pairs, F32), tile(n_pairs, jnp.int32), tile(PEER_TOPK, F32)],
        compiler_params=_params("parallel", "parallel"),
    )(scores)


def _pack_bf16_pair(hi, lo):
    bits = lambda a: lax.bitcast_convert_type(a.astype(BF16), jnp.uint16).astype(jnp.uint32)
    return (bits(hi) << 16) | bits(lo)


def _packed_hi(word):
    return lax.bitcast_convert_type(word & jnp.uint32(0xFFFF0000), F32)


def _packed_lo(word):
    return lax.bitcast_convert_type(word << 16, F32)


def _experts_body(eidx_ref, gate_ref, h2_ref, x1_ref, gf_ref, uv_ref, y_ref,
                  idx_s, buf0, buf1, buf2, idx_sem, row_sem, *, tb):
    bufs = (buf0, buf1, buf2)
    idx_cp = pltpu.make_async_copy(eidx_ref, idx_s, idx_sem)
    idx_cp.start()
    idx_cp.wait()

    def issue_group(g, slot_set):
        for j in range(GATHER_GROUP):
            t = g * GATHER_GROUP + j
            for r in range(PEER_SEL):
                pltpu.make_async_copy(uv_ref.at[idx_s[t, r]],
                                      bufs[slot_set].at[j, pl.ds(r * SUBLANES, SUBLANES), :],
                                      row_sem.at[slot_set, j]).start(priority=r % 2)

    def wait(slot_set, j):
        dst = bufs[slot_set].at[j]
        pltpu.make_async_copy(dst, dst, row_sem.at[slot_set, j]).wait()

    gf = gf_ref[...]
    lane_t = lax.broadcasted_iota(jnp.int32, (PEER_SEL, tb), 1)
    n_groups = tb // GATHER_GROUP

    def group(g, slot_set, prefetch):
        for j in range(GATHER_GROUP):
            wait(slot_set, j)
        if prefetch:
            issue_group(g + GATHER_SETS - 1, (slot_set + GATHER_SETS - 1) % GATHER_SETS)
        buf = bufs[slot_set]
        for j in range(GATHER_GROUP):
            t = g * GATHER_GROUP + j
            xrow = h2_ref[pl.ds(t, 1), :]
            tiles = [buf[j, pl.ds(s, PEER_SEL, stride=SUBLANES), :] for s in range(SUBLANES)]
            p = _packed_hi(tiles[0]) * xrow[:, :LANES]
            for s in range(1, SUBLANES):
                p = p + _packed_hi(tiles[s]) * xrow[:, s * LANES:(s + 1) * LANES]
            hcol = jnp.sum(p, axis=1, keepdims=True)
            act = 0.5 * hcol * (1.0 + lax.erf(hcol * (2.0 ** -0.5)))
            gcol = jnp.sum(jnp.where(lane_t == t, gate_ref[0], 0.0), axis=1, keepdims=True)
            w = gcol * act
            o = [jnp.sum(w * _packed_lo(tiles[s]), axis=0, keepdims=True) for s in range(SUBLANES)]
            orow = jnp.concatenate(o, axis=1)
            y_ref[pl.ds(t, 1), :] = _rms(x1_ref[pl.ds(t, 1), :] + orow, gf)

    for s in range(GATHER_SETS - 1):
        issue_group(s, s)
    n_main = (n_groups - (GATHER_SETS - 1)) // GATHER_SETS

    def main(k, carry):
        for s in range(GATHER_SETS):
            group(k * GATHER_SETS + s, s, True)
        return carry

    lax.fori_loop(0, n_main, main, 0)
    for g in range(n_main * GATHER_SETS, n_groups):
        group(g, g % GATHER_SETS, g + GATHER_SETS - 1 < n_groups)


def _experts(eidx, gate, h2, x1, gf, uv):
    n = h2.shape[0]
    tb = TOKEN_BLOCK_EXPERT
    tbr = gate.shape[2]
    assert GATHER_SETS == 3 and tbr % tb == 0 and tb % GATHER_GROUP == 0 and tb // GATHER_GROUP >= GATHER_SETS
    per = tbr // tb
    row = lambda w: pl.BlockSpec((tb, w), lambda i: (i, 0))
    sel = pl.BlockSpec((1, PEER_SEL, tb), lambda i: (i // per, 0, i % per))
    slots = pltpu.VMEM((GATHER_GROUP, PEER_SEL * SUBLANES, LANES), jnp.uint32)
    return pl.pallas_call(
        functools.partial(_experts_body, tb=tb),
        grid=(n // tb,),
        in_specs=[row(PEER_SEL), sel, row(D_MODEL), row(D_MODEL), pl.BlockSpec(gf.shape, lambda i: (0, 0)),
                  pl.BlockSpec(memory_space=pl.ANY)],
        out_specs=row(D_MODEL),
        out_shape=jax.ShapeDtypeStruct((n, D_MODEL), F32),
        scratch_shapes=[pltpu.SMEM((tb, PEER_SEL), jnp.int32), slots, slots, slots,
                        pltpu.SemaphoreType.DMA(()),
                        pltpu.SemaphoreType.DMA((GATHER_SETS, GATHER_GROUP))],
        compiler_params=_params("arbitrary"),
    )(eidx, gate, h2, x1, gf, uv)


def _mixer(x, batch, params, layer, cache=None):
    q, k, v, logf, hgrp = _inproj(x, params["g1"], params["wqkv"], params["wfg"], params["bfg"], params["wh"])
    length = x.shape[0] // batch
    if cache is None:
        k_all = k.reshape(batch, length, ATT_KV_WIDTH)
        v_all = v.reshape(batch, length, ATT_KV_WIDTH)
        logf_all, q_off = logf.reshape(batch, length, ATT_HEADS), 0
        s0t = jnp.zeros((batch, HG_HEADS, HG_DIM, HG_DIM), F32)
    else:
        cache_k, cache_v, cache_logf, state = cache
        past = cache_k.shape[1]
        cat = lambda c, new: jnp.concatenate(
            [c.reshape(batch, past, -1).astype(F32), new.reshape(batch, length, -1)], axis=1)
        k_all, v_all = cat(cache_k, k), cat(cache_v, v)
        logf_all, q_off = cat(cache_logf, logf), past
        s0t = jnp.swapaxes(state.astype(F32), -1, -2)
    f_all = _cumsum(logf_all)
    f_col = f_all[:, q_off:].reshape(batch * length, ATT_HEADS)
    f_row = jnp.swapaxes(f_all, 1, 2)
    att = _attn(q, k_all, v_all, f_col, f_row, batch=batch, q_off=q_off)
    hgo, st = _hgrn(hgrp, params["lb_logits"], params["hg_gain"], s0t, batch=batch, layer=layer)
    return att, hgo, (k, v, logf, jnp.swapaxes(st, -1, -2))


def kernel(x_prompt, x_sample, cache_k, cache_v, cache_logf, state_hgrn, meta_tokens,
           w_in, b_forget, hg_lb_logits, hg_norm_gain, w_out, norm1_gain, norm2_gain,
           peer_w_query, peer_sub_keys, peer_expert_u, peer_expert_v, final_norm_gain):
    depth = w_in.shape[0]
    assert depth == 1, "single-layer trunk: the PEER stage fuses the final norm"
    assert D_MODEL == SUBLANES * LANES, "an expert's packed u|v row must fill exactly one (SUBLANES, LANES) slab"
    bp, seq, _ = x_prompt.shape
    bs, dseq, _ = x_sample.shape
    lp = N_META + seq
    meta = jnp.broadcast_to(meta_tokens.astype(x_prompt.dtype)[None], (bp, N_META, D_MODEL))
    xp = jnp.concatenate([meta, x_prompt], axis=1).reshape(bp * lp, D_MODEL)
    xs = x_sample.reshape(bs * dseq, D_MODEL)

    l = 0
    o_fg = ATT_WIDTH + 2 * ATT_KV_WIDTH
    o_h = o_fg + ATT_HEADS
    row = lambda a: a.reshape(1, -1).astype(F32)
    params = {
        "g1": row(norm1_gain[l]),
        "wqkv": w_in[l][:, :o_fg].astype(BF16),
        "wfg": jnp.pad(w_in[l][:, o_fg:o_h], ((0, 0), (0, LANES - ATT_HEADS))).astype(BF16),
        "bfg": row(b_forget[l]),
        "wh": w_in[l][:, o_h:].astype(BF16),
        "lb_logits": hg_lb_logits.astype(F32),
        "hg_gain": row(hg_norm_gain[l]),
    }
    att_p, hgo_p, sp = _mixer(xp, bp, params, l)
    att_s, hgo_s, ss = _mixer(xs, bs, params, l,
                              cache=(cache_k[l], cache_v[l], cache_logf[l], state_hgrn[l]))

    x_all = jnp.concatenate([xp, xs], axis=0)
    att = jnp.concatenate([att_p, att_s], axis=0)
    hgo = jnp.concatenate([hgo_p, hgo_s], axis=0)
    wo = w_out[l].astype(BF16)
    x1, h2, scores = _proj(x_all, att, hgo, wo[:ATT_WIDTH], wo[ATT_WIDTH:], row(norm2_gain[l]),
                           peer_w_query[l].astype(BF16), peer_sub_keys[l].astype(BF16))
    eidx, gate = (a.reshape(-1, PEER_SEL, ROUTE_TOKENS) for a in _route_topk(scores))
    n_exp = peer_expert_u.shape[1]
    uv = _pack_bf16_pair(peer_expert_u[l], peer_expert_v[l]).reshape(n_exp, SUBLANES, LANES)
    eidx_t = jnp.swapaxes(eidx, 1, 2).reshape(-1, PEER_SEL)
    y = _experts(eidx_t, gate, h2, x1, row(final_norm_gain), uv)

    n_p = bp * lp
    y_prompt = y[:n_p].reshape(bp, lp, D_MODEL)[:, N_META:]
    y_sample = y[n_p:].reshape(bs, dseq, D_MODEL)

    def states(s, batch, length):
        k, v, logf, st = s
        return (k.reshape(1, batch, length, ATT_KV_HEADS, ATT_HEAD_DIM),
                v.reshape(1, batch, length, ATT_KV_HEADS, ATT_HEAD_DIM),
                logf.reshape(1, batch, length, ATT_HEADS),
                st.reshape(1, batch, HG_HEADS, HG_DIM, HG_DIM))

    return (y_prompt, y_sample) + states(sp, bp, lp) + states(ss, bs, dseq)
```

```python
import functools

import jax
import jax.numpy as jnp
from jax import lax
from jax.experimental import pallas as pl
from jax.experimental.pallas import tpu as pltpu

F32 = jnp.float32
BF16 = jnp.bfloat16
EPS = 1e-6
N_META = 16

D_MODEL = 1024
ATT_HEADS = 8
ATT_KV_HEADS = 4
ATT_HEAD_DIM = 64
ATT_GROUP = ATT_HEADS // ATT_KV_HEADS
ATT_WIDTH = ATT_HEADS * ATT_HEAD_DIM
ATT_KV_WIDTH = ATT_KV_HEADS * ATT_HEAD_DIM
HG_HEADS = 4
HG_DIM = 128
HG_WIDTH = HG_HEADS * HG_DIM
HG_PER_STEP = 2
PEER_HEADS = 8
PEER_N_KEYS = 128
PEER_TOPK = 16
PEER_HALF = 128
PEER_SEL = PEER_HEADS * PEER_TOPK

LANES = 128
SUBLANES = 8
VMEM_LIMIT = 48 * 1024 * 1024

TOKEN_BLOCK_PROJ = 512
ROUTE_TOKENS = SUBLANES * LANES
TOKEN_BLOCK_EXPERT = 128
GATHER_GROUP = 4
GATHER_SETS = 3


def _params(*sem):
    return pltpu.CompilerParams(dimension_semantics=sem, vmem_limit_bytes=VMEM_LIMIT)


def _largest_block(n, cap, mult):
    best = None
    for d in range(mult, cap + 1, mult):
        if n % d == 0:
            best = d
    assert best is not None, (n, cap, mult)
    return best


def _split3(x):
    hi = x.astype(BF16)
    r1 = x - hi.astype(F32)
    mid = r1.astype(BF16)
    lo = (r1 - mid.astype(F32)).astype(BF16)
    return hi, mid, lo


def _tri_cumsum(tri, x):
    hi, mid, lo = _split3(x)
    d = lambda a: jnp.dot(tri, a, preferred_element_type=F32)
    return (d(lo) + d(mid)) + d(hi)


def _lower_tri(c):
    r = lax.broadcasted_iota(jnp.int32, (c, c), 0)
    s = lax.broadcasted_iota(jnp.int32, (c, c), 1)
    return jnp.where(s <= r, 1.0, 0.0).astype(BF16)


def _rms(x, gain):
    return x * lax.rsqrt(jnp.mean(x * x, axis=-1, keepdims=True) + EPS) * gain


def _inproj_body(x_ref, g_ref, wqkv_ref, wfg_ref, bfg_ref, wh_ref,
                 q_ref, k_ref, v_ref, logf_ref, hgrp_ref):
    hb = _rms(x_ref[...], g_ref[...]).astype(BF16)
    qkv = jnp.dot(hb, wqkv_ref[...], preferred_element_type=F32)
    q_ref[...] = (qkv[:, :ATT_WIDTH] * (ATT_HEAD_DIM ** -0.5)).astype(BF16)
    k_ref[...] = qkv[:, ATT_WIDTH:ATT_WIDTH + ATT_KV_WIDTH]
    v_ref[...] = qkv[:, ATT_WIDTH + ATT_KV_WIDTH:]
    fg = jnp.dot(hb, wfg_ref[...], preferred_element_type=F32)[:, :ATT_HEADS] + bfg_ref[...]
    logf_ref[...] = jnp.minimum(fg, 0.0) - jnp.log1p(jnp.exp(-jnp.abs(fg)))
    hgrp_ref[...] = jnp.dot(hb, wh_ref[...], preferred_element_type=F32)


def _inproj(x, gain, wqkv, wfg, bfg, wh):
    n = x.shape[0]
    tb = _largest_block(n, TOKEN_BLOCK_PROJ, 8)
    row = lambda w: pl.BlockSpec((tb, w), lambda i: (i, 0))
    full = lambda a: pl.BlockSpec(a.shape, lambda i: (0,) * a.ndim)
    return pl.pallas_call(
        _inproj_body,
        grid=(n // tb,),
        in_specs=[row(D_MODEL), full(gain), full(wqkv), full(wfg), full(bfg), full(wh)],
        out_specs=[row(ATT_WIDTH), row(ATT_KV_WIDTH), row(ATT_KV_WIDTH), row(ATT_HEADS), row(4 * HG_WIDTH)],
        out_shape=[jax.ShapeDtypeStruct((n, ATT_WIDTH), BF16),
                   jax.ShapeDtypeStruct((n, ATT_KV_WIDTH), F32),
                   jax.ShapeDtypeStruct((n, ATT_KV_WIDTH), F32),
                   jax.ShapeDtypeStruct((n, ATT_HEADS), F32),
                   jax.ShapeDtypeStruct((n, 4 * HG_WIDTH), F32)],
        compiler_params=_params("parallel"),
    )(x, gain, wqkv, wfg, bfg, wh)


def _cumsum_body(x_ref, o_ref, *, cb):
    length, width = x_ref.shape[1], x_ref.shape[2]
    tri = _lower_tri(cb)
    carry = jnp.zeros((1, width), F32)
    for j in range(length // cb):
        f = _tri_cumsum(tri, x_ref[0, j * cb:(j + 1) * cb, :]) + carry
        o_ref[0, j * cb:(j + 1) * cb, :] = f
        carry = f[cb - 1:cb, :]


def _cumsum(logf):
    b, length, h = logf.shape
    cb = _largest_block(length, 512, 8)
    spec = pl.BlockSpec((1, length, h), lambda i: (i, 0, 0))
    return pl.pallas_call(
        functools.partial(_cumsum_body, cb=cb),
        grid=(b,),
        in_specs=[spec],
        out_specs=spec,
        out_shape=jax.ShapeDtypeStruct(logf.shape, F32),
        compiler_params=_params("parallel"),
    )(logf)


def _attn_body(q_ref, k_ref, v_ref, fc_ref, fr_ref, o_ref, *, q_start):
    tq, kmax = q_ref.shape[0], k_ref.shape[1]
    qpos = q_start + lax.broadcasted_iota(jnp.int32, (tq, kmax), 0)
    kpos = lax.broadcasted_iota(jnp.int32, (tq, kmax), 1)
    mask = kpos <= qpos
    kb = k_ref[0].astype(BF16)
    vb = v_ref[0].astype(BF16)
    q = q_ref[...]
    fc = fc_ref[...]
    fr = fr_ref[0]
    outs = []
    for h in range(ATT_HEADS):
        g = h // ATT_GROUP
        hs = slice(h * ATT_HEAD_DIM, (h + 1) * ATT_HEAD_DIM)
        gs = slice(g * ATT_HEAD_DIM, (g + 1) * ATT_HEAD_DIM)
        s = lax.dot_general(q[:, hs], kb[:, gs], (((1,), (1,)), ((), ())), preferred_element_type=F32)
        s = s + fc[:, h:h + 1] - fr[h:h + 1, :]
        s = jnp.where(mask, s, -jnp.inf)
        p = jnp.exp(s - jnp.max(s, axis=-1, keepdims=True))
        l = jnp.sum(p, axis=-1, keepdims=True)
        o = jnp.dot(p.astype(BF16), vb[:, gs], preferred_element_type=F32)
        outs.append(o / l)
    o_ref[0] = jnp.concatenate(outs, axis=-1).astype(BF16)


def _attn(q, k, v, f_col, f_row, *, batch, q_off):
    t_q = q.shape[0] // batch
    lk = k.shape[1]
    tq = _largest_block(t_q, 384, 8)
    nq = t_q // tq
    outs = []
    for c in range(nq):
        q_start = q_off + c * tq
        kmax = min(lk, -(-(q_start + tq) // LANES) * LANES)
        qspec = lambda w, c=c: pl.BlockSpec((tq, w), lambda b: (b * nq + c, 0))
        kspec = pl.BlockSpec((1, kmax, ATT_KV_WIDTH), lambda b: (b, 0, 0))
        outs.append(pl.pallas_call(
            functools.partial(_attn_body, q_start=q_start),
            grid=(batch,),
            in_specs=[qspec(ATT_WIDTH), kspec, kspec, qspec(ATT_HEADS),
                      pl.BlockSpec((1, ATT_HEADS, kmax), lambda b: (b, 0, 0))],
            out_specs=pl.BlockSpec((1, tq, ATT_WIDTH), lambda b: (b, 0, 0)),
            out_shape=jax.ShapeDtypeStruct((batch, tq, ATT_WIDTH), BF16),
            compiler_params=_params("parallel"),
        )(q, k, v, f_col, f_row))
    return jnp.concatenate(outs, axis=1).reshape(batch * t_q, ATT_WIDTH)


def _hgrn_body(hq_ref, hf_ref, hi_ref, hg_ref, lbl_ref, gain_ref, s0_ref, o_ref, st_ref,
               oi_s, *, chunk, layer):
    length = hq_ref.shape[0]
    lbl = lbl_ref[...]
    e = jnp.exp(lbl - jnp.max(lbl, axis=0, keepdims=True))
    lb_all = jnp.sum(e[:layer + 1, :], axis=0, keepdims=True) / jnp.sum(e, axis=0, keepdims=True)
    gain_all = gain_ref[...]
    tri = _lower_tri(chunk)
    rows = lax.broadcasted_iota(jnp.int32, (chunk, 1), 0)
    st_ref[...] = s0_ref[...]

    def chunk_step(c, carry):
        r0 = pl.multiple_of(c * chunk, 16)
        for hh in range(HG_PER_STEP):
            ls = slice(hh * HG_DIM, (hh + 1) * HG_DIM)
            lb = lb_all[:, ls]
            st = st_ref[0, hh]
            f = lb + (1.0 - lb) * jax.nn.sigmoid(hf_ref[pl.ds(r0, chunk), ls])
            kk = 1.0 - f
            hq = hq_ref[pl.ds(r0, chunk), ls]
            qq = hq * jax.nn.sigmoid(hq)
            iv = hi_ref[pl.ds(r0, chunk), ls]
            g = _tri_cumsum(tri, jnp.log(f))
            o_inter = lax.dot_general((qq * jnp.exp(g)).astype(BF16), st.astype(BF16),
                                      (((1,), (1,)), ((), ())), preferred_element_type=F32)
            for t in range(chunk):
                n = (t // SUBLANES + 1) * SUBLANES
                d = jnp.where(rows[:n] <= t, g[t:t + 1, :] - g[:n, :], -jnp.inf)
                a = jnp.exp(d) * kk[:n, :] * qq[t:t + 1, :]
                sc = jnp.sum(a, axis=1, keepdims=True)
                oi_s[hh, t:t + 1, :] = jnp.sum(sc * iv[:n, :], axis=0, keepdims=True)
            o = o_inter + oi_s[hh]
            on = _rms(o, gain_all[:, ls])
            hg = hg_ref[pl.ds(r0, chunk), ls]
            o_ref[pl.ds(r0, chunk), ls] = (on * (hg * jax.nn.sigmoid(hg))).astype(BF16)
            g_end = g[chunk - 1:chunk, :]
            kd = kk * jnp.exp(g_end - g)
            upd = lax.dot_general(iv.astype(BF16), kd.astype(BF16),
                                  (((0,), (0,)), ((), ())), preferred_element_type=F32)
            st_ref[0, hh] = st * jnp.exp(g_end) + upd
        return carry

    lax.fori_loop(0, length // chunk, chunk_step, 0)


def _hgrn(hgrp, lb_logits, gain, s0t, *, batch, layer):
    length = hgrp.shape[0] // batch
    chunk = _largest_block(length, 64, 16)
    width = HG_PER_STEP * HG_DIM
    steps = HG_HEADS // HG_PER_STEP
    col = lambda off: pl.BlockSpec((length, width), lambda b, h: (b, off * steps + h))
    par = lambda a: pl.BlockSpec((a.shape[0], width), lambda b, h: (0, h))
    st = pl.BlockSpec((1, HG_PER_STEP, HG_DIM, HG_DIM), lambda b, h: (b, h, 0, 0))
    buf = pltpu.VMEM((HG_PER_STEP, chunk, HG_DIM), F32)
    return pl.pallas_call(
        functools.partial(_hgrn_body, chunk=chunk, layer=layer),
        grid=(batch, steps),
        in_specs=[col(0), col(1), col(2), col(3), par(lb_logits), par(gain), st],
        out_specs=[pl.BlockSpec((length, width), lambda b, h: (b, h)), st],
        out_shape=[jax.ShapeDtypeStruct((batch * length, HG_WIDTH), BF16),
                   jax.ShapeDtypeStruct(s0t.shape, F32)],
        scratch_shapes=[buf],
        compiler_params=_params("parallel", "parallel"),
    )(hgrp, hgrp, hgrp, hgrp, lb_logits, gain, s0t)


def _proj_body(x_ref, att_ref, hgo_ref, woa_ref, wob_ref, g2_ref, wq_ref, keys_ref, x1_ref, h2_ref, s_ref):
    x1 = (x_ref[...] + jnp.dot(att_ref[...], woa_ref[...], preferred_element_type=F32)
          + jnp.dot(hgo_ref[...], wob_ref[...], preferred_element_type=F32))
    x1_ref[...] = x1
    h2 = _rms(x1, g2_ref[...])
    h2_ref[...] = h2
    qp = jnp.dot(h2.astype(BF16), wq_ref[...], preferred_element_type=F32)
    for h in range(PEER_HEADS):
        for c in range(2):
            col = (2 * h + c) * PEER_HALF
            s_ref[2 * h + c] = lax.dot_general(keys_ref[h, c], qp[:, col:col + PEER_HALF].astype(BF16),
                                               (((1,), (1,)), ((), ())), preferred_element_type=F32)


def _proj(x, att, hgo, woa, wob, g2, wq, keys):
    n = x.shape[0]
    tb = _largest_block(n, TOKEN_BLOCK_PROJ, LANES)
    row = lambda w: pl.BlockSpec((tb, w), lambda i: (i, 0))
    full = lambda a: pl.BlockSpec(a.shape, lambda i: (0,) * a.ndim)
    return pl.pallas_call(
        _proj_body,
        grid=(n // tb,),
        in_specs=[row(D_MODEL), row(ATT_WIDTH), row(HG_WIDTH), full(woa), full(wob), full(g2), full(wq), full(keys)],
        out_specs=[row(D_MODEL), row(D_MODEL), pl.BlockSpec((2 * PEER_HEADS, PEER_N_KEYS, tb), lambda i: (0, 0, i))],
        out_shape=[jax.ShapeDtypeStruct((n, D_MODEL), F32),
                   jax.ShapeDtypeStruct((n, D_MODEL), F32),
                   jax.ShapeDtypeStruct((2 * PEER_HEADS, PEER_N_KEYS, n), F32)],
        compiler_params=_params("parallel"),
    )(x, att, hgo, woa, wob, g2, wq, keys)


def _stream_topk(w_ref, n, k_out, emit, payload_ref=None):
    n_acc = 4
    shape = w_ref.shape[1:]

    def step(j, prev):
        accs = [(jnp.full(shape, -jnp.inf, F32), jnp.full(shape, n, jnp.int32), jnp.full(shape, -1, jnp.int32))
                for _ in range(n_acc)]
        for k in range(n):
            v = jnp.where(prev == k, -jnp.inf, w_ref[k])
            w_ref[k] = v
            m, pos, pay = accs[k % n_acc]
            new = v > m
            pay = pay if payload_ref is None else jnp.where(new, payload_ref[k], pay)
            accs[k % n_acc] = (jnp.where(new, v, m), jnp.where(new, k, pos), pay)
        while len(accs) > 1:
            a, b = accs[0], accs[1]
            take = (b[0] > a[0]) | ((b[0] == a[0]) & (b[1] < a[1]))
            accs = accs[2:] + [tuple(jnp.where(take, y, x) for x, y in zip(a, b))]
        m, pos, pay = accs[0]
        emit(j, m, pos if payload_ref is None else pay)
        return pos

    lax.fori_loop(0, k_out, step, jnp.full(shape, n, jnp.int32))


def _pair_list(k):
    return [(i, j) for i in range(k) for j in range(k // (i + 1))]


def _topk_body(s_ref, eidx_ref, gate_ref, stage_s, w_s, va_s, ia_s, vb_s, ib_s, cand_s, cid_s, ts_s):
    tiles = ROUTE_TOKENS // LANES
    for c, (v_s, i_s) in enumerate(((va_s, ia_s), (vb_s, ib_s))):
        for lt in range(tiles):
            stage_s[lt * PEER_N_KEYS:(lt + 1) * PEER_N_KEYS, :] = s_ref[c, :, lt * LANES:(lt + 1) * LANES]
        for k in range(PEER_N_KEYS):
            w_s[k] = stage_s[pl.ds(k, tiles, stride=PEER_N_KEYS), :]

        def emit(j, m, pos, v_s=v_s, i_s=i_s):
            v_s[j] = m
            i_s[j] = pos

        _stream_topk(w_s, PEER_N_KEYS, PEER_TOPK, emit)
    pairs = _pair_list(PEER_TOPK)
    for p, (i, j) in enumerate(pairs):
        cand_s[p] = va_s[i] + vb_s[j]
        cid_s[p] = ia_s[i] * PEER_N_KEYS + ib_s[j]

    def emit_final(j, m, pay):
        ts_s[j] = m
        eidx_ref[0, j] = pay

    _stream_topk(cand_s, len(pairs), PEER_TOPK, emit_final, payload_ref=cid_s)
    ex = [jnp.exp(ts_s[j] - ts_s[0]) for j in range(PEER_TOPK)]
    tot = ex[0]
    for e in ex[1:]:
        tot = tot + e
    for j in range(PEER_TOPK):
        gate_ref[0, j] = ex[j] / tot


def _route_topk(scores):
    n = scores.shape[2]
    assert n % ROUTE_TOKENS == 0 and ROUTE_TOKENS // LANES == SUBLANES
    nb = n // ROUTE_TOKENS
    n_pairs = len(_pair_list(PEER_TOPK))
    tile = lambda rows, dt: pltpu.VMEM((rows, SUBLANES, LANES), dt)
    sel = pl.BlockSpec((1, PEER_TOPK, SUBLANES, LANES), lambda i, h: (i, h, 0, 0))
    return pl.pallas_call(
        _topk_body,
        grid=(nb, PEER_HEADS),
        in_specs=[pl.BlockSpec((2, PEER_N_KEYS, ROUTE_TOKENS), lambda i, h: (h, 0, i))],
        out_specs=[sel, sel],
        out_shape=[jax.ShapeDtypeStruct((nb, PEER_SEL, SUBLANES, LANES), jnp.int32),
                   jax.ShapeDtypeStruct((nb, PEER_SEL, SUBLANES, LANES), F32)],
        scratch_shapes=[pltpu.VMEM((SUBLANES * PEER_N_KEYS, LANES), F32), tile(PEER_N_KEYS, F32),
                        tile(PEER_TOPK, F32), tile(PEER_TOPK, jnp.int32),
                        tile(PEER_TOPK, F32), tile(PEER_TOPK, jnp.int32),
                        tile(n_pairs, F32), tile(n_pairs, jnp.int32), tile(PEER_TOPK, F32)],
        compiler_params=_params("parallel", "parallel"),
    )(scores)


def _pack_bf16_pair(hi, lo):
    bits = lambda a: lax.bitcast_convert_type(a.astype(BF16), jnp.uint16).astype(jnp.uint32)
    return (bits(hi) << 16) | bits(lo)


def _packed_hi(word):
    return lax.bitcast_convert_type(word & jnp.uint32(0xFFFF0000), F32)


def _packed_lo(word):
    return lax.bitcast_convert_type(word << 16, F32)


def _experts_body(eidx_ref, gate_ref, h2_ref, x1_ref, gf_ref, uv_ref, y_ref,
                  idx_s, buf0, buf1, buf2, idx_sem, row_sem, *, tb):
    bufs = (buf0, buf1, buf2)
    idx_cp = pltpu.make_async_copy(eidx_ref, idx_s, idx_sem)
    idx_cp.start()
    idx_cp.wait()

    def issue_group(g, slot_set):
        for j in range(GATHER_GROUP):
            t = g * GATHER_GROUP + j
            for r in range(PEER_SEL):
                pltpu.make_async_copy(uv_ref.at[idx_s[t, r]],
                                      bufs[slot_set].at[j, pl.ds(r * SUBLANES, SUBLANES), :],
                                      row_sem.at[slot_set, j]).start(priority=r % 2)

    def wait(slot_set, j):
        dst = bufs[slot_set].at[j]
        pltpu.make_async_copy(dst, dst, row_sem.at[slot_set, j]).wait()

    gf = gf_ref[...]
    lane_t = lax.broadcasted_iota(jnp.int32, (PEER_SEL, tb), 1)
    n_groups = tb // GATHER_GROUP

    def group(g, slot_set, prefetch):
        for j in range(GATHER_GROUP):
            wait(slot_set, j)
        if prefetch:
            issue_group(g + GATHER_SETS - 1, (slot_set + GATHER_SETS - 1) % GATHER_SETS)
        buf = bufs[slot_set]
        for j in range(GATHER_GROUP):
            t = g * GATHER_GROUP + j
            xrow = h2_ref[pl.ds(t, 1), :]
            tiles = [buf[j, pl.ds(s, PEER_SEL, stride=SUBLANES), :] for s in range(SUBLANES)]
            p = _packed_hi(tiles[0]) * xrow[:, :LANES]
            for s in range(1, SUBLANES):
                p = p + _packed_hi(tiles[s]) * xrow[:, s * LANES:(s + 1) * LANES]
            hcol = jnp.sum(p, axis=1, keepdims=True)
            act = 0.5 * hcol * (1.0 + lax.erf(hcol * (2.0 ** -0.5)))
            gcol = jnp.sum(jnp.where(lane_t == t, gate_ref[0], 0.0), axis=1, keepdims=True)
            w = gcol * act
            o = [jnp.sum(w * _packed_lo(tiles[s]), axis=0, keepdims=True) for s in range(SUBLANES)]
            orow = jnp.concatenate(o, axis=1)
            y_ref[pl.ds(t, 1), :] = _rms(x1_ref[pl.ds(t, 1), :] + orow, gf)

    for s in range(GATHER_SETS - 1):
        issue_group(s, s)
    n_main = (n_groups - (GATHER_SETS - 1)) // GATHER_SETS

    def main(k, carry):
        for s in range(GATHER_SETS):
            group(k * GATHER_SETS + s, s, True)
        return carry

    lax.fori_loop(0, n_main, main, 0)
    for g in range(n_main * GATHER_SETS, n_groups):
        group(g, g % GATHER_SETS, g + GATHER_SETS - 1 < n_groups)


def _experts(eidx, gate, h2, x1, gf, uv):
    n = h2.shape[0]
    tb = TOKEN_BLOCK_EXPERT
    tbr = gate.shape[2]
    assert GATHER_SETS == 3 and tbr % tb == 0 and tb % GATHER_GROUP == 0 and tb // GATHER_GROUP >= GATHER_SETS
    per = tbr // tb
    row = lambda w: pl.BlockSpec((tb, w), lambda i: (i, 0))
    sel = pl.BlockSpec((1, PEER_SEL, tb), lambda i: (i // per, 0, i % per))
    slots = pltpu.VMEM((GATHER_GROUP, PEER_SEL * SUBLANES, LANES), jnp.uint32)
    return pl.pallas_call(
        functools.partial(_experts_body, tb=tb),
        grid=(n // tb,),
        in_specs=[row(PEER_SEL), sel, row(D_MODEL), row(D_MODEL), pl.BlockSpec(gf.shape, lambda i: (0, 0)),
                  pl.BlockSpec(memory_space=pl.ANY)],
        out_specs=row(D_MODEL),
        out_shape=jax.ShapeDtypeStruct((n, D_MODEL), F32),
        scratch_shapes=[pltpu.SMEM((tb, PEER_SEL), jnp.int32), slots, slots, slots,
                        pltpu.SemaphoreType.DMA(()),
                        pltpu.SemaphoreType.DMA((GATHER_SETS, GATHER_GROUP))],
        compiler_params=_params("arbitrary"),
    )(eidx, gate, h2, x1, gf, uv)


def _mixer(x, batch, params, layer, cache=None):
    q, k, v, logf, hgrp = _inproj(x, params["g1"], params["wqkv"], params["wfg"], params["bfg"], params["wh"])
    length = x.shape[0] // batch
    if cache is None:
        k_all = k.reshape(batch, length, ATT_KV_WIDTH)
        v_all = v.reshape(batch, length, ATT_KV_WIDTH)
        logf_all, q_off = logf.reshape(batch, length, ATT_HEADS), 0
        s0t = jnp.zeros((batch, HG_HEADS, HG_DIM, HG_DIM), F32)
    else:
        cache_k, cache_v, cache_logf, state = cache
        past = cache_k.shape[1]
        cat = lambda c, new: jnp.concatenate(
            [c.reshape(batch, past, -1).astype(F32), new.reshape(batch, length, -1)], axis=1)
        k_all, v_all = cat(cache_k, k), cat(cache_v, v)
        logf_all, q_off = cat(cache_logf, logf), past
        s0t = jnp.swapaxes(state.astype(F32), -1, -2)
    f_all = _cumsum(logf_all)
    f_col = f_all[:, q_off:].reshape(batch * length, ATT_HEADS)
    f_row = jnp.swapaxes(f_all, 1, 2)
    att = _attn(q, k_all, v_all, f_col, f_row, batch=batch, q_off=q_off)
    hgo, st = _hgrn(hgrp, params["lb_logits"], params["hg_gain"], s0t, batch=batch, layer=layer)
    return att, hgo, (k, v, logf, jnp.swapaxes(st, -1, -2))


def kernel(x_prompt, x_sample, cache_k, cache_v, cache_logf, state_hgrn, meta_tokens,
           w_in, b_forget, hg_lb_logits, hg_norm_gain, w_out, norm1_gain, norm2_gain,
           peer_w_query, peer_sub_keys, peer_expert_u, peer_expert_v, final_norm_gain):
    depth = w_in.shape[0]
    assert depth == 1, "single-layer trunk: the PEER stage fuses the final norm"
    assert D_MODEL == SUBLANES * LANES, "an expert's packed u|v row must fill exactly one (SUBLANES, LANES) slab"
    bp, seq, _ = x_prompt.shape
    bs, dseq, _ = x_sample.shape
    lp = N_META + seq
    meta = jnp.broadcast_to(meta_tokens.astype(x_prompt.dtype)[None], (bp, N_META, D_MODEL))
    xp = jnp.concatenate([meta, x_prompt], axis=1).reshape(bp * lp, D_MODEL)
    xs = x_sample.reshape(bs * dseq, D_MODEL)

    l = 0
    o_fg = ATT_WIDTH + 2 * ATT_KV_WIDTH
    o_h = o_fg + ATT_HEADS
    row = lambda a: a.reshape(1, -1).astype(F32)
    params = {
        "g1": row(norm1_gain[l]),
        "wqkv": w_in[l][:, :o_fg].astype(BF16),
        "wfg": jnp.pad(w_in[l][:, o_fg:o_h], ((0, 0), (0, LANES - ATT_HEADS))).astype(BF16),
        "bfg": row(b_forget[l]),
        "wh": w_in[l][:, o_h:].astype(BF16),
        "lb_logits": hg_lb_logits.astype(F32),
        "hg_gain": row(hg_norm_gain[l]),
    }
    att_p, hgo_p, sp = _mixer(xp, bp, params, l)
    att_s, hgo_s, ss = _mixer(xs, bs, params, l,
                              cache=(cache_k[l], cache_v[l], cache_logf[l], state_hgrn[l]))

    x_all = jnp.concatenate([xp, xs], axis=0)
    att = jnp.concatenate([att_p, att_s], axis=0)
    hgo = jnp.concatenate([hgo_p, hgo_s], axis=0)
    wo = w_out[l].astype(BF16)
    x1, h2, scores = _proj(x_all, att, hgo, wo[:ATT_WIDTH], wo[ATT_WIDTH:], row(norm2_gain[l]),
                           peer_w_query[l].astype(BF16), peer_sub_keys[l].astype(BF16))
    eidx, gate = (a.reshape(-1, PEER_SEL, ROUTE_TOKENS) for a in _route_topk(scores))
    n_exp = peer_expert_u.shape[1]
    uv = _pack_bf16_pair(peer_expert_u[l], peer_expert_v[l]).reshape(n_exp, SUBLANES, LANES)
    eidx_t = jnp.swapaxes(eidx, 1, 2).reshape(-1, PEER_SEL)
    y = _experts(eidx_t, gate, h2, x1, row(final_norm_gain), uv)

    n_p = bp * lp
    y_prompt = y[:n_p].reshape(bp, lp, D_MODEL)[:, N_META:]
    y_sample = y[n_p:].reshape(bs, dseq, D_MODEL)

    def states(s, batch, length):
        k, v, logf, st = s
        return (k.reshape(1, batch, length, ATT_KV_HEADS, ATT_HEAD_DIM),
                v.reshape(1, batch, length, ATT_KV_HEADS, ATT_HEAD_DIM),
                logf.reshape(1, batch, length, ATT_HEADS),
                st.reshape(1, batch, HG_HEADS, HG_DIM, HG_DIM))

    return (y_prompt, y_sample) + states(sp, bp, lp) + states(ss, bs, dseq)
```

```python
import functools

import jax
import jax.numpy as jnp
from jax import lax
from jax.experimental import pallas as pl
from jax.experimental.pallas import tpu as pltpu

F32 = jnp.float32
BF16 = jnp.bfloat16
EPS = 1e-6
N_META = 16

D_MODEL = 1024
ATT_HEADS = 8
ATT_KV_HEADS = 4
ATT_HEAD_DIM = 64
ATT_GROUP = ATT_HEADS // ATT_KV_HEADS
ATT_WIDTH = ATT_HEADS * ATT_HEAD_DIM
ATT_KV_WIDTH = ATT_KV_HEADS * ATT_HEAD_DIM
HG_HEADS = 4
HG_DIM = 128
HG_WIDTH = HG_HEADS * HG_DIM
HG_PER_STEP = 2
PEER_HEADS = 8
PEER_N_KEYS = 128
PEER_TOPK = 16
PEER_HALF = 128
PEER_SEL = PEER_HEADS * PEER_TOPK

LANES = 128
SUBLANES = 8
VMEM_LIMIT = 48 * 1024 * 1024

TOKEN_BLOCK_PROJ = 512
ROUTE_TOKENS = SUBLANES * LANES
TOKEN_BLOCK_EXPERT = 256
GATHER_GROUP = 8
GATHER_SETS = 3


def _params(*sem):
    return pltpu.CompilerParams(dimension_semantics=sem, vmem_limit_bytes=VMEM_LIMIT)


def _largest_block(n, cap, mult):
    best = None
    for d in range(mult, cap + 1, mult):
        if n % d == 0:
            best = d
    assert best is not None, (n, cap, mult)
    return best


def _split3(x):
    hi = x.astype(BF16)
    r1 = x - hi.astype(F32)
    mid = r1.astype(BF16)
    lo = (r1 - mid.astype(F32)).astype(BF16)
    return hi, mid, lo


def _tri_cumsum(tri, x):
    hi, mid, lo = _split3(x)
    d = lambda a: jnp.dot(tri, a, preferred_element_type=F32)
    return (d(lo) + d(mid)) + d(hi)


def _lower_tri(c):
    r = lax.broadcasted_iota(jnp.int32, (c, c), 0)
    s = lax.broadcasted_iota(jnp.int32, (c, c), 1)
    return jnp.where(s <= r, 1.0, 0.0).astype(BF16)


def _rms(x, gain):
    return x * lax.rsqrt(jnp.mean(x * x, axis=-1, keepdims=True) + EPS) * gain


def _inproj_body(x_ref, g_ref, wqkv_ref, wfg_ref, bfg_ref, wh_ref,
                 q_ref, k_ref, v_ref, logf_ref, hgrp_ref):
    hb = _rms(x_ref[...], g_ref[...]).astype(BF16)
    qkv = jnp.dot(hb, wqkv_ref[...], preferred_element_type=F32)
    q_ref[...] = (qkv[:, :ATT_WIDTH] * (ATT_HEAD_DIM ** -0.5)).astype(BF16)
    k_ref[...] = qkv[:, ATT_WIDTH:ATT_WIDTH + ATT_KV_WIDTH]
    v_ref[...] = qkv[:, ATT_WIDTH + ATT_KV_WIDTH:]
    fg = jnp.dot(hb, wfg_ref[...], preferred_element_type=F32)[:, :ATT_HEADS] + bfg_ref[...]
    logf_ref[...] = jnp.minimum(fg, 0.0) - jnp.log1p(jnp.exp(-jnp.abs(fg)))
    hgrp_ref[...] = jnp.dot(hb, wh_ref[...], preferred_element_type=F32)


def _inproj(x, gain, wqkv, wfg, bfg, wh):
    n = x.shape[0]
    tb = _largest_block(n, TOKEN_BLOCK_PROJ, 8)
    row = lambda w: pl.BlockSpec((tb, w), lambda i: (i, 0))
    full = lambda a: pl.BlockSpec(a.shape, lambda i: (0,) * a.ndim)
    return pl.pallas_call(
        _inproj_body,
        grid=(n // tb,),
        in_specs=[row(D_MODEL), full(gain), full(wqkv), full(wfg), full(bfg), full(wh)],
        out_specs=[row(ATT_WIDTH), row(ATT_KV_WIDTH), row(ATT_KV_WIDTH), row(ATT_HEADS), row(4 * HG_WIDTH)],
        out_shape=[jax.ShapeDtypeStruct((n, ATT_WIDTH), BF16),
                   jax.ShapeDtypeStruct((n, ATT_KV_WIDTH), F32),
                   jax.ShapeDtypeStruct((n, ATT_KV_WIDTH), F32),
                   jax.ShapeDtypeStruct((n, ATT_HEADS), F32),
                   jax.ShapeDtypeStruct((n, 4 * HG_WIDTH), F32)],
        compiler_params=_params("parallel"),
    )(x, gain, wqkv, wfg, bfg, wh)


def _cumsum_body(x_ref, o_ref, *, cb):
    length, width = x_ref.shape[1], x_ref.shape[2]
    tri = _lower_tri(cb)
    carry = jnp.zeros((1, width), F32)
    for j in range(length // cb):
        f = _tri_cumsum(tri, x_ref[0, j * cb:(j + 1) * cb, :]) + carry
        o_ref[0, j * cb:(j + 1) * cb, :] = f
        carry = f[cb - 1:cb, :]


def _cumsum(logf):
    b, length, h = logf.shape
    cb = _largest_block(length, 512, 8)
    spec = pl.BlockSpec((1, length, h), lambda i: (i, 0, 0))
    return pl.pallas_call(
        functools.partial(_cumsum_body, cb=cb),
        grid=(b,),
        in_specs=[spec],
        out_specs=spec,
        out_shape=jax.ShapeDtypeStruct(logf.shape, F32),
        compiler_params=_params("parallel"),
    )(logf)


def _attn_body(q_ref, k_ref, v_ref, fc_ref, fr_ref, o_ref, *, q_start):
    tq, kmax = q_ref.shape[0], k_ref.shape[1]
    qpos = q_start + lax.broadcasted_iota(jnp.int32, (tq, kmax), 0)
    kpos = lax.broadcasted_iota(jnp.int32, (tq, kmax), 1)
    mask = kpos <= qpos
    kb = k_ref[0].astype(BF16)
    vb = v_ref[0].astype(BF16)
    q = q_ref[...]
    fc = fc_ref[...]
    fr = fr_ref[0]
    outs = []
    for h in range(ATT_HEADS):
        g = h // ATT_GROUP
        hs = slice(h * ATT_HEAD_DIM, (h + 1) * ATT_HEAD_DIM)
        gs = slice(g * ATT_HEAD_DIM, (g + 1) * ATT_HEAD_DIM)
        s = lax.dot_general(q[:, hs], kb[:, gs], (((1,), (1,)), ((), ())), preferred_element_type=F32)
        s = s + fc[:, h:h + 1] - fr[h:h + 1, :]
        s = jnp.where(mask, s, -jnp.inf)
        p = jnp.exp(s - jnp.max(s, axis=-1, keepdims=True))
        l = jnp.sum(p, axis=-1, keepdims=True)
        o = jnp.dot(p.astype(BF16), vb[:, gs], preferred_element_type=F32)
        outs.append(o / l)
    o_ref[0] = jnp.concatenate(outs, axis=-1).astype(BF16)


def _attn(q, k, v, f_col, f_row, *, batch, q_off):
    t_q = q.shape[0] // batch
    lk = k.shape[1]
    tq = _largest_block(t_q, 384, 8)
    nq = t_q // tq
    outs = []
    for c in range(nq):
        q_start = q_off + c * tq
        kmax = min(lk, -(-(q_start + tq) // LANES) * LANES)
        qspec = lambda w, c=c: pl.BlockSpec((tq, w), lambda b: (b * nq + c, 0))
        kspec = pl.BlockSpec((1, kmax, ATT_KV_WIDTH), lambda b: (b, 0, 0))
        outs.append(pl.pallas_call(
            functools.partial(_attn_body, q_start=q_start),
            grid=(batch,),
            in_specs=[qspec(ATT_WIDTH), kspec, kspec, qspec(ATT_HEADS),
                      pl.BlockSpec((1, ATT_HEADS, kmax), lambda b: (b, 0, 0))],
            out_specs=pl.BlockSpec((1, tq, ATT_WIDTH), lambda b: (b, 0, 0)),
            out_shape=jax.ShapeDtypeStruct((batch, tq, ATT_WIDTH), BF16),
            compiler_params=_params("parallel"),
        )(q, k, v, f_col, f_row))
    return jnp.concatenate(outs, axis=1).reshape(batch * t_q, ATT_WIDTH)


def _hgrn_body(hq_ref, hf_ref, hi_ref, hg_ref, lbl_ref, gain_ref, s0_ref, o_ref, st_ref,
               oi_s, *, chunk, layer):
    length = hq_ref.shape[0]
    lbl = lbl_ref[...]
    e = jnp.exp(lbl - jnp.max(lbl, axis=0, keepdims=True))
    lb_all = jnp.sum(e[:layer + 1, :], axis=0, keepdims=True) / jnp.sum(e, axis=0, keepdims=True)
    gain_all = gain_ref[...]
    tri = _lower_tri(chunk)
    rows = lax.broadcasted_iota(jnp.int32, (chunk, 1), 0)
    st_ref[...] = s0_ref[...]

    def chunk_step(c, carry):
        r0 = pl.multiple_of(c * chunk, 16)
        for hh in range(HG_PER_STEP):
            ls = slice(hh * HG_DIM, (hh + 1) * HG_DIM)
            lb = lb_all[:, ls]
            st = st_ref[0, hh]
            f = lb + (1.0 - lb) * jax.nn.sigmoid(hf_ref[pl.ds(r0, chunk), ls])
            kk = 1.0 - f
            hq = hq_ref[pl.ds(r0, chunk), ls]
            qq = hq * jax.nn.sigmoid(hq)
            iv = hi_ref[pl.ds(r0, chunk), ls]
            g = _tri_cumsum(tri, jnp.log(f))
            o_inter = lax.dot_general((qq * jnp.exp(g)).astype(BF16), st.astype(BF16),
                                      (((1,), (1,)), ((), ())), preferred_element_type=F32)
            for t in range(chunk):
                n = (t // SUBLANES + 1) * SUBLANES
                d = jnp.where(rows[:n] <= t, g[t:t + 1, :] - g[:n, :], -jnp.inf)
                a = jnp.exp(d) * kk[:n, :] * qq[t:t + 1, :]
                sc = jnp.sum(a, axis=1, keepdims=True)
                oi_s[hh, t:t + 1, :] = jnp.sum(sc * iv[:n, :], axis=0, keepdims=True)
            o = o_inter + oi_s[hh]
            on = _rms(o, gain_all[:, ls])
            hg = hg_ref[pl.ds(r0, chunk), ls]
            o_ref[pl.ds(r0, chunk), ls] = (on * (hg * jax.nn.sigmoid(hg))).astype(BF16)
            g_end = g[chunk - 1:chunk, :]
            kd = kk * jnp.exp(g_end - g)
            upd = lax.dot_general(iv.astype(BF16), kd.astype(BF16),
                                  (((0,), (0,)), ((), ())), preferred_element_type=F32)
            st_ref[0, hh] = st * jnp.exp(g_end) + upd
        return carry

    lax.fori_loop(0, length // chunk, chunk_step, 0)


def _hgrn(hgrp, lb_logits, gain, s0t, *, batch, layer):
    length = hgrp.shape[0] // batch
    chunk = _largest_block(length, 64, 16)
    width = HG_PER_STEP * HG_DIM
    steps = HG_HEADS // HG_PER_STEP
    col = lambda off: pl.BlockSpec((length, width), lambda b, h: (b, off * steps + h))
    par = lambda a: pl.BlockSpec((a.shape[0], width), lambda b, h: (0, h))
    st = pl.BlockSpec((1, HG_PER_STEP, HG_DIM, HG_DIM), lambda b, h: (b, h, 0, 0))
    buf = pltpu.VMEM((HG_PER_STEP, chunk, HG_DIM), F32)
    return pl.pallas_call(
        functools.partial(_hgrn_body, chunk=chunk, layer=layer),
        grid=(batch, steps),
        in_specs=[col(0), col(1), col(2), col(3), par(lb_logits), par(gain), st],
        out_specs=[pl.BlockSpec((length, width), lambda b, h: (b, h)), st],
        out_shape=[jax.ShapeDtypeStruct((batch * length, HG_WIDTH), BF16),
                   jax.ShapeDtypeStruct(s0t.shape, F32)],
        scratch_shapes=[buf],
        compiler_params=_params("parallel", "parallel"),
    )(hgrp, hgrp, hgrp, hgrp, lb_logits, gain, s0t)


def _proj_body(x_ref, att_ref, hgo_ref, woa_ref, wob_ref, g2_ref, wq_ref, keys_ref, x1_ref, h2_ref, s_ref):
    x1 = (x_ref[...] + jnp.dot(att_ref[...], woa_ref[...], preferred_element_type=F32)
          + jnp.dot(hgo_ref[...], wob_ref[...], preferred_element_type=F32))
    x1_ref[...] = x1
    h2 = _rms(x1, g2_ref[...])
    h2_ref[...] = h2
    qp = jnp.dot(h2.astype(BF16), wq_ref[...], preferred_element_type=F32)
    for h in range(PEER_HEADS):
        for c in range(2):
            col = (2 * h + c) * PEER_HALF
            s_ref[2 * h + c] = lax.dot_general(keys_ref[h, c], qp[:, col:col + PEER_HALF].astype(BF16),
                                               (((1,), (1,)), ((), ())), preferred_element_type=F32)


def _proj(x, att, hgo, woa, wob, g2, wq, keys):
    n = x.shape[0]
    tb = _largest_block(n, TOKEN_BLOCK_PROJ, LANES)
    row = lambda w: pl.BlockSpec((tb, w), lambda i: (i, 0))
    full = lambda a: pl.BlockSpec(a.shape, lambda i: (0,) * a.ndim)
    return pl.pallas_call(
        _proj_body,
        grid=(n // tb,),
        in_specs=[row(D_MODEL), row(ATT_WIDTH), row(HG_WIDTH), full(woa), full(wob), full(g2), full(wq), full(keys)],
        out_specs=[row(D_MODEL), row(D_MODEL), pl.BlockSpec((2 * PEER_HEADS, PEER_N_KEYS, tb), lambda i: (0, 0, i))],
        out_shape=[jax.ShapeDtypeStruct((n, D_MODEL), F32),
                   jax.ShapeDtypeStruct((n, D_MODEL), F32),
                   jax.ShapeDtypeStruct((2 * PEER_HEADS, PEER_N_KEYS, n), F32)],
        compiler_params=_params("parallel"),
    )(x, att, hgo, woa, wob, g2, wq, keys)


def _stream_topk(w_ref, n, k_out, emit, payload_ref=None):
    n_acc = 4
    shape = w_ref.shape[1:]

    def step(j, prev):
        accs = [(jnp.full(shape, -jnp.inf, F32), jnp.full(shape, n, jnp.int32), jnp.full(shape, -1, jnp.int32))
                for _ in range(n_acc)]
        for k in range(n):
            v = jnp.where(prev == k, -jnp.inf, w_ref[k])
            w_ref[k] = v
            m, pos, pay = accs[k % n_acc]
            new = v > m
            pay = pay if payload_ref is None else jnp.where(new, payload_ref[k], pay)
            accs[k % n_acc] = (jnp.where(new, v, m), jnp.where(new, k, pos), pay)
        while len(accs) > 1:
            a, b = accs[0], accs[1]
            take = (b[0] > a[0]) | ((b[0] == a[0]) & (b[1] < a[1]))
            accs = accs[2:] + [tuple(jnp.where(take, y, x) for x, y in zip(a, b))]
        m, pos, pay = accs[0]
        emit(j, m, pos if payload_ref is None else pay)
        return pos

    lax.fori_loop(0, k_out, step, jnp.full(shape, n, jnp.int32))


def _pair_list(k):
    return [(i, j) for i in range(k) for j in range(k // (i + 1))]


def _topk_body(s_ref, eidx_ref, gate_ref, stage_s, w_s, va_s, ia_s, vb_s, ib_s, cand_s, cid_s, ts_s):
    tiles = ROUTE_TOKENS // LANES
    for c, (v_s, i_s) in enumerate(((va_s, ia_s), (vb_s, ib_s))):
        for lt in range(tiles):
            stage_s[lt * PEER_N_KEYS:(lt + 1) * PEER_N_KEYS, :] = s_ref[c, :, lt * LANES:(lt + 1) * LANES]
        for k in range(PEER_N_KEYS):
            w_s[k] = stage_s[pl.ds(k, tiles, stride=PEER_N_KEYS), :]

        def emit(j, m, pos, v_s=v_s, i_s=i_s):
            v_s[j] = m
            i_s[j] = pos

        _stream_topk(w_s, PEER_N_KEYS, PEER_TOPK, emit)
    pairs = _pair_list(PEER_TOPK)
    for p, (i, j) in enumerate(pairs):
        cand_s[p] = va_s[i] + vb_s[j]
        cid_s[p] = ia_s[i] * PEER_N_KEYS + ib_s[j]

    def emit_final(j, m, pay):
        ts_s[j] = m
        eidx_ref[0, j] = pay

    _stream_topk(cand_s, len(pairs), PEER_TOPK, emit_final, payload_ref=cid_s)
    ex = [jnp.exp(ts_s[j] - ts_s[0]) for j in range(PEER_TOPK)]
    tot = ex[0]
    for e in ex[1:]:
        tot = tot + e
    for j in range(PEER_TOPK):
        gate_ref[0, j] = ex[j] / tot


def _route_topk(scores):
    n = scores.shape[2]
    assert n % ROUTE_TOKENS == 0 and ROUTE_TOKENS // LANES == SUBLANES
    nb = n // ROUTE_TOKENS
    n_pairs = len(_pair_list(PEER_TOPK))
    tile = lambda rows, dt: pltpu.VMEM((rows, SUBLANES, LANES), dt)
    sel = pl.BlockSpec((1, PEER_TOPK, SUBLANES, LANES), lambda i, h: (i, h, 0, 0))
    return pl.pallas_call(
        _topk_body,
        grid=(nb, PEER_HEADS),
        in_specs=[pl.BlockSpec((2, PEER_N_KEYS, ROUTE_TOKENS), lambda i, h: (h, 0, i))],
        out_specs=[sel, sel],
        out_shape=[jax.ShapeDtypeStruct((nb, PEER_SEL, SUBLANES, LANES), jnp.int32),
                   jax.ShapeDtypeStruct((nb, PEER_SEL, SUBLANES, LANES), F32)],
        scratch_shapes=[pltpu.VMEM((SUBLANES * PEER_N_KEYS, LANES), F32), tile(PEER_N_KEYS, F32),
                        tile(PEER_TOPK, F32), tile(PEER_TOPK, jnp.int32),
                        tile(PEER_TOPK, F32), tile(PEER_TOPK, jnp.int32),
                        tile(n_pairs, F32), tile(n_pairs, jnp.int32), tile(PEER_TOPK, F32)],
        compiler_params=_params("parallel", "parallel"),
    )(scores)


def _pack_bf16_pair(hi, lo):
    bits = lambda a: lax.bitcast_convert_type(a.astype(BF16), jnp.uint16).astype(jnp.uint32)
    return (bits(hi) << 16) | bits(lo)


def _packed_hi(word):
    return lax.bitcast_convert_type(word & jnp.uint32(0xFFFF0000), F32)


def _packed_lo(word):
    return lax.bitcast_convert_type(word << 16, F32)


def _experts_body(eidx_ref, gate_ref, h2_ref, x1_ref, gf_ref, uv_ref, y_ref,
                  idx_s, buf0, buf1, buf2, idx_sem, row_sem, *, tb):
    bufs = (buf0, buf1, buf2)
    idx_cp = pltpu.make_async_copy(eidx_ref, idx_s, idx_sem)
    idx_cp.start()
    idx_cp.wait()

    def issue_group(g, slot_set):
        for j in range(GATHER_GROUP):
            t = g * GATHER_GROUP + j
            for r in range(PEER_SEL):
                pltpu.make_async_copy(uv_ref.at[idx_s[t, r]],
                                      bufs[slot_set].at[j, pl.ds(r * SUBLANES, SUBLANES), :],
                                      row_sem.at[slot_set, j]).start(priority=r % 2)

    def wait(slot_set, j):
        dst = bufs[slot_set].at[j]
        pltpu.make_async_copy(dst, dst, row_sem.at[slot_set, j]).wait()

    gf = gf_ref[...]
    lane_t = lax.broadcasted_iota(jnp.int32, (PEER_SEL, tb), 1)
    n_groups = tb // GATHER_GROUP

    def group(g, slot_set, prefetch):
        for j in range(GATHER_GROUP):
            wait(slot_set, j)
        if prefetch:
            issue_group(g + GATHER_SETS - 1, (slot_set + GATHER_SETS - 1) % GATHER_SETS)
        buf = bufs[slot_set]
        for j in range(GATHER_GROUP):
            t = g * GATHER_GROUP + j
            xrow = h2_ref[pl.ds(t, 1), :]
            tiles = [buf[j, pl.ds(s, PEER_SEL, stride=SUBLANES), :] for s in range(SUBLANES)]
            p = _packed_hi(tiles[0]) * xrow[:, :LANES]
            for s in range(1, SUBLANES):
                p = p + _packed_hi(tiles[s]) * xrow[:, s * LANES:(s + 1) * LANES]
            hcol = jnp.sum(p, axis=1, keepdims=True)
            act = 0.5 * hcol * (1.0 + lax.erf(hcol * (2.0 ** -0.5)))
            gcol = jnp.sum(jnp.where(lane_t == t, gate_ref[0], 0.0), axis=1, keepdims=True)
            w = gcol * act
            o = [jnp.sum(w * _packed_lo(tiles[s]), axis=0, keepdims=True) for s in range(SUBLANES)]
            orow = jnp.concatenate(o, axis=1)
            y_ref[pl.ds(t, 1), :] = _rms(x1_ref[pl.ds(t, 1), :] + orow, gf)

    for s in range(GATHER_SETS - 1):
        issue_group(s, s)
    n_main = (n_groups - (GATHER_SETS - 1)) // GATHER_SETS

    def main(k, carry):
        for s in range(GATHER_SETS):
            group(k * GATHER_SETS + s, s, True)
        return carry

    lax.fori_loop(0, n_main, main, 0)
    for g in range(n_main * GATHER_SETS, n_groups):
        group(g, g % GATHER_SETS, g + GATHER_SETS - 1 < n_groups)


def _experts(eidx, gate, h2, x1, gf, uv):
    n = h2.shape[0]
    tb = TOKEN_BLOCK_EXPERT
    tbr = gate.shape[2]
    assert GATHER_SETS == 3 and tbr % tb == 0 and tb % GATHER_GROUP == 0 and tb // GATHER_GROUP >= GATHER_SETS
    per = tbr // tb
    row = lambda w: pl.BlockSpec((tb, w), lambda i: (i, 0))
    sel = pl.BlockSpec((1, PEER_SEL, tb), lambda i: (i // per, 0, i % per))
    slots = pltpu.VMEM((GATHER_GROUP, PEER_SEL * SUBLANES, LANES), jnp.uint32)
    return pl.pallas_call(
        functools.partial(_experts_body, tb=tb),
        grid=(n // tb,),
        in_specs=[row(PEER_SEL), sel, row(D_MODEL), row(D_MODEL), pl.BlockSpec(gf.shape, lambda i: (0, 0)),
                  pl.BlockSpec(memory_space=pl.ANY)],
        out_specs=row(D_MODEL),
        out_shape=jax.ShapeDtypeStruct((n, D_MODEL), F32),
        scratch_shapes=[pltpu.SMEM((tb, PEER_SEL), jnp.int32), slots, slots, slots,
                        pltpu.SemaphoreType.DMA(()),
                        pltpu.SemaphoreType.DMA((GATHER_SETS, GATHER_GROUP))],
        compiler_params=_params("arbitrary"),
    )(eidx, gate, h2, x1, gf, uv)


def _mixer(x, batch, params, layer, cache=None):
    q, k, v, logf, hgrp = _inproj(x, params["g1"], params["wqkv"], params["wfg"], params["bfg"], params["wh"])
    length = x.shape[0] // batch
    if cache is None:
        k_all = k.reshape(batch, length, ATT_KV_WIDTH)
        v_all = v.reshape(batch, length, ATT_KV_WIDTH)
        logf_all, q_off = logf.reshape(batch, length, ATT_HEADS), 0
        s0t = jnp.zeros((batch, HG_HEADS, HG_DIM, HG_DIM), F32)
    else:
        cache_k, cache_v, cache_logf, state = cache
        past = cache_k.shape[1]
        cat = lambda c, new: jnp.concatenate(
            [c.reshape(batch, past, -1).astype(F32), new.reshape(batch, length, -1)], axis=1)
        k_all, v_all = cat(cache_k, k), cat(cache_v, v)
        logf_all, q_off = cat(cache_logf, logf), past
        s0t = jnp.swapaxes(state.astype(F32), -1, -2)
    f_all = _cumsum(logf_all)
    f_col = f_all[:, q_off:].reshape(batch * length, ATT_HEADS)
    f_row = jnp.swapaxes(f_all, 1, 2)
    att = _attn(q, k_all, v_all, f_col, f_row, batch=batch, q_off=q_off)
    hgo, st = _hgrn(hgrp, params["lb_logits"], params["hg_gain"], s0t, batch=batch, layer=layer)
    return att, hgo, (k, v, logf, jnp.swapaxes(st, -1, -2))


def kernel(x_prompt, x_sample, cache_k, cache_v, cache_logf, state_hgrn, meta_tokens,
           w_in, b_forget, hg_lb_logits, hg_norm_gain, w_out, norm1_gain, norm2_gain,
           peer_w_query, peer_sub_keys, peer_expert_u, peer_expert_v, final_norm_gain):
    depth = w_in.shape[0]
    assert depth == 1, "single-layer trunk: the PEER stage fuses the final norm"
    assert D_MODEL == SUBLANES * LANES, "an expert's packed u|v row must fill exactly one (SUBLANES, LANES) slab"
    bp, seq, _ = x_prompt.shape
    bs, dseq, _ = x_sample.shape
    lp = N_META + seq
    meta = jnp.broadcast_to(meta_tokens.astype(x_prompt.dtype)[None], (bp, N_META, D_MODEL))
    xp = jnp.concatenate([meta, x_prompt], axis=1).reshape(bp * lp, D_MODEL)
    xs = x_sample.reshape(bs * dseq, D_MODEL)

    l = 0
    o_fg = ATT_WIDTH + 2 * ATT_KV_WIDTH
    o_h = o_fg + ATT_HEADS
    row = lambda a: a.reshape(1, -1).astype(F32)
    params = {
        "g1": row(norm1_gain[l]),
        "wqkv": w_in[l][:, :o_fg].astype(BF16),
        "wfg": jnp.pad(w_in[l][:, o_fg:o_h], ((0, 0), (0, LANES - ATT_HEADS))).astype(BF16),
        "bfg": row(b_forget[l]),
        "wh": w_in[l][:, o_h:].astype(BF16),
        "lb_logits": hg_lb_logits.astype(F32),
        "hg_gain": row(hg_norm_gain[l]),
    }
    att_p, hgo_p, sp = _mixer(xp, bp, params, l)
    att_s, hgo_s, ss = _mixer(xs, bs, params, l,
                              cache=(cache_k[l], cache_v[l], cache_logf[l], state_hgrn[l]))

    x_all = jnp.concatenate([xp, xs], axis=0)
    att = jnp.concatenate([att_p, att_s], axis=0)
    hgo = jnp.concatenate([hgo_p, hgo_s], axis=0)
    wo = w_out[l].astype(BF16)
    x1, h2, scores = _proj(x_all, att, hgo, wo[:ATT_WIDTH], wo[ATT_WIDTH:], row(norm2_gain[l]),
                           peer_w_query[l].astype(BF16), peer_sub_keys[l].astype(BF16))
    eidx, gate = (a.reshape(-1, PEER_SEL, ROUTE_TOKENS) for a in _route_topk(scores))
    n_exp = peer_expert_u.shape[1]
    uv = _pack_bf16_pair(peer_expert_u[l], peer_expert_v[l]).reshape(n_exp, SUBLANES, LANES)
    eidx_t = jnp.swapaxes(eidx, 1, 2).reshape(-1, PEER_SEL)
    y = _experts(eidx_t, gate, h2, x1, row(final_norm_gain), uv)

    n_p = bp * lp
    y_prompt = y[:n_p].reshape(bp, lp, D_MODEL)[:, N_META:]
    y_sample = y[n_p:].reshape(bs, dseq, D_MODEL)

    def states(s, batch, length):
        k, v, logf, st = s
        return (k.reshape(1, batch, length, ATT_KV_HEADS, ATT_HEAD_DIM),
                v.reshape(1, batch, length, ATT_KV_HEADS, ATT_HEAD_DIM),
                logf.reshape(1, batch, length, ATT_HEADS),
                st.reshape(1, batch, HG_HEADS, HG_DIM, HG_DIM))

    return (y_prompt, y_sample) + states(sp, bp, lp) + states(ss, bs, dseq)
```

```python
import functools

import jax
import jax.numpy as jnp
from jax import lax
from jax.experimental import pallas as pl
from jax.experimental.pallas import tpu as pltpu

F32 = jnp.float32
BF16 = jnp.bfloat16
EPS = 1e-6
N_META = 16

D_MODEL = 1024
ATT_HEADS = 8
ATT_KV_HEADS = 4
ATT_HEAD_DIM = 64
ATT_GROUP = ATT_HEADS // ATT_KV_HEADS
ATT_WIDTH = ATT_HEADS * ATT_HEAD_DIM
ATT_KV_WIDTH = ATT_KV_HEADS * ATT_HEAD_DIM
HG_HEADS = 4
HG_DIM = 128
HG_WIDTH = HG_HEADS * HG_DIM
HG_PER_STEP = 2
PEER_HEADS = 8
PEER_N_KEYS = 128
PEER_TOPK = 16
PEER_HALF = 128
PEER_SEL = PEER_HEADS * PEER_TOPK

LANES = 128
SUBLANES = 8
VMEM_LIMIT = 48 * 1024 * 1024

TOKEN_BLOCK_PROJ = 512
ROUTE_TOKENS = SUBLANES * LANES
TOKEN_BLOCK_EXPERT = 256
GATHER_GROUP = 8
GATHER_SETS = 4


def _params(*sem):
    return pltpu.CompilerParams(dimension_semantics=sem, vmem_limit_bytes=VMEM_LIMIT)


def _largest_block(n, cap, mult):
    best = None
    for d in range(mult, cap + 1, mult):
        if n % d == 0:
            best = d
    assert best is not None, (n, cap, mult)
    return best


def _split3(x):
    hi = x.astype(BF16)
    r1 = x - hi.astype(F32)
    mid = r1.astype(BF16)
    lo = (r1 - mid.astype(F32)).astype(BF16)
    return hi, mid, lo


def _tri_cumsum(tri, x):
    hi, mid, lo = _split3(x)
    d = lambda a: jnp.dot(tri, a, preferred_element_type=F32)
    return (d(lo) + d(mid)) + d(hi)


def _lower_tri(c):
    r = lax.broadcasted_iota(jnp.int32, (c, c), 0)
    s = lax.broadcasted_iota(jnp.int32, (c, c), 1)
    return jnp.where(s <= r, 1.0, 0.0).astype(BF16)


def _rms(x, gain):
    return x * lax.rsqrt(jnp.mean(x * x, axis=-1, keepdims=True) + EPS) * gain


def _inproj_body(x_ref, g_ref, wqkv_ref, wfg_ref, bfg_ref, wh_ref,
                 q_ref, k_ref, v_ref, logf_ref, hgrp_ref):
    hb = _rms(x_ref[...], g_ref[...]).astype(BF16)
    qkv = jnp.dot(hb, wqkv_ref[...], preferred_element_type=F32)
    q_ref[...] = (qkv[:, :ATT_WIDTH] * (ATT_HEAD_DIM ** -0.5)).astype(BF16)
    k_ref[...] = qkv[:, ATT_WIDTH:ATT_WIDTH + ATT_KV_WIDTH]
    v_ref[...] = qkv[:, ATT_WIDTH + ATT_KV_WIDTH:]
    fg = jnp.dot(hb, wfg_ref[...], preferred_element_type=F32)[:, :ATT_HEADS] + bfg_ref[...]
    logf_ref[...] = jnp.minimum(fg, 0.0) - jnp.log1p(jnp.exp(-jnp.abs(fg)))
    hgrp_ref[...] = jnp.dot(hb, wh_ref[...], preferred_element_type=F32)


def _inproj(x, gain, wqkv, wfg, bfg, wh):
    n = x.shape[0]
    tb = _largest_block(n, TOKEN_BLOCK_PROJ, 8)
    row = lambda w: pl.BlockSpec((tb, w), lambda i: (i, 0))
    full = lambda a: pl.BlockSpec(a.shape, lambda i: (0,) * a.ndim)
    return pl.pallas_call(
        _inproj_body,
        grid=(n // tb,),
        in_specs=[row(D_MODEL), full(gain), full(wqkv), full(wfg), full(bfg), full(wh)],
        out_specs=[row(ATT_WIDTH), row(ATT_KV_WIDTH), row(ATT_KV_WIDTH), row(ATT_HEADS), row(4 * HG_WIDTH)],
        out_shape=[jax.ShapeDtypeStruct((n, ATT_WIDTH), BF16),
                   jax.ShapeDtypeStruct((n, ATT_KV_WIDTH), F32),
                   jax.ShapeDtypeStruct((n, ATT_KV_WIDTH), F32),
                   jax.ShapeDtypeStruct((n, ATT_HEADS), F32),
                   jax.ShapeDtypeStruct((n, 4 * HG_WIDTH), F32)],
        compiler_params=_params("parallel"),
    )(x, gain, wqkv, wfg, bfg, wh)


def _cumsum_body(x_ref, o_ref, *, cb):
    length, width = x_ref.shape[1], x_ref.shape[2]
    tri = _lower_tri(cb)
    carry = jnp.zeros((1, width), F32)
    for j in range(length // cb):
        f = _tri_cumsum(tri, x_ref[0, j * cb:(j + 1) * cb, :]) + carry
        o_ref[0, j * cb:(j + 1) * cb, :] = f
        carry = f[cb - 1:cb, :]


def _cumsum(logf):
    b, length, h = logf.shape
    cb = _largest_block(length, 512, 8)
    spec = pl.BlockSpec((1, length, h), lambda i: (i, 0, 0))
    return pl.pallas_call(
        functools.partial(_cumsum_body, cb=cb),
        grid=(b,),
        in_specs=[spec],
        out_specs=spec,
        out_shape=jax.ShapeDtypeStruct(logf.shape, F32),
        compiler_params=_params("parallel"),
    )(logf)


def _attn_body(q_ref, k_ref, v_ref, fc_ref, fr_ref, o_ref, *, q_start):
    tq, kmax = q_ref.shape[0], k_ref.shape[1]
    qpos = q_start + lax.broadcasted_iota(jnp.int32, (tq, kmax), 0)
    kpos = lax.broadcasted_iota(jnp.int32, (tq, kmax), 1)
    mask = kpos <= qpos
    kb = k_ref[0].astype(BF16)
    vb = v_ref[0].astype(BF16)
    q = q_ref[...]
    fc = fc_ref[...]
    fr = fr_ref[0]
    outs = []
    for h in range(ATT_HEADS):
        g = h // ATT_GROUP
        hs = slice(h * ATT_HEAD_DIM, (h + 1) * ATT_HEAD_DIM)
        gs = slice(g * ATT_HEAD_DIM, (g + 1) * ATT_HEAD_DIM)
        s = lax.dot_general(q[:, hs], kb[:, gs], (((1,), (1,)), ((), ())), preferred_element_type=F32)
        s = s + fc[:, h:h + 1] - fr[h:h + 1, :]
        s = jnp.where(mask, s, -jnp.inf)
        p = jnp.exp(s - jnp.max(s, axis=-1, keepdims=True))
        l = jnp.sum(p, axis=-1, keepdims=True)
        o = jnp.dot(p.astype(BF16), vb[:, gs], preferred_element_type=F32)
        outs.append(o / l)
    o_ref[0] = jnp.concatenate(outs, axis=-1).astype(BF16)


def _attn(q, k, v, f_col, f_row, *, batch, q_off):
    t_q = q.shape[0] // batch
    lk = k.shape[1]
    tq = _largest_block(t_q, 384, 8)
    nq = t_q // tq
    outs = []
    for c in range(nq):
        q_start = q_off + c * tq
        kmax = min(lk, -(-(q_start + tq) // LANES) * LANES)
        qspec = lambda w, c=c: pl.BlockSpec((tq, w), lambda b: (b * nq + c, 0))
        kspec = pl.BlockSpec((1, kmax, ATT_KV_WIDTH), lambda b: (b, 0, 0))
        outs.append(pl.pallas_call(
            functools.partial(_attn_body, q_start=q_start),
            grid=(batch,),
            in_specs=[qspec(ATT_WIDTH), kspec, kspec, qspec(ATT_HEADS),
                      pl.BlockSpec((1, ATT_HEADS, kmax), lambda b: (b, 0, 0))],
            out_specs=pl.BlockSpec((1, tq, ATT_WIDTH), lambda b: (b, 0, 0)),
            out_shape=jax.ShapeDtypeStruct((batch, tq, ATT_WIDTH), BF16),
            compiler_params=_params("parallel"),
        )(q, k, v, f_col, f_row))
    return jnp.concatenate(outs, axis=1).reshape(batch * t_q, ATT_WIDTH)


def _hgrn_body(hq_ref, hf_ref, hi_ref, hg_ref, lbl_ref, gain_ref, s0_ref, o_ref, st_ref,
               oi_s, *, chunk, layer):
    length = hq_ref.shape[0]
    lbl = lbl_ref[...]
    e = jnp.exp(lbl - jnp.max(lbl, axis=0, keepdims=True))
    lb_all = jnp.sum(e[:layer + 1, :], axis=0, keepdims=True) / jnp.sum(e, axis=0, keepdims=True)
    gain_all = gain_ref[...]
    tri = _lower_tri(chunk)
    rows = lax.broadcasted_iota(jnp.int32, (chunk, 1), 0)
    st_ref[...] = s0_ref[...]

    def chunk_step(c, carry):
        r0 = pl.multiple_of(c * chunk, 16)
        for hh in range(HG_PER_STEP):
            ls = slice(hh * HG_DIM, (hh + 1) * HG_DIM)
            lb = lb_all[:, ls]
            st = st_ref[0, hh]
            f = lb + (1.0 - lb) * jax.nn.sigmoid(hf_ref[pl.ds(r0, chunk), ls])
            kk = 1.0 - f
            hq = hq_ref[pl.ds(r0, chunk), ls]
            qq = hq * jax.nn.sigmoid(hq)
            iv = hi_ref[pl.ds(r0, chunk), ls]
            g = _tri_cumsum(tri, jnp.log(f))
            o_inter = lax.dot_general((qq * jnp.exp(g)).astype(BF16), st.astype(BF16),
                                      (((1,), (1,)), ((), ())), preferred_element_type=F32)
            for t in range(chunk):
                n = (t // SUBLANES + 1) * SUBLANES
                d = jnp.where(rows[:n] <= t, g[t:t + 1, :] - g[:n, :], -jnp.inf)
                a = jnp.exp(d) * kk[:n, :] * qq[t:t + 1, :]
                sc = jnp.sum(a, axis=1, keepdims=True)
                oi_s[hh, t:t + 1, :] = jnp.sum(sc * iv[:n, :], axis=0, keepdims=True)
            o = o_inter + oi_s[hh]
            on = _rms(o, gain_all[:, ls])
            hg = hg_ref[pl.ds(r0, chunk), ls]
            o_ref[pl.ds(r0, chunk), ls] = (on * (hg * jax.nn.sigmoid(hg))).astype(BF16)
            g_end = g[chunk - 1:chunk, :]
            kd = kk * jnp.exp(g_end - g)
            upd = lax.dot_general(iv.astype(BF16), kd.astype(BF16),
                                  (((0,), (0,)), ((), ())), preferred_element_type=F32)
            st_ref[0, hh] = st * jnp.exp(g_end) + upd
        return carry

    lax.fori_loop(0, length // chunk, chunk_step, 0)


def _hgrn(hgrp, lb_logits, gain, s0t, *, batch, layer):
    length = hgrp.shape[0] // batch
    chunk = _largest_block(length, 64, 16)
    width = HG_PER_STEP * HG_DIM
    steps = HG_HEADS // HG_PER_STEP
    col = lambda off: pl.BlockSpec((length, width), lambda b, h: (b, off * steps + h))
    par = lambda a: pl.BlockSpec((a.shape[0], width), lambda b, h: (0, h))
    st = pl.BlockSpec((1, HG_PER_STEP, HG_DIM, HG_DIM), lambda b, h: (b, h, 0, 0))
    buf = pltpu.VMEM((HG_PER_STEP, chunk, HG_DIM), F32)
    return pl.pallas_call(
        functools.partial(_hgrn_body, chunk=chunk, layer=layer),
        grid=(batch, steps),
        in_specs=[col(0), col(1), col(2), col(3), par(lb_logits), par(gain), st],
        out_specs=[pl.BlockSpec((length, width), lambda b, h: (b, h)), st],
        out_shape=[jax.ShapeDtypeStruct((batch * length, HG_WIDTH), BF16),
                   jax.ShapeDtypeStruct(s0t.shape, F32)],
        scratch_shapes=[buf],
        compiler_params=_params("parallel", "parallel"),
    )(hgrp, hgrp, hgrp, hgrp, lb_logits, gain, s0t)


def _proj_body(x_ref, att_ref, hgo_ref, woa_ref, wob_ref, g2_ref, wq_ref, keys_ref, x1_ref, h2_ref, s_ref):
    x1 = (x_ref[...] + jnp.dot(att_ref[...], woa_ref[...], preferred_element_type=F32)
          + jnp.dot(hgo_ref[...], wob_ref[...], preferred_element_type=F32))
    x1_ref[...] = x1
    h2 = _rms(x1, g2_ref[...])
    h2_ref[...] = h2
    qp = jnp.dot(h2.astype(BF16), wq_ref[...], preferred_element_type=F32)
    for h in range(PEER_HEADS):
        for c in range(2):
            col = (2 * h + c) * PEER_HALF
            s_ref[2 * h + c] = lax.dot_general(keys_ref[h, c], qp[:, col:col + PEER_HALF].astype(BF16),
                                               (((1,), (1,)), ((), ())), preferred_element_type=F32)


def _proj(x, att, hgo, woa, wob, g2, wq, keys):
    n = x.shape[0]
    tb = _largest_block(n, TOKEN_BLOCK_PROJ, LANES)
    row = lambda w: pl.BlockSpec((tb, w), lambda i: (i, 0))
    full = lambda a: pl.BlockSpec(a.shape, lambda i: (0,) * a.ndim)
    return pl.pallas_call(
        _proj_body,
        grid=(n // tb,),
        in_specs=[row(D_MODEL), row(ATT_WIDTH), row(HG_WIDTH), full(woa), full(wob), full(g2), full(wq), full(keys)],
        out_specs=[row(D_MODEL), row(D_MODEL), pl.BlockSpec((2 * PEER_HEADS, PEER_N_KEYS, tb), lambda i: (0, 0, i))],
        out_shape=[jax.ShapeDtypeStruct((n, D_MODEL), F32),
                   jax.ShapeDtypeStruct((n, D_MODEL), F32),
                   jax.ShapeDtypeStruct((2 * PEER_HEADS, PEER_N_KEYS, n), F32)],
        compiler_params=_params("parallel"),
    )(x, att, hgo, woa, wob, g2, wq, keys)


def _stream_topk(w_ref, n, k_out, emit, payload_ref=None):
    n_acc = 4
    shape = w_ref.shape[1:]

    def step(j, prev):
        accs = [(jnp.full(shape, -jnp.inf, F32), jnp.full(shape, n, jnp.int32), jnp.full(shape, -1, jnp.int32))
                for _ in range(n_acc)]
        for k in range(n):
            v = jnp.where(prev == k, -jnp.inf, w_ref[k])
            w_ref[k] = v
            m, pos, pay = accs[k % n_acc]
            new = v > m
            pay = pay if payload_ref is None else jnp.where(new, payload_ref[k], pay)
            accs[k % n_acc] = (jnp.where(new, v, m), jnp.where(new, k, pos), pay)
        while len(accs) > 1:
            a, b = accs[0], accs[1]
            take = (b[0] > a[0]) | ((b[0] == a[0]) & (b[1] < a[1]))
            accs = accs[2:] + [tuple(jnp.where(take, y, x) for x, y in zip(a, b))]
        m, pos, pay = accs[0]
        emit(j, m, pos if payload_ref is None else pay)
        return pos

    lax.fori_loop(0, k_out, step, jnp.full(shape, n, jnp.int32))


def _pair_list(k):
    return [(i, j) for i in range(k) for j in range(k // (i + 1))]


def _topk_body(s_ref, eidx_ref, gate_ref, stage_s, w_s, va_s, ia_s, vb_s, ib_s, cand_s, cid_s, ts_s):
    tiles = ROUTE_TOKENS // LANES
    for c, (v_s, i_s) in enumerate(((va_s, ia_s), (vb_s, ib_s))):
        for lt in range(tiles):
            stage_s[lt * PEER_N_KEYS:(lt + 1) * PEER_N_KEYS, :] = s_ref[c, :, lt * LANES:(lt + 1) * LANES]
        for k in range(PEER_N_KEYS):
            w_s[k] = stage_s[pl.ds(k, tiles, stride=PEER_N_KEYS), :]

        def emit(j, m, pos, v_s=v_s, i_s=i_s):
            v_s[j] = m
            i_s[j] = pos

        _stream_topk(w_s, PEER_N_KEYS, PEER_TOPK, emit)
    pairs = _pair_list(PEER_TOPK)
    for p, (i, j) in enumerate(pairs):
        cand_s[p] = va_s[i] + vb_s[j]
        cid_s[p] = ia_s[i] * PEER_N_KEYS + ib_s[j]

    def emit_final(j, m, pay):
        ts_s[j] = m
        eidx_ref[0, j] = pay

    _stream_topk(cand_s, len(pairs), PEER_TOPK, emit_final, payload_ref=cid_s)
    ex = [jnp.exp(ts_s[j] - ts_s[0]) for j in range(PEER_TOPK)]
    tot = ex[0]
    for e in ex[1:]:
        tot = tot + e
    for j in range(PEER_TOPK):
        gate_ref[0, j] = ex[j] / tot


def _route_topk(scores):
    n = scores.shape[2]
    assert n % ROUTE_TOKENS == 0 and ROUTE_TOKENS // LANES == SUBLANES
    nb = n // ROUTE_TOKENS
    n_pairs = len(_pair_list(PEER_TOPK))
    tile = lambda rows, dt: pltpu.VMEM((rows, SUBLANES, LANES), dt)
    sel = pl.BlockSpec((1, PEER_TOPK, SUBLANES, LANES), lambda i, h: (i, h, 0, 0))
    return pl.pallas_call(
        _topk_body,
        grid=(nb, PEER_HEADS),
        in_specs=[pl.BlockSpec((2, PEER_N_KEYS, ROUTE_TOKENS), lambda i, h: (h, 0, i))],
        out_specs=[sel, sel],
        out_shape=[jax.ShapeDtypeStruct((nb, PEER_SEL, SUBLANES, LANES), jnp.int32),
                   jax.ShapeDtypeStruct((nb, PEER_SEL, SUBLANES, LANES), F32)],
        scratch_shapes=[pltpu.VMEM((SUBLANES * PEER_N_KEYS, LANES), F32), tile(PEER_N_KEYS, F32),
                        tile(PEER_TOPK, F32), tile(PEER_TOPK, jnp.int32),
                        tile(PEER_TOPK, F32), tile(PEER_TOPK, jnp.int32),
                        tile(n_pairs, F32), tile(n_pairs, jnp.int32), tile(PEER_TOPK, F32)],
        compiler_params=_params("parallel", "parallel"),
    )(scores)


def _pack_bf16_pair(hi, lo):
    bits = lambda a: lax.bitcast_convert_type(a.astype(BF16), jnp.uint16).astype(jnp.uint32)
    return (bits(hi) << 16) | bits(lo)


def _packed_hi(word):
    return lax.bitcast_convert_type(word & jnp.uint32(0xFFFF0000), F32)


def _packed_lo(word):
    return lax.bitcast_convert_type(word << 16, F32)


def _experts_body(eidx_ref, enext_ref, gate_ref, h2_ref, x1_ref, gf_ref, uv_ref, y_ref,
                  idx_s, buf0, buf1, buf2, buf3, idx_sem, row_sem, *, tb):
    bufs = (buf0, buf1, buf2, buf3)
    step, last = pl.program_id(0), pl.num_programs(0) - 1
    n_groups = tb // GATHER_GROUP
    ahead = GATHER_SETS - 1
    head = ahead * GATHER_GROUP
    idx_cps = (pltpu.make_async_copy(eidx_ref, idx_s.at[pl.ds(0, tb)], idx_sem.at[0]),
               pltpu.make_async_copy(enext_ref.at[pl.ds(0, head)], idx_s.at[pl.ds(tb, head)], idx_sem.at[1]))
    for cp in idx_cps:
        cp.start()
    for cp in idx_cps:
        cp.wait()

    def issue_group(g, slot_set):
        for j in range(GATHER_GROUP):
            t = g * GATHER_GROUP + j
            for r in range(PEER_SEL):
                pltpu.make_async_copy(uv_ref.at[idx_s[t, r]],
                                      bufs[slot_set].at[j, pl.ds(r * SUBLANES, SUBLANES), :],
                                      row_sem.at[slot_set, j]).start(priority=r % 2)

    def wait(slot_set, j):
        dst = bufs[slot_set].at[j]
        pltpu.make_async_copy(dst, dst, row_sem.at[slot_set, j]).wait()

    gf = gf_ref[...]
    lane_t = lax.broadcasted_iota(jnp.int32, (PEER_SEL, tb), 1)

    def group(g, slot_set):
        for j in range(GATHER_GROUP):
            wait(slot_set, j)
        issue_group(g + ahead, (slot_set + ahead) % GATHER_SETS)
        buf = bufs[slot_set]
        for j in range(GATHER_GROUP):
            t = g * GATHER_GROUP + j
            xrow = h2_ref[pl.ds(t, 1), :]
            tiles = [buf[j, pl.ds(s, PEER_SEL, stride=SUBLANES), :] for s in range(SUBLANES)]
            p = _packed_hi(tiles[0]) * xrow[:, :LANES]
            for s in range(1, SUBLANES):
                p = p + _packed_hi(tiles[s]) * xrow[:, s * LANES:(s + 1) * LANES]
            hcol = jnp.sum(p, axis=1, keepdims=True)
            act = 0.5 * hcol * (1.0 + lax.erf(hcol * (2.0 ** -0.5)))
            gcol = jnp.sum(jnp.where(lane_t == t, gate_ref[0], 0.0), axis=1, keepdims=True)
            w = gcol * act
            o = [jnp.sum(w * _packed_lo(tiles[s]), axis=0, keepdims=True) for s in range(SUBLANES)]
            orow = jnp.concatenate(o, axis=1)
            y_ref[pl.ds(t, 1), :] = _rms(x1_ref[pl.ds(t, 1), :] + orow, gf)

    def first_head(_, carry):
        for s in range(ahead):
            issue_group(s, s)
        return carry

    lax.fori_loop(0, (step == 0).astype(jnp.int32), first_head, 0)

    def main(k, carry):
        for s in range(GATHER_SETS):
            group(k * GATHER_SETS + s, s)
        return carry

    lax.fori_loop(0, n_groups // GATHER_SETS, main, 0)

    @pl.when(step == last)
    def _():
        for s in range(ahead):
            for j in range(GATHER_GROUP):
                wait(s, j)


def _experts(eidx, gate, h2, x1, gf, uv):
    n = h2.shape[0]
    tb = TOKEN_BLOCK_EXPERT
    tbr = gate.shape[2]
    assert GATHER_SETS == 4 and tbr % tb == 0 and tb % (GATHER_GROUP * GATHER_SETS) == 0
    per, nb = tbr // tb, n // tb
    head = (GATHER_SETS - 1) * GATHER_GROUP
    row = lambda w: pl.BlockSpec((tb, w), lambda i: (i, 0))
    nxt = pl.BlockSpec((tb, PEER_SEL), lambda i: (jnp.minimum(i + 1, nb - 1), 0))
    sel = pl.BlockSpec((1, PEER_SEL, tb), lambda i: (i // per, 0, i % per))
    slots = pltpu.VMEM((GATHER_GROUP, PEER_SEL * SUBLANES, LANES), jnp.uint32)
    return pl.pallas_call(
        functools.partial(_experts_body, tb=tb),
        grid=(nb,),
        in_specs=[row(PEER_SEL), nxt, sel, row(D_MODEL), row(D_MODEL), pl.BlockSpec(gf.shape, lambda i: (0, 0)),
                  pl.BlockSpec(memory_space=pl.ANY)],
        out_specs=row(D_MODEL),
        out_shape=jax.ShapeDtypeStruct((n, D_MODEL), F32),
        scratch_shapes=[pltpu.SMEM((tb + head, PEER_SEL), jnp.int32), slots, slots, slots, slots,
                        pltpu.SemaphoreType.DMA((2,)),
                        pltpu.SemaphoreType.DMA((GATHER_SETS, GATHER_GROUP))],
        compiler_params=_params("arbitrary"),
    )(eidx, eidx, gate, h2, x1, gf, uv)


def _mixer(x, batch, params, layer, cache=None):
    q, k, v, logf, hgrp = _inproj(x, params["g1"], params["wqkv"], params["wfg"], params["bfg"], params["wh"])
    length = x.shape[0] // batch
    if cache is None:
        k_all = k.reshape(batch, length, ATT_KV_WIDTH)
        v_all = v.reshape(batch, length, ATT_KV_WIDTH)
        logf_all, q_off = logf.reshape(batch, length, ATT_HEADS), 0
        s0t = jnp.zeros((batch, HG_HEADS, HG_DIM, HG_DIM), F32)
    else:
        cache_k, cache_v, cache_logf, state = cache
        past = cache_k.shape[1]
        cat = lambda c, new: jnp.concatenate(
            [c.reshape(batch, past, -1).astype(F32), new.reshape(batch, length, -1)], axis=1)
        k_all, v_all = cat(cache_k, k), cat(cache_v, v)
        logf_all, q_off = cat(cache_logf, logf), past
        s0t = jnp.swapaxes(state.astype(F32), -1, -2)
    f_all = _cumsum(logf_all)
    f_col = f_all[:, q_off:].reshape(batch * length, ATT_HEADS)
    f_row = jnp.swapaxes(f_all, 1, 2)
    att = _attn(q, k_all, v_all, f_col, f_row, batch=batch, q_off=q_off)
    hgo, st = _hgrn(hgrp, params["lb_logits"], params["hg_gain"], s0t, batch=batch, layer=layer)
    return att, hgo, (k, v, logf, jnp.swapaxes(st, -1, -2))


def kernel(x_prompt, x_sample, cache_k, cache_v, cache_logf, state_hgrn, meta_tokens,
           w_in, b_forget, hg_lb_logits, hg_norm_gain, w_out, norm1_gain, norm2_gain,
           peer_w_query, peer_sub_keys, peer_expert_u, peer_expert_v, final_norm_gain):
    depth = w_in.shape[0]
    assert depth == 1, "single-layer trunk: the PEER stage fuses the final norm"
    assert D_MODEL == SUBLANES * LANES, "an expert's packed u|v row must fill exactly one (SUBLANES, LANES) slab"
    bp, seq, _ = x_prompt.shape
    bs, dseq, _ = x_sample.shape
    lp = N_META + seq
    meta = jnp.broadcast_to(meta_tokens.astype(x_prompt.dtype)[None], (bp, N_META, D_MODEL))
    xp = jnp.concatenate([meta, x_prompt], axis=1).reshape(bp * lp, D_MODEL)
    xs = x_sample.reshape(bs * dseq, D_MODEL)

    l = 0
    o_fg = ATT_WIDTH + 2 * ATT_KV_WIDTH
    o_h = o_fg + ATT_HEADS
    row = lambda a: a.reshape(1, -1).astype(F32)
    params = {
        "g1": row(norm1_gain[l]),
        "wqkv": w_in[l][:, :o_fg].astype(BF16),
        "wfg": jnp.pad(w_in[l][:, o_fg:o_h], ((0, 0), (0, LANES - ATT_HEADS))).astype(BF16),
        "bfg": row(b_forget[l]),
        "wh": w_in[l][:, o_h:].astype(BF16),
        "lb_logits": hg_lb_logits.astype(F32),
        "hg_gain": row(hg_norm_gain[l]),
    }
    att_p, hgo_p, sp = _mixer(xp, bp, params, l)
    att_s, hgo_s, ss = _mixer(xs, bs, params, l,
                              cache=(cache_k[l], cache_v[l], cache_logf[l], state_hgrn[l]))

    x_all = jnp.concatenate([xp, xs], axis=0)
    att = jnp.concatenate([att_p, att_s], axis=0)
    hgo = jnp.concatenate([hgo_p, hgo_s], axis=0)
    wo = w_out[l].astype(BF16)
    x1, h2, scores = _proj(x_all, att, hgo, wo[:ATT_WIDTH], wo[ATT_WIDTH:], row(norm2_gain[l]),
                           peer_w_query[l].astype(BF16), peer_sub_keys[l].astype(BF16))
    eidx, gate = (a.reshape(-1, PEER_SEL, ROUTE_TOKENS) for a in _route_topk(scores))
    n_exp = peer_expert_u.shape[1]
    uv = _pack_bf16_pair(peer_expert_u[l], peer_expert_v[l]).reshape(n_exp, SUBLANES, LANES)
    eidx_t = jnp.swapaxes(eidx, 1, 2).reshape(-1, PEER_SEL)
    y = _experts(eidx_t, gate, h2, x1, row(final_norm_gain), uv)

    n_p = bp * lp
    y_prompt = y[:n_p].reshape(bp, lp, D_MODEL)[:, N_META:]
    y_sample = y[n_p:].reshape(bs, dseq, D_MODEL)

    def states(s, batch, length):
        k, v, logf, st = s
        return (k.reshape(1, batch, length, ATT_KV_HEADS, ATT_HEAD_DIM),
                v.reshape(1, batch, length, ATT_KV_HEADS, ATT_HEAD_DIM),
                logf.reshape(1, batch, length, ATT_HEADS),
                st.reshape(1, batch, HG_HEADS, HG_DIM, HG_DIM))

    return (y_prompt, y_sample) + states(sp, bp, lp) + states(ss, bs, dseq)
```

```python
import functools

import jax
import jax.numpy as jnp
from jax import lax
from jax.experimental import pallas as pl
from jax.experimental.pallas import tpu as pltpu

F32 = jnp.float32
BF16 = jnp.bfloat16
EPS = 1e-6
N_META = 16

D_MODEL = 1024
ATT_HEADS = 8
ATT_KV_HEADS = 4
ATT_HEAD_DIM = 64
ATT_GROUP = ATT_HEADS // ATT_KV_HEADS
ATT_WIDTH = ATT_HEADS * ATT_HEAD_DIM
ATT_KV_WIDTH = ATT_KV_HEADS * ATT_HEAD_DIM
HG_HEADS = 4
HG_DIM = 128
HG_WIDTH = HG_HEADS * HG_DIM
HG_PER_STEP = 4
PEER_HEADS = 8
PEER_N_KEYS = 128
PEER_TOPK = 16
PEER_HALF = 128
PEER_SEL = PEER_HEADS * PEER_TOPK

LANES = 128
SUBLANES = 8
VMEM_LIMIT = 48 * 1024 * 1024

TOKEN_BLOCK_PROJ = 512
ROUTE_TOKENS = SUBLANES * LANES
TOKEN_BLOCK_EXPERT = 256
GATHER_GROUP = 8
GATHER_SETS = 4


def _params(*sem):
    return pltpu.CompilerParams(dimension_semantics=sem, vmem_limit_bytes=VMEM_LIMIT)


def _largest_block(n, cap, mult):
    best = None
    for d in range(mult, cap + 1, mult):
        if n % d == 0:
            best = d
    assert best is not None, (n, cap, mult)
    return best


def _split3(x):
    hi = x.astype(BF16)
    r1 = x - hi.astype(F32)
    mid = r1.astype(BF16)
    lo = (r1 - mid.astype(F32)).astype(BF16)
    return hi, mid, lo


def _tri_cumsum(tri, x):
    hi, mid, lo = _split3(x)
    d = lambda a: jnp.dot(tri, a, preferred_element_type=F32)
    return (d(lo) + d(mid)) + d(hi)


def _lower_tri(c):
    r = lax.broadcasted_iota(jnp.int32, (c, c), 0)
    s = lax.broadcasted_iota(jnp.int32, (c, c), 1)
    return jnp.where(s <= r, 1.0, 0.0).astype(BF16)


def _rms(x, gain):
    return x * lax.rsqrt(jnp.mean(x * x, axis=-1, keepdims=True) + EPS) * gain


def _inproj_body(x_ref, g_ref, wqkv_ref, wfg_ref, bfg_ref, wh_ref,
                 q_ref, k_ref, v_ref, logf_ref, hgrp_ref):
    hb = _rms(x_ref[...], g_ref[...]).astype(BF16)
    qkv = jnp.dot(hb, wqkv_ref[...], preferred_element_type=F32)
    q_ref[...] = (qkv[:, :ATT_WIDTH] * (ATT_HEAD_DIM ** -0.5)).astype(BF16)
    k_ref[...] = qkv[:, ATT_WIDTH:ATT_WIDTH + ATT_KV_WIDTH]
    v_ref[...] = qkv[:, ATT_WIDTH + ATT_KV_WIDTH:]
    fg = jnp.dot(hb, wfg_ref[...], preferred_element_type=F32)[:, :ATT_HEADS] + bfg_ref[...]
    logf_ref[...] = jnp.minimum(fg, 0.0) - jnp.log1p(jnp.exp(-jnp.abs(fg)))
    hgrp_ref[...] = jnp.dot(hb, wh_ref[...], preferred_element_type=F32)


def _inproj(x, gain, wqkv, wfg, bfg, wh):
    n = x.shape[0]
    tb = _largest_block(n, TOKEN_BLOCK_PROJ, 8)
    row = lambda w: pl.BlockSpec((tb, w), lambda i: (i, 0))
    full = lambda a: pl.BlockSpec(a.shape, lambda i: (0,) * a.ndim)
    return pl.pallas_call(
        _inproj_body,
        grid=(n // tb,),
        in_specs=[row(D_MODEL), full(gain), full(wqkv), full(wfg), full(bfg), full(wh)],
        out_specs=[row(ATT_WIDTH), row(ATT_KV_WIDTH), row(ATT_KV_WIDTH), row(ATT_HEADS), row(4 * HG_WIDTH)],
        out_shape=[jax.ShapeDtypeStruct((n, ATT_WIDTH), BF16),
                   jax.ShapeDtypeStruct((n, ATT_KV_WIDTH), F32),
                   jax.ShapeDtypeStruct((n, ATT_KV_WIDTH), F32),
                   jax.ShapeDtypeStruct((n, ATT_HEADS), F32),
                   jax.ShapeDtypeStruct((n, 4 * HG_WIDTH), F32)],
        compiler_params=_params("parallel"),
    )(x, gain, wqkv, wfg, bfg, wh)


def _cumsum_body(x_ref, o_ref, *, cb):
    length, width = x_ref.shape[1], x_ref.shape[2]
    tri = _lower_tri(cb)
    carry = jnp.zeros((1, width), F32)
    for j in range(length // cb):
        f = _tri_cumsum(tri, x_ref[0, j * cb:(j + 1) * cb, :]) + carry
        o_ref[0, j * cb:(j + 1) * cb, :] = f
        carry = f[cb - 1:cb, :]


def _cumsum(logf):
    b, length, h = logf.shape
    cb = _largest_block(length, 512, 8)
    spec = pl.BlockSpec((1, length, h), lambda i: (i, 0, 0))
    return pl.pallas_call(
        functools.partial(_cumsum_body, cb=cb),
        grid=(b,),
        in_specs=[spec],
        out_specs=spec,
        out_shape=jax.ShapeDtypeStruct(logf.shape, F32),
        compiler_params=_params("parallel"),
    )(logf)


def _attn_body(q_ref, k_ref, v_ref, fc_ref, fr_ref, o_ref, *, q_start):
    tq, kmax = q_ref.shape[0], k_ref.shape[1]
    qpos = q_start + lax.broadcasted_iota(jnp.int32, (tq, kmax), 0)
    kpos = lax.broadcasted_iota(jnp.int32, (tq, kmax), 1)
    mask = kpos <= qpos
    kb = k_ref[0].astype(BF16)
    vb = v_ref[0].astype(BF16)
    q = q_ref[...]
    fc = fc_ref[...]
    fr = fr_ref[0]
    outs = []
    for h in range(ATT_HEADS):
        g = h // ATT_GROUP
        hs = slice(h * ATT_HEAD_DIM, (h + 1) * ATT_HEAD_DIM)
        gs = slice(g * ATT_HEAD_DIM, (g + 1) * ATT_HEAD_DIM)
        s = lax.dot_general(q[:, hs], kb[:, gs], (((1,), (1,)), ((), ())), preferred_element_type=F32)
        s = s + fc[:, h:h + 1] - fr[h:h + 1, :]
        s = jnp.where(mask, s, -jnp.inf)
        p = jnp.exp(s - jnp.max(s, axis=-1, keepdims=True))
        l = jnp.sum(p, axis=-1, keepdims=True)
        o = jnp.dot(p.astype(BF16), vb[:, gs], preferred_element_type=F32)
        outs.append(o / l)
    o_ref[0] = jnp.concatenate(outs, axis=-1).astype(BF16)


def _attn(q, k, v, f_col, f_row, *, batch, q_off):
    t_q = q.shape[0] // batch
    lk = k.shape[1]
    tq = _largest_block(t_q, 384, 8)
    nq = t_q // tq
    outs = []
    for c in range(nq):
        q_start = q_off + c * tq
        kmax = min(lk, -(-(q_start + tq) // LANES) * LANES)
        qspec = lambda w, c=c: pl.BlockSpec((tq, w), lambda b: (b * nq + c, 0))
        kspec = pl.BlockSpec((1, kmax, ATT_KV_WIDTH), lambda b: (b, 0, 0))
        outs.append(pl.pallas_call(
            functools.partial(_attn_body, q_start=q_start),
            grid=(batch,),
            in_specs=[qspec(ATT_WIDTH), kspec, kspec, qspec(ATT_HEADS),
                      pl.BlockSpec((1, ATT_HEADS, kmax), lambda b: (b, 0, 0))],
            out_specs=pl.BlockSpec((1, tq, ATT_WIDTH), lambda b: (b, 0, 0)),
            out_shape=jax.ShapeDtypeStruct((batch, tq, ATT_WIDTH), BF16),
            compiler_params=_params("parallel"),
        )(q, k, v, f_col, f_row))
    return jnp.concatenate(outs, axis=1).reshape(batch * t_q, ATT_WIDTH)


def _hgrn_body(hq_ref, hf_ref, hi_ref, hg_ref, lbl_ref, gain_ref, s0_ref, o_ref, st_ref,
               oi_s, *, chunk, layer):
    length = hq_ref.shape[0]
    lbl = lbl_ref[...]
    e = jnp.exp(lbl - jnp.max(lbl, axis=0, keepdims=True))
    lb_all = jnp.sum(e[:layer + 1, :], axis=0, keepdims=True) / jnp.sum(e, axis=0, keepdims=True)
    gain_all = gain_ref[...]
    tri = _lower_tri(chunk)
    rows = lax.broadcasted_iota(jnp.int32, (chunk, 1), 0)
    st_ref[...] = s0_ref[...]

    def chunk_step(c, carry):
        r0 = pl.multiple_of(c * chunk, 16)
        for hh in range(HG_PER_STEP):
            ls = slice(hh * HG_DIM, (hh + 1) * HG_DIM)
            lb = lb_all[:, ls]
            st = st_ref[0, hh]
            f = lb + (1.0 - lb) * jax.nn.sigmoid(hf_ref[pl.ds(r0, chunk), ls])
            kk = 1.0 - f
            hq = hq_ref[pl.ds(r0, chunk), ls]
            qq = hq * jax.nn.sigmoid(hq)
            iv = hi_ref[pl.ds(r0, chunk), ls]
            g = _tri_cumsum(tri, jnp.log(f))
            o_inter = lax.dot_general((qq * jnp.exp(g)).astype(BF16), st.astype(BF16),
                                      (((1,), (1,)), ((), ())), preferred_element_type=F32)
            for t in range(chunk):
                n = (t // SUBLANES + 1) * SUBLANES
                d = jnp.where(rows[:n] <= t, g[t:t + 1, :] - g[:n, :], -jnp.inf)
                a = jnp.exp(d) * kk[:n, :] * qq[t:t + 1, :]
                sc = jnp.sum(a, axis=1, keepdims=True)
                oi_s[hh, t:t + 1, :] = jnp.sum(sc * iv[:n, :], axis=0, keepdims=True)
            o = o_inter + oi_s[hh]
            on = _rms(o, gain_all[:, ls])
            hg = hg_ref[pl.ds(r0, chunk), ls]
            o_ref[pl.ds(r0, chunk), ls] = (on * (hg * jax.nn.sigmoid(hg))).astype(BF16)
            g_end = g[chunk - 1:chunk, :]
            kd = kk * jnp.exp(g_end - g)
            upd = lax.dot_general(iv.astype(BF16), kd.astype(BF16),
                                  (((0,), (0,)), ((), ())), preferred_element_type=F32)
            st_ref[0, hh] = st * jnp.exp(g_end) + upd
        return carry

    lax.fori_loop(0, length // chunk, chunk_step, 0)


def _hgrn(hgrp, lb_logits, gain, s0t, *, batch, layer):
    length = hgrp.shape[0] // batch
    chunk = _largest_block(length, 64, 16)
    width = HG_PER_STEP * HG_DIM
    steps = HG_HEADS // HG_PER_STEP
    col = lambda off: pl.BlockSpec((length, width), lambda b, h: (b, off * steps + h))
    par = lambda a: pl.BlockSpec((a.shape[0], width), lambda b, h: (0, h))
    st = pl.BlockSpec((1, HG_PER_STEP, HG_DIM, HG_DIM), lambda b, h: (b, h, 0, 0))
    buf = pltpu.VMEM((HG_PER_STEP, chunk, HG_DIM), F32)
    return pl.pallas_call(
        functools.partial(_hgrn_body, chunk=chunk, layer=layer),
        grid=(batch, steps),
        in_specs=[col(0), col(1), col(2), col(3), par(lb_logits), par(gain), st],
        out_specs=[pl.BlockSpec((length, width), lambda b, h: (b, h)), st],
        out_shape=[jax.ShapeDtypeStruct((batch * length, HG_WIDTH), BF16),
                   jax.ShapeDtypeStruct(s0t.shape, F32)],
        scratch_shapes=[buf],
        compiler_params=_params("parallel", "parallel"),
    )(hgrp, hgrp, hgrp, hgrp, lb_logits, gain, s0t)


def _proj_body(xa_ref, atta_ref, hgoa_ref, xb_ref, attb_ref, hgob_ref, woa_ref, wob_ref, g2_ref, wq_ref, keys_ref,
               x1_ref, h2_ref, s_ref, *, steps_a):
    first = pl.program_id(0) < steps_a
    pick = lambda a_ref, b_ref: jnp.where(first, a_ref[...], b_ref[...])
    x1 = (pick(xa_ref, xb_ref) + jnp.dot(pick(atta_ref, attb_ref), woa_ref[...], preferred_element_type=F32)
          + jnp.dot(pick(hgoa_ref, hgob_ref), wob_ref[...], preferred_element_type=F32))
    x1_ref[...] = x1
    h2 = _rms(x1, g2_ref[...])
    h2_ref[...] = h2
    qp = jnp.dot(h2.astype(BF16), wq_ref[...], preferred_element_type=F32)
    for h in range(PEER_HEADS):
        for c in range(2):
            col = (2 * h + c) * PEER_HALF
            s_ref[2 * h + c] = lax.dot_general(keys_ref[h, c], qp[:, col:col + PEER_HALF].astype(BF16),
                                               (((1,), (1,)), ((), ())), preferred_element_type=F32)


def _proj(group_a, group_b, woa, wob, g2, wq, keys):
    na, nb = group_a[0].shape[0], group_b[0].shape[0]
    n = na + nb
    tb = min(_largest_block(na, TOKEN_BLOCK_PROJ, LANES), _largest_block(nb, TOKEN_BLOCK_PROJ, LANES))
    assert na % tb == 0 and nb % tb == 0
    steps_a = na // tb
    row = lambda w: pl.BlockSpec((tb, w), lambda i: (i, 0))
    row_a = lambda w: pl.BlockSpec((tb, w), lambda i: (jnp.minimum(i, steps_a - 1), 0))
    row_b = lambda w: pl.BlockSpec((tb, w), lambda i: (jnp.maximum(i - steps_a, 0), 0))
    full = lambda a: pl.BlockSpec(a.shape, lambda i: (0,) * a.ndim)
    widths = (D_MODEL, ATT_WIDTH, HG_WIDTH)
    return pl.pallas_call(
        functools.partial(_proj_body, steps_a=steps_a),
        grid=(n // tb,),
        in_specs=[row_a(w) for w in widths] + [row_b(w) for w in widths]
                 + [full(woa), full(wob), full(g2), full(wq), full(keys)],
        out_specs=[row(D_MODEL), row(D_MODEL), pl.BlockSpec((2 * PEER_HEADS, PEER_N_KEYS, tb), lambda i: (0, 0, i))],
        out_shape=[jax.ShapeDtypeStruct((n, D_MODEL), F32),
                   jax.ShapeDtypeStruct((n, D_MODEL), F32),
                   jax.ShapeDtypeStruct((2 * PEER_HEADS, PEER_N_KEYS, n), F32)],
        compiler_params=_params("arbitrary"),
    )(*group_a, *group_b, woa, wob, g2, wq, keys)


def _stream_topk(w_ref, n, k_out, emit, payload_ref=None):
    n_acc = 4
    shape = w_ref.shape[1:]

    def step(j, prev):
        accs = [(jnp.full(shape, -jnp.inf, F32), jnp.full(shape, n, jnp.int32), jnp.full(shape, -1, jnp.int32))
                for _ in range(n_acc)]
        for k in range(n):
            v = jnp.where(prev == k, -jnp.inf, w_ref[k])
            w_ref[k] = v
            m, pos, pay = accs[k % n_acc]
            new = v > m
            pay = pay if payload_ref is None else jnp.where(new, payload_ref[k], pay)
            accs[k % n_acc] = (jnp.where(new, v, m), jnp.where(new, k, pos), pay)
        while len(accs) > 1:
            a, b = accs[0], accs[1]
            take = (b[0] > a[0]) | ((b[0] == a[0]) & (b[1] < a[1]))
            accs = accs[2:] + [tuple(jnp.where(take, y, x) for x, y in zip(a, b))]
        m, pos, pay = accs[0]
        emit(j, m, pos if payload_ref is None else pay)
        return pos

    lax.fori_loop(0, k_out, step, jnp.full(shape, n, jnp.int32))


def _pair_list(k):
    return [(i, j) for i in range(k) for j in range(k // (i + 1))]


def _topk_body(s_ref, eidx_ref, gate_ref, stage_s, w_s, va_s, ia_s, vb_s, ib_s, cand_s, cid_s, ts_s):
    tiles = ROUTE_TOKENS // LANES
    for c, (v_s, i_s) in enumerate(((va_s, ia_s), (vb_s, ib_s))):
        for lt in range(tiles):
            stage_s[lt * PEER_N_KEYS:(lt + 1) * PEER_N_KEYS, :] = s_ref[c, :, lt * LANES:(lt + 1) * LANES]
        for k in range(PEER_N_KEYS):
            w_s[k] = stage_s[pl.ds(k, tiles, stride=PEER_N_KEYS), :]

        def emit(j, m, pos, v_s=v_s, i_s=i_s):
            v_s[j] = m
            i_s[j] = pos

        _stream_topk(w_s, PEER_N_KEYS, PEER_TOPK, emit)
    pairs = _pair_list(PEER_TOPK)
    for p, (i, j) in enumerate(pairs):
        cand_s[p] = va_s[i] + vb_s[j]
        cid_s[p] = ia_s[i] * PEER_N_KEYS + ib_s[j]

    def emit_final(j, m, pay):
        ts_s[j] = m
        eidx_ref[0, j] = pay

    _stream_topk(cand_s, len(pairs), PEER_TOPK, emit_final, payload_ref=cid_s)
    ex = [jnp.exp(ts_s[j] - ts_s[0]) for j in range(PEER_TOPK)]
    tot = ex[0]
    for e in ex[1:]:
        tot = tot + e
    for j in range(PEER_TOPK):
        gate_ref[0, j] = ex[j] / tot


def _route_topk(scores):
    n = scores.shape[2]
    assert n % ROUTE_TOKENS == 0 and ROUTE_TOKENS // LANES == SUBLANES
    nb = n // ROUTE_TOKENS
    n_pairs = len(_pair_list(PEER_TOPK))
    tile = lambda rows, dt: pltpu.VMEM((rows, SUBLANES, LANES), dt)
    sel = pl.BlockSpec((1, PEER_TOPK, SUBLANES, LANES), lambda i, h: (i, h, 0, 0))
    return pl.pallas_call(
        _topk_body,
        grid=(nb, PEER_HEADS),
        in_specs=[pl.BlockSpec((2, PEER_N_KEYS, ROUTE_TOKENS), lambda i, h: (h, 0, i))],
        out_specs=[sel, sel],
        out_shape=[jax.ShapeDtypeStruct((nb, PEER_SEL, SUBLANES, LANES), jnp.int32),
                   jax.ShapeDtypeStruct((nb, PEER_SEL, SUBLANES, LANES), F32)],
        scratch_shapes=[pltpu.VMEM((SUBLANES * PEER_N_KEYS, LANES), F32), tile(PEER_N_KEYS, F32),
                        tile(PEER_TOPK, F32), tile(PEER_TOPK, jnp.int32),
                        tile(PEER_TOPK, F32), tile(PEER_TOPK, jnp.int32),
                        tile(n_pairs, F32), tile(n_pairs, jnp.int32), tile(PEER_TOPK, F32)],
        compiler_params=_params("parallel", "parallel"),
    )(scores)


def _pack_bf16_pair(hi, lo):
    bits = lambda a: lax.bitcast_convert_type(a.astype(BF16), jnp.uint16).astype(jnp.uint32)
    return (bits(hi) << 16) | bits(lo)


def _packed_hi(word):
    return lax.bitcast_convert_type(word & jnp.uint32(0xFFFF0000), F32)


def _packed_lo(word):
    return lax.bitcast_convert_type(word << 16, F32)


def _experts_body(eidx_ref, enext_ref, gate_ref, h2_ref, x1_ref, gf_ref, uv_ref, y_ref,
                  idx_s, buf0, buf1, buf2, buf3, idx_sem, row_sem, *, tb):
    bufs = (buf0, buf1, buf2, buf3)
    step, last = pl.program_id(0), pl.num_programs(0) - 1
    n_groups = tb // GATHER_GROUP
    ahead = GATHER_SETS - 1
    head = ahead * GATHER_GROUP
    idx_cps = (pltpu.make_async_copy(eidx_ref, idx_s.at[pl.ds(0, tb)], idx_sem.at[0]),
               pltpu.make_async_copy(enext_ref.at[pl.ds(0, head)], idx_s.at[pl.ds(tb, head)], idx_sem.at[1]))
    for cp in idx_cps:
        cp.start()
    for cp in idx_cps:
        cp.wait()

    def issue_group(g, slot_set):
        for j in range(GATHER_GROUP):
            t = g * GATHER_GROUP + j
            for r in range(PEER_SEL):
                pltpu.make_async_copy(uv_ref.at[idx_s[t, r]],
                                      bufs[slot_set].at[j, pl.ds(r * SUBLANES, SUBLANES), :],
                                      row_sem.at[slot_set, j]).start(priority=r % 2)

    def wait(slot_set, j):
        dst = bufs[slot_set].at[j]
        pltpu.make_async_copy(dst, dst, row_sem.at[slot_set, j]).wait()

    gf = gf_ref[...]
    lane_t = lax.broadcasted_iota(jnp.int32, (PEER_SEL, tb), 1)

    def group(g, slot_set):
        for j in range(GATHER_GROUP):
            wait(slot_set, j)
        issue_group(g + ahead, (slot_set + ahead) % GATHER_SETS)
        buf = bufs[slot_set]
        for j in range(GATHER_GROUP):
            t = g * GATHER_GROUP + j
            xrow = h2_ref[pl.ds(t, 1), :]
            tiles = [buf[j, pl.ds(s, PEER_SEL, stride=SUBLANES), :] for s in range(SUBLANES)]
            p = _packed_hi(tiles[0]) * xrow[:, :LANES]
            for s in range(1, SUBLANES):
                p = p + _packed_hi(tiles[s]) * xrow[:, s * LANES:(s + 1) * LANES]
            hcol = jnp.sum(p, axis=1, keepdims=True)
            act = 0.5 * hcol * (1.0 + lax.erf(hcol * (2.0 ** -0.5)))
            gcol = jnp.sum(jnp.where(lane_t == t, gate_ref[0], 0.0), axis=1, keepdims=True)
            w = gcol * act
            o = [jnp.sum(w * _packed_lo(tiles[s]), axis=0, keepdims=True) for s in range(SUBLANES)]
            orow = jnp.concatenate(o, axis=1)
            y_ref[pl.ds(t, 1), :] = _rms(x1_ref[pl.ds(t, 1), :] + orow, gf)

    def first_head(_, carry):
        for s in range(ahead):
            issue_group(s, s)
        return carry

    lax.fori_loop(0, (step == 0).astype(jnp.int32), first_head, 0)

    def main(k, carry):
        for s in range(GATHER_SETS):
            group(k * GATHER_SETS + s, s)
        return carry

    lax.fori_loop(0, n_groups // GATHER_SETS, main, 0)

    @pl.when(step == last)
    def _():
        for s in range(ahead):
            for j in range(GATHER_GROUP):
                wait(s, j)


def _experts(eidx, gate, h2, x1, gf, uv):
    n = h2.shape[0]
    tb = TOKEN_BLOCK_EXPERT
    tbr = gate.shape[2]
    assert GATHER_SETS == 4 and tbr % tb == 0 and tb % (GATHER_GROUP * GATHER_SETS) == 0
    per, nb = tbr // tb, n // tb
    head = (GATHER_SETS - 1) * GATHER_GROUP
    row = lambda w: pl.BlockSpec((tb, w), lambda i: (i, 0))
    nxt = pl.BlockSpec((tb, PEER_SEL), lambda i: (jnp.minimum(i + 1, nb - 1), 0))
    sel = pl.BlockSpec((1, PEER_SEL, tb), lambda i: (i // per, 0, i % per))
    slots = pltpu.VMEM((GATHER_GROUP, PEER_SEL * SUBLANES, LANES), jnp.uint32)
    return pl.pallas_call(
        functools.partial(_experts_body, tb=tb),
        grid=(nb,),
        in_specs=[row(PEER_SEL), nxt, sel, row(D_MODEL), row(D_MODEL), pl.BlockSpec(gf.shape, lambda i: (0, 0)),
                  pl.BlockSpec(memory_space=pl.ANY)],
        out_specs=row(D_MODEL),
        out_shape=jax.ShapeDtypeStruct((n, D_MODEL), F32),
        scratch_shapes=[pltpu.SMEM((tb + head, PEER_SEL), jnp.int32), slots, slots, slots, slots,
                        pltpu.SemaphoreType.DMA((2,)),
                        pltpu.SemaphoreType.DMA((GATHER_SETS, GATHER_GROUP))],
        compiler_params=_params("arbitrary"),
    )(eidx, eidx, gate, h2, x1, gf, uv)


def _mixer(x, batch, params, layer, cache=None):
    q, k, v, logf, hgrp = _inproj(x, params["g1"], params["wqkv"], params["wfg"], params["bfg"], params["wh"])
    length = x.shape[0] // batch
    if cache is None:
        k_all = k.reshape(batch, length, ATT_KV_WIDTH)
        v_all = v.reshape(batch, length, ATT_KV_WIDTH)
        logf_all, q_off = logf.reshape(batch, length, ATT_HEADS), 0
        s0t = jnp.zeros((batch, HG_HEADS, HG_DIM, HG_DIM), F32)
    else:
        cache_k, cache_v, cache_logf, state = cache
        past = cache_k.shape[1]
        cat = lambda c, new: jnp.concatenate(
            [c.reshape(batch, past, -1).astype(F32), new.reshape(batch, length, -1)], axis=1)
        k_all, v_all = cat(cache_k, k), cat(cache_v, v)
        logf_all, q_off = cat(cache_logf, logf), past
        s0t = jnp.swapaxes(state.astype(F32), -1, -2)
    f_all = _cumsum(logf_all)
    f_col = f_all[:, q_off:].reshape(batch * length, ATT_HEADS)
    f_row = jnp.swapaxes(f_all, 1, 2)
    att = _attn(q, k_all, v_all, f_col, f_row, batch=batch, q_off=q_off)
    hgo, st = _hgrn(hgrp, params["lb_logits"], params["hg_gain"], s0t, batch=batch, layer=layer)
    return att, hgo, (k, v, logf, jnp.swapaxes(st, -1, -2))


def kernel(x_prompt, x_sample, cache_k, cache_v, cache_logf, state_hgrn, meta_tokens,
           w_in, b_forget, hg_lb_logits, hg_norm_gain, w_out, norm1_gain, norm2_gain,
           peer_w_query, peer_sub_keys, peer_expert_u, peer_expert_v, final_norm_gain):
    depth = w_in.shape[0]
    assert depth == 1, "single-layer trunk: the PEER stage fuses the final norm"
    assert D_MODEL == SUBLANES * LANES, "an expert's packed u|v row must fill exactly one (SUBLANES, LANES) slab"
    bp, seq, _ = x_prompt.shape
    bs, dseq, _ = x_sample.shape
    lp = N_META + seq
    meta = jnp.broadcast_to(meta_tokens.astype(x_prompt.dtype)[None], (bp, N_META, D_MODEL))
    xp = jnp.concatenate([meta, x_prompt], axis=1).reshape(bp * lp, D_MODEL)
    xs = x_sample.reshape(bs * dseq, D_MODEL)

    l = 0
    o_fg = ATT_WIDTH + 2 * ATT_KV_WIDTH
    o_h = o_fg + ATT_HEADS
    row = lambda a: a.reshape(1, -1).astype(F32)
    params = {
        "g1": row(norm1_gain[l]),
        "wqkv": w_in[l][:, :o_fg].astype(BF16),
        "wfg": jnp.pad(w_in[l][:, o_fg:o_h], ((0, 0), (0, LANES - ATT_HEADS))).astype(BF16),
        "bfg": row(b_forget[l]),
        "wh": w_in[l][:, o_h:].astype(BF16),
        "lb_logits": hg_lb_logits.astype(F32),
        "hg_gain": row(hg_norm_gain[l]),
    }
    att_p, hgo_p, sp = _mixer(xp, bp, params, l)
    att_s, hgo_s, ss = _mixer(xs, bs, params, l,
                              cache=(cache_k[l], cache_v[l], cache_logf[l], state_hgrn[l]))

    wo = w_out[l].astype(BF16)
    x1, h2, scores = _proj((xp, att_p, hgo_p), (xs, att_s, hgo_s), wo[:ATT_WIDTH], wo[ATT_WIDTH:],
                           row(norm2_gain[l]), peer_w_query[l].astype(BF16), peer_sub_keys[l].astype(BF16))
    eidx, gate = (a.reshape(-1, PEER_SEL, ROUTE_TOKENS) for a in _route_topk(scores))
    n_exp = peer_expert_u.shape[1]
    uv = _pack_bf16_pair(peer_expert_u[l], peer_expert_v[l]).reshape(n_exp, SUBLANES, LANES)
    eidx_t = jnp.swapaxes(eidx, 1, 2).reshape(-1, PEER_SEL)
    y = _experts(eidx_t, gate, h2, x1, row(final_norm_gain), uv)

    n_p = bp * lp
    y_prompt = y[:n_p].reshape(bp, lp, D_MODEL)[:, N_META:]
    y_sample = y[n_p:].reshape(bs, dseq, D_MODEL)

    def states(s, batch, length):
        k, v, logf, st = s
        return (k.reshape(1, batch, length, ATT_KV_HEADS, ATT_HEAD_DIM),
                v.reshape(1, batch, length, ATT_KV_HEADS, ATT_HEAD_DIM),
                logf.reshape(1, batch, length, ATT_HEADS),
                st.reshape(1, batch, HG_HEADS, HG_DIM, HG_DIM))

    return (y_prompt, y_sample) + states(sp, bp, lp) + states(ss, bs, dseq)
```

```python
import functools

import jax
import jax.numpy as jnp
from jax import lax
from jax.experimental import pallas as pl
from jax.experimental.pallas import tpu as pltpu

F32 = jnp.float32
BF16 = jnp.bfloat16
EPS = 1e-6
N_META = 16

D_MODEL = 1024
ATT_HEADS = 8
ATT_KV_HEADS = 4
ATT_HEAD_DIM = 64
ATT_GROUP = ATT_HEADS // ATT_KV_HEADS
ATT_WIDTH = ATT_HEADS * ATT_HEAD_DIM
ATT_KV_WIDTH = ATT_KV_HEADS * ATT_HEAD_DIM
HG_HEADS = 4
HG_DIM = 128
HG_WIDTH = HG_HEADS * HG_DIM
HG_PER_STEP = 4
PEER_HEADS = 8
PEER_N_KEYS = 128
PEER_TOPK = 16
PEER_HALF = 128
PEER_SEL = PEER_HEADS * PEER_TOPK

LANES = 128
SUBLANES = 8
VMEM_LIMIT = 48 * 1024 * 1024

TOKEN_BLOCK_PROJ = 512
ROUTE_TOKENS = SUBLANES * LANES
TOKEN_BLOCK_EXPERT = 256
GATHER_GROUP = 8
GATHER_SETS = 4


def _params(*sem):
    return pltpu.CompilerParams(dimension_semantics=sem, vmem_limit_bytes=VMEM_LIMIT)


def _largest_block(n, cap, mult):
    best = None
    for d in range(mult, cap + 1, mult):
        if n % d == 0:
            best = d
    assert best is not None, (n, cap, mult)
    return best


def _split3(x):
    hi = x.astype(BF16)
    r1 = x - hi.astype(F32)
    mid = r1.astype(BF16)
    lo = (r1 - mid.astype(F32)).astype(BF16)
    return hi, mid, lo


def _tri_cumsum(tri, x):
    hi, mid, lo = _split3(x)
    d = lambda a: jnp.dot(tri, a, preferred_element_type=F32)
    return (d(lo) + d(mid)) + d(hi)


def _lower_tri(c):
    r = lax.broadcasted_iota(jnp.int32, (c, c), 0)
    s = lax.broadcasted_iota(jnp.int32, (c, c), 1)
    return jnp.where(s <= r, 1.0, 0.0).astype(BF16)


def _rms(x, gain):
    return x * lax.rsqrt(jnp.mean(x * x, axis=-1, keepdims=True) + EPS) * gain


def _inproj_body(x_ref, g_ref, wqkv_ref, wfg_ref, bfg_ref, wh_ref,
                 q_ref, k_ref, v_ref, logf_ref, hgrp_ref):
    hb = _rms(x_ref[...], g_ref[...]).astype(BF16)
    qkv = jnp.dot(hb, wqkv_ref[...], preferred_element_type=F32)
    q_ref[...] = (qkv[:, :ATT_WIDTH] * (ATT_HEAD_DIM ** -0.5)).astype(BF16)
    k_ref[...] = qkv[:, ATT_WIDTH:ATT_WIDTH + ATT_KV_WIDTH]
    v_ref[...] = qkv[:, ATT_WIDTH + ATT_KV_WIDTH:]
    fg = jnp.dot(hb, wfg_ref[...], preferred_element_type=F32)[:, :ATT_HEADS] + bfg_ref[...]
    logf_ref[...] = jnp.minimum(fg, 0.0) - jnp.log1p(jnp.exp(-jnp.abs(fg)))
    hgrp_ref[...] = jnp.dot(hb, wh_ref[...], preferred_element_type=F32)


def _inproj(x, gain, wqkv, wfg, bfg, wh):
    n = x.shape[0]
    tb = _largest_block(n, TOKEN_BLOCK_PROJ, 8)
    row = lambda w: pl.BlockSpec((tb, w), lambda i: (i, 0))
    full = lambda a: pl.BlockSpec(a.shape, lambda i: (0,) * a.ndim)
    return pl.pallas_call(
        _inproj_body,
        grid=(n // tb,),
        in_specs=[row(D_MODEL), full(gain), full(wqkv), full(wfg), full(bfg), full(wh)],
        out_specs=[row(ATT_WIDTH), row(ATT_KV_WIDTH), row(ATT_KV_WIDTH), row(ATT_HEADS), row(4 * HG_WIDTH)],
        out_shape=[jax.ShapeDtypeStruct((n, ATT_WIDTH), BF16),
                   jax.ShapeDtypeStruct((n, ATT_KV_WIDTH), F32),
                   jax.ShapeDtypeStruct((n, ATT_KV_WIDTH), F32),
                   jax.ShapeDtypeStruct((n, ATT_HEADS), F32),
                   jax.ShapeDtypeStruct((n, 4 * HG_WIDTH), F32)],
        compiler_params=_params("parallel"),
    )(x, gain, wqkv, wfg, bfg, wh)


def _cumsum_body(x_ref, o_ref, *, cb):
    length, width = x_ref.shape[1], x_ref.shape[2]
    tri = _lower_tri(cb)
    carry = jnp.zeros((1, width), F32)
    for j in range(length // cb):
        f = _tri_cumsum(tri, x_ref[0, j * cb:(j + 1) * cb, :]) + carry
        o_ref[0, j * cb:(j + 1) * cb, :] = f
        carry = f[cb - 1:cb, :]


def _cumsum(logf):
    b, length, h = logf.shape
    cb = _largest_block(length, 512, 8)
    spec = pl.BlockSpec((1, length, h), lambda i: (i, 0, 0))
    return pl.pallas_call(
        functools.partial(_cumsum_body, cb=cb),
        grid=(b,),
        in_specs=[spec],
        out_specs=spec,
        out_shape=jax.ShapeDtypeStruct(logf.shape, F32),
        compiler_params=_params("parallel"),
    )(logf)


def _attn_body(q_ref, k_ref, v_ref, fc_ref, fr_ref, o_ref, *, q_start):
    tq, kmax = q_ref.shape[0], k_ref.shape[1]
    qpos = q_start + lax.broadcasted_iota(jnp.int32, (tq, kmax), 0)
    kpos = lax.broadcasted_iota(jnp.int32, (tq, kmax), 1)
    mask = kpos <= qpos
    kb = k_ref[0].astype(BF16)
    vb = v_ref[0].astype(BF16)
    q = q_ref[...]
    fc = fc_ref[...]
    fr = fr_ref[0]
    outs = []
    for h in range(ATT_HEADS):
        g = h // ATT_GROUP
        hs = slice(h * ATT_HEAD_DIM, (h + 1) * ATT_HEAD_DIM)
        gs = slice(g * ATT_HEAD_DIM, (g + 1) * ATT_HEAD_DIM)
        s = lax.dot_general(q[:, hs], kb[:, gs], (((1,), (1,)), ((), ())), preferred_element_type=F32)
        s = s + fc[:, h:h + 1] - fr[h:h + 1, :]
        s = jnp.where(mask, s, -jnp.inf)
        p = jnp.exp(s - jnp.max(s, axis=-1, keepdims=True))
        l = jnp.sum(p, axis=-1, keepdims=True)
        o = jnp.dot(p.astype(BF16), vb[:, gs], preferred_element_type=F32)
        outs.append(o / l)
    o_ref[0] = jnp.concatenate(outs, axis=-1).astype(BF16)


def _attn(q, k, v, f_col, f_row, *, batch, q_off):
    t_q = q.shape[0] // batch
    lk = k.shape[1]
    tq = _largest_block(t_q, 384, 8)
    nq = t_q // tq
    outs = []
    for c in range(nq):
        q_start = q_off + c * tq
        kmax = min(lk, -(-(q_start + tq) // LANES) * LANES)
        qspec = lambda w, c=c: pl.BlockSpec((tq, w), lambda b: (b * nq + c, 0))
        kspec = pl.BlockSpec((1, kmax, ATT_KV_WIDTH), lambda b: (b, 0, 0))
        outs.append(pl.pallas_call(
            functools.partial(_attn_body, q_start=q_start),
            grid=(batch,),
            in_specs=[qspec(ATT_WIDTH), kspec, kspec, qspec(ATT_HEADS),
                      pl.BlockSpec((1, ATT_HEADS, kmax), lambda b: (b, 0, 0))],
            out_specs=pl.BlockSpec((1, tq, ATT_WIDTH), lambda b: (b, 0, 0)),
            out_shape=jax.ShapeDtypeStruct((batch, tq, ATT_WIDTH), BF16),
            compiler_params=_params("parallel"),
        )(q, k, v, f_col, f_row))
    return jnp.concatenate(outs, axis=1).reshape(batch * t_q, ATT_WIDTH)


def _hgrn_body(hq_ref, hf_ref, hi_ref, hg_ref, lbl_ref, gain_ref, s0_ref, o_ref, st_ref,
               oi_s, *, chunk, layer):
    length = hq_ref.shape[0]
    lbl = lbl_ref[...]
    e = jnp.exp(lbl - jnp.max(lbl, axis=0, keepdims=True))
    lb_all = jnp.sum(e[:layer + 1, :], axis=0, keepdims=True) / jnp.sum(e, axis=0, keepdims=True)
    gain_all = gain_ref[...]
    tri = _lower_tri(chunk)
    rows = lax.broadcasted_iota(jnp.int32, (chunk, 1), 0)
    st_ref[...] = s0_ref[...]

    def chunk_step(c, carry):
        r0 = pl.multiple_of(c * chunk, 16)
        for hh in range(HG_PER_STEP):
            ls = slice(hh * HG_DIM, (hh + 1) * HG_DIM)
            lb = lb_all[:, ls]
            st = st_ref[0, hh]
            f = lb + (1.0 - lb) * jax.nn.sigmoid(hf_ref[pl.ds(r0, chunk), ls])
            kk = 1.0 - f
            hq = hq_ref[pl.ds(r0, chunk), ls]
            qq = hq * jax.nn.sigmoid(hq)
            iv = hi_ref[pl.ds(r0, chunk), ls]
            g = _tri_cumsum(tri, jnp.log(f))
            o_inter = lax.dot_general((qq * jnp.exp(g)).astype(BF16), st.astype(BF16),
                                      (((1,), (1,)), ((), ())), preferred_element_type=F32)
            for t in range(chunk):
                n = (t // SUBLANES + 1) * SUBLANES
                d = jnp.where(rows[:n] <= t, g[t:t + 1, :] - g[:n, :], -jnp.inf)
                a = jnp.exp(d) * kk[:n, :] * qq[t:t + 1, :]
                sc = jnp.sum(a, axis=1, keepdims=True)
                oi_s[hh, t:t + 1, :] = jnp.sum(sc * iv[:n, :], axis=0, keepdims=True)
            o = o_inter + oi_s[hh]
            on = _rms(o, gain_all[:, ls])
            hg = hg_ref[pl.ds(r0, chunk), ls]
            o_ref[pl.ds(r0, chunk), ls] = (on * (hg * jax.nn.sigmoid(hg))).astype(BF16)
            g_end = g[chunk - 1:chunk, :]
            kd = kk * jnp.exp(g_end - g)
            upd = lax.dot_general(iv.astype(BF16), kd.astype(BF16),
                                  (((0,), (0,)), ((), ())), preferred_element_type=F32)
            st_ref[0, hh] = st * jnp.exp(g_end) + upd
        return carry

    lax.fori_loop(0, length // chunk, chunk_step, 0)


def _hgrn(hgrp, lb_logits, gain, s0t, *, batch, layer):
    length = hgrp.shape[0] // batch
    chunk = _largest_block(length, 64, 16)
    width = HG_PER_STEP * HG_DIM
    steps = HG_HEADS // HG_PER_STEP
    col = lambda off: pl.BlockSpec((length, width), lambda b, h: (b, off * steps + h))
    par = lambda a: pl.BlockSpec((a.shape[0], width), lambda b, h: (0, h))
    st = pl.BlockSpec((1, HG_PER_STEP, HG_DIM, HG_DIM), lambda b, h: (b, h, 0, 0))
    buf = pltpu.VMEM((HG_PER_STEP, chunk, HG_DIM), F32)
    return pl.pallas_call(
        functools.partial(_hgrn_body, chunk=chunk, layer=layer),
        grid=(batch, steps),
        in_specs=[col(0), col(1), col(2), col(3), par(lb_logits), par(gain), st],
        out_specs=[pl.BlockSpec((length, width), lambda b, h: (b, h)), st],
        out_shape=[jax.ShapeDtypeStruct((batch * length, HG_WIDTH), BF16),
                   jax.ShapeDtypeStruct(s0t.shape, F32)],
        scratch_shapes=[buf],
        compiler_params=_params("parallel", "parallel"),
    )(hgrp, hgrp, hgrp, hgrp, lb_logits, gain, s0t)


def _proj_body(xa_ref, atta_ref, hgoa_ref, xb_ref, attb_ref, hgob_ref, woa_ref, wob_ref, g2_ref, wq_ref, keys_ref,
               x1_ref, h2_ref, s_ref, *, steps_a):
    first = pl.program_id(0) < steps_a
    pick = lambda a_ref, b_ref: jnp.where(first, a_ref[...], b_ref[...])
    x1 = (pick(xa_ref, xb_ref) + jnp.dot(pick(atta_ref, attb_ref), woa_ref[...], preferred_element_type=F32)
          + jnp.dot(pick(hgoa_ref, hgob_ref), wob_ref[...], preferred_element_type=F32))
    x1_ref[...] = x1
    h2 = _rms(x1, g2_ref[...])
    h2_ref[...] = h2
    qp = jnp.dot(h2.astype(BF16), wq_ref[...], preferred_element_type=F32)
    for h in range(PEER_HEADS):
        for c in range(2):
            col = (2 * h + c) * PEER_HALF
            s_ref[2 * h + c] = lax.dot_general(keys_ref[h, c], qp[:, col:col + PEER_HALF].astype(BF16),
                                               (((1,), (1,)), ((), ())), preferred_element_type=F32)


def _proj(group_a, group_b, woa, wob, g2, wq, keys):
    na, nb = group_a[0].shape[0], group_b[0].shape[0]
    n = na + nb
    tb = min(_largest_block(na, TOKEN_BLOCK_PROJ, LANES), _largest_block(nb, TOKEN_BLOCK_PROJ, LANES))
    assert na % tb == 0 and nb % tb == 0
    steps_a = na // tb
    row = lambda w: pl.BlockSpec((tb, w), lambda i: (i, 0))
    row_a = lambda w: pl.BlockSpec((tb, w), lambda i: (jnp.minimum(i, steps_a - 1), 0))
    row_b = lambda w: pl.BlockSpec((tb, w), lambda i: (jnp.maximum(i - steps_a, 0), 0))
    full = lambda a: pl.BlockSpec(a.shape, lambda i: (0,) * a.ndim)
    widths = (D_MODEL, ATT_WIDTH, HG_WIDTH)
    return pl.pallas_call(
        functools.partial(_proj_body, steps_a=steps_a),
        grid=(n // tb,),
        in_specs=[row_a(w) for w in widths] + [row_b(w) for w in widths]
                 + [full(woa), full(wob), full(g2), full(wq), full(keys)],
        out_specs=[row(D_MODEL), row(D_MODEL), pl.BlockSpec((2 * PEER_HEADS, PEER_N_KEYS, tb), lambda i: (0, 0, i))],
        out_shape=[jax.ShapeDtypeStruct((n, D_MODEL), F32),
                   jax.ShapeDtypeStruct((n, D_MODEL), F32),
                   jax.ShapeDtypeStruct((2 * PEER_HEADS, PEER_N_KEYS, n), F32)],
        compiler_params=_params("arbitrary"),
    )(*group_a, *group_b, woa, wob, g2, wq, keys)


def _stream_topk(w_ref, n, k_out, emit, payload_ref=None):
    n_acc = 4
    shape = w_ref.shape[1:]

    def step(j, prev):
        accs = [(jnp.full(shape, -jnp.inf, F32), jnp.full(shape, n, jnp.int32), jnp.full(shape, -1, jnp.int32))
                for _ in range(n_acc)]
        for k in range(n):
            v = jnp.where(prev == k, -jnp.inf, w_ref[k])
            w_ref[k] = v
            m, pos, pay = accs[k % n_acc]
            new = v > m
            pay = pay if payload_ref is None else jnp.where(new, payload_ref[k], pay)
            accs[k % n_acc] = (jnp.where(new, v, m), jnp.where(new, k, pos), pay)
        while len(accs) > 1:
            a, b = accs[0], accs[1]
            take = (b[0] > a[0]) | ((b[0] == a[0]) & (b[1] < a[1]))
            accs = accs[2:] + [tuple(jnp.where(take, y, x) for x, y in zip(a, b))]
        m, pos, pay = accs[0]
        emit(j, m, pos if payload_ref is None else pay)
        return pos

    lax.fori_loop(0, k_out, step, jnp.full(shape, n, jnp.int32))


def _pair_list(k):
    return [(i, j) for i in range(k) for j in range(k // (i + 1))]


def _topk_body(s_ref, eidx_ref, gate_ref, stage_s, w_s, va_s, ia_s, vb_s, ib_s, cand_s, cid_s, ts_s):
    tiles = ROUTE_TOKENS // LANES
    for c, (v_s, i_s) in enumerate(((va_s, ia_s), (vb_s, ib_s))):
        for lt in range(tiles):
            stage_s[lt * PEER_N_KEYS:(lt + 1) * PEER_N_KEYS, :] = s_ref[c, :, lt * LANES:(lt + 1) * LANES]
        for k in range(PEER_N_KEYS):
            w_s[k] = stage_s[pl.ds(k, tiles, stride=PEER_N_KEYS), :]

        def emit(j, m, pos, v_s=v_s, i_s=i_s):
            v_s[j] = m
            i_s[j] = pos

        _stream_topk(w_s, PEER_N_KEYS, PEER_TOPK, emit)
    pairs = _pair_list(PEER_TOPK)
    for p, (i, j) in enumerate(pairs):
        cand_s[p] = va_s[i] + vb_s[j]
        cid_s[p] = ia_s[i] * PEER_N_KEYS + ib_s[j]

    def emit_final(j, m, pay):
        ts_s[j] = m
        eidx_ref[0, j] = pay

    _stream_topk(cand_s, len(pairs), PEER_TOPK, emit_final, payload_ref=cid_s)
    ex = [jnp.exp(ts_s[j] - ts_s[0]) for j in range(PEER_TOPK)]
    tot = ex[0]
    for e in ex[1:]:
        tot = tot + e
    for j in range(PEER_TOPK):
        gate_ref[0, j] = ex[j] / tot


def _route_topk(scores):
    n = scores.shape[2]
    assert n % ROUTE_TOKENS == 0 and ROUTE_TOKENS // LANES == SUBLANES
    nb = n // ROUTE_TOKENS
    n_pairs = len(_pair_list(PEER_TOPK))
    tile = lambda rows, dt: pltpu.VMEM((rows, SUBLANES, LANES), dt)
    sel = pl.BlockSpec((1, PEER_TOPK, SUBLANES, LANES), lambda i, h: (i, h, 0, 0))
    return pl.pallas_call(
        _topk_body,
        grid=(nb, PEER_HEADS),
        in_specs=[pl.BlockSpec((2, PEER_N_KEYS, ROUTE_TOKENS), lambda i, h: (h, 0, i))],
        out_specs=[sel, sel],
        out_shape=[jax.ShapeDtypeStruct((nb, PEER_SEL, SUBLANES, LANES), jnp.int32),
                   jax.ShapeDtypeStruct((nb, PEER_SEL, SUBLANES, LANES), F32)],
        scratch_shapes=[pltpu.VMEM((SUBLANES * PEER_N_KEYS, LANES), F32), tile(PEER_N_KEYS, F32),
                        tile(PEER_TOPK, F32), tile(PEER_TOPK, jnp.int32),
                        tile(PEER_TOPK, F32), tile(PEER_TOPK, jnp.int32),
                        tile(n_pairs, F32), tile(n_pairs, jnp.int32), tile(PEER_TOPK, F32)],
        compiler_params=_params("parallel", "parallel"),
    )(scores)


def _pack_bf16_pair(hi, lo):
    bits = lambda a: lax.bitcast_convert_type(a.astype(BF16), jnp.uint16).astype(jnp.uint32)
    return (bits(hi) << 16) | bits(lo)


def _packed_hi(word):
    return lax.bitcast_convert_type(word & jnp.uint32(0xFFFF0000), F32)


def _packed_lo(word):
    return lax.bitcast_convert_type(word << 16, F32)


def _experts_body(eidx_ref, enext_ref, gate_ref, h2_ref, x1_ref, gf_ref, uv_ref, ya_ref, yb_ref,
                  idx_s, buf0, buf1, buf2, buf3, idx_sem, row_sem, *, tb, steps_a):
    bufs = (buf0, buf1, buf2, buf3)
    step, last = pl.program_id(0), pl.num_programs(0) - 1
    n_groups = tb // GATHER_GROUP
    ahead = GATHER_SETS - 1
    head = ahead * GATHER_GROUP
    idx_cps = (pltpu.make_async_copy(eidx_ref, idx_s.at[pl.ds(0, tb)], idx_sem.at[0]),
               pltpu.make_async_copy(enext_ref.at[pl.ds(0, head)], idx_s.at[pl.ds(tb, head)], idx_sem.at[1]))
    for cp in idx_cps:
        cp.start()
    for cp in idx_cps:
        cp.wait()

    def issue_group(g, slot_set):
        for j in range(GATHER_GROUP):
            t = g * GATHER_GROUP + j
            for r in range(PEER_SEL):
                pltpu.make_async_copy(uv_ref.at[idx_s[t, r]],
                                      bufs[slot_set].at[j, pl.ds(r * SUBLANES, SUBLANES), :],
                                      row_sem.at[slot_set, j]).start(priority=r % 2)

    def wait(slot_set, j):
        dst = bufs[slot_set].at[j]
        pltpu.make_async_copy(dst, dst, row_sem.at[slot_set, j]).wait()

    gf = gf_ref[...]
    lane_t = lax.broadcasted_iota(jnp.int32, (PEER_SEL, tb), 1)
    in_a = jnp.broadcast_to(step < steps_a, (1, D_MODEL))

    def group(g, slot_set):
        for j in range(GATHER_GROUP):
            wait(slot_set, j)
        issue_group(g + ahead, (slot_set + ahead) % GATHER_SETS)
        buf = bufs[slot_set]
        for j in range(GATHER_GROUP):
            t = g * GATHER_GROUP + j
            xrow = h2_ref[pl.ds(t, 1), :]
            tiles = [buf[j, pl.ds(s, PEER_SEL, stride=SUBLANES), :] for s in range(SUBLANES)]
            p = _packed_hi(tiles[0]) * xrow[:, :LANES]
            for s in range(1, SUBLANES):
                p = p + _packed_hi(tiles[s]) * xrow[:, s * LANES:(s + 1) * LANES]
            hcol = jnp.sum(p, axis=1, keepdims=True)
            act = 0.5 * hcol * (1.0 + lax.erf(hcol * (2.0 ** -0.5)))
            gcol = jnp.sum(jnp.where(lane_t == t, gate_ref[0], 0.0), axis=1, keepdims=True)
            w = gcol * act
            o = [jnp.sum(w * _packed_lo(tiles[s]), axis=0, keepdims=True) for s in range(SUBLANES)]
            orow = jnp.concatenate(o, axis=1)
            yrow = _rms(x1_ref[pl.ds(t, 1), :] + orow, gf)
            pltpu.store(ya_ref.at[pl.ds(t, 1), :], yrow, mask=in_a)
            pltpu.store(yb_ref.at[pl.ds(t, 1), :], yrow, mask=~in_a)

    def first_head(_, carry):
        for s in range(ahead):
            issue_group(s, s)
        return carry

    lax.fori_loop(0, (step == 0).astype(jnp.int32), first_head, 0)

    def main(k, carry):
        for s in range(GATHER_SETS):
            group(k * GATHER_SETS + s, s)
        return carry

    lax.fori_loop(0, n_groups // GATHER_SETS, main, 0)

    @pl.when(step == last)
    def _():
        for s in range(ahead):
            for j in range(GATHER_GROUP):
                wait(s, j)


def _experts(eidx, gate, h2, x1, gf, uv, n_a):
    n = h2.shape[0]
    tb = TOKEN_BLOCK_EXPERT
    tbr = gate.shape[2]
    assert GATHER_SETS == 4 and tbr % tb == 0 and tb % (GATHER_GROUP * GATHER_SETS) == 0
    assert n_a % tb == 0 and 0 < n_a < n
    per, nb, steps_a = tbr // tb, n // tb, n_a // tb
    head = (GATHER_SETS - 1) * GATHER_GROUP
    row = lambda w: pl.BlockSpec((tb, w), lambda i: (i, 0))
    nxt = pl.BlockSpec((tb, PEER_SEL), lambda i: (jnp.minimum(i + 1, nb - 1), 0))
    sel = pl.BlockSpec((1, PEER_SEL, tb), lambda i: (i // per, 0, i % per))
    slots = pltpu.VMEM((GATHER_GROUP, PEER_SEL * SUBLANES, LANES), jnp.uint32)
    return pl.pallas_call(
        functools.partial(_experts_body, tb=tb, steps_a=steps_a),
        grid=(nb,),
        in_specs=[row(PEER_SEL), nxt, sel, row(D_MODEL), row(D_MODEL), pl.BlockSpec(gf.shape, lambda i: (0, 0)),
                  pl.BlockSpec(memory_space=pl.ANY)],
        out_specs=[pl.BlockSpec((tb, D_MODEL), lambda i: (jnp.minimum(i, steps_a - 1), 0)),
                   pl.BlockSpec((tb, D_MODEL), lambda i: (jnp.maximum(i - steps_a, 0), 0))],
        out_shape=[jax.ShapeDtypeStruct((n_a, D_MODEL), F32), jax.ShapeDtypeStruct((n - n_a, D_MODEL), F32)],
        scratch_shapes=[pltpu.SMEM((tb + head, PEER_SEL), jnp.int32), slots, slots, slots, slots,
                        pltpu.SemaphoreType.DMA((2,)),
                        pltpu.SemaphoreType.DMA((GATHER_SETS, GATHER_GROUP))],
        compiler_params=_params("arbitrary"),
    )(eidx, eidx, gate, h2, x1, gf, uv)


def _mixer(x, batch, params, layer, cache=None):
    q, k, v, logf, hgrp = _inproj(x, params["g1"], params["wqkv"], params["wfg"], params["bfg"], params["wh"])
    length = x.shape[0] // batch
    if cache is None:
        k_all = k.reshape(batch, length, ATT_KV_WIDTH)
        v_all = v.reshape(batch, length, ATT_KV_WIDTH)
        logf_all, q_off = logf.reshape(batch, length, ATT_HEADS), 0
        s0t = jnp.zeros((batch, HG_HEADS, HG_DIM, HG_DIM), F32)
    else:
        cache_k, cache_v, cache_logf, state = cache
        past = cache_k.shape[1]
        cat = lambda c, new: jnp.concatenate(
            [c.reshape(batch, past, -1).astype(F32), new.reshape(batch, length, -1)], axis=1)
        k_all, v_all = cat(cache_k, k), cat(cache_v, v)
        logf_all, q_off = cat(cache_logf, logf), past
        s0t = jnp.swapaxes(state.astype(F32), -1, -2)
    f_all = _cumsum(logf_all)
    f_col = f_all[:, q_off:].reshape(batch * length, ATT_HEADS)
    f_row = jnp.swapaxes(f_all, 1, 2)
    att = _attn(q, k_all, v_all, f_col, f_row, batch=batch, q_off=q_off)
    hgo, st = _hgrn(hgrp, params["lb_logits"], params["hg_gain"], s0t, batch=batch, layer=layer)
    return att, hgo, (k, v, logf, jnp.swapaxes(st, -1, -2))


def kernel(x_prompt, x_sample, cache_k, cache_v, cache_logf, state_hgrn, meta_tokens,
           w_in, b_forget, hg_lb_logits, hg_norm_gain, w_out, norm1_gain, norm2_gain,
           peer_w_query, peer_sub_keys, peer_expert_u, peer_expert_v, final_norm_gain):
    depth = w_in.shape[0]
    assert depth == 1, "single-layer trunk: the PEER stage fuses the final norm"
    assert D_MODEL == SUBLANES * LANES, "an expert's packed u|v row must fill exactly one (SUBLANES, LANES) slab"
    bp, seq, _ = x_prompt.shape
    bs, dseq, _ = x_sample.shape
    lp = N_META + seq
    meta = jnp.broadcast_to(meta_tokens.astype(x_prompt.dtype)[None], (bp, N_META, D_MODEL))
    xp = jnp.concatenate([meta, x_prompt], axis=1).reshape(bp * lp, D_MODEL)
    xs = x_sample.reshape(bs * dseq, D_MODEL)

    l = 0
    o_fg = ATT_WIDTH + 2 * ATT_KV_WIDTH
    o_h = o_fg + ATT_HEADS
    row = lambda a: a.reshape(1, -1).astype(F32)
    params = {
        "g1": row(norm1_gain[l]),
        "wqkv": w_in[l][:, :o_fg].astype(BF16),
        "wfg": jnp.pad(w_in[l][:, o_fg:o_h], ((0, 0), (0, LANES - ATT_HEADS))).astype(BF16),
        "bfg": row(b_forget[l]),
        "wh": w_in[l][:, o_h:].astype(BF16),
        "lb_logits": hg_lb_logits.astype(F32),
        "hg_gain": row(hg_norm_gain[l]),
    }
    att_p, hgo_p, sp = _mixer(xp, bp, params, l)
    att_s, hgo_s, ss = _mixer(xs, bs, params, l,
                              cache=(cache_k[l], cache_v[l], cache_logf[l], state_hgrn[l]))

    wo = w_out[l].astype(BF16)
    x1, h2, scores = _proj((xp, att_p, hgo_p), (xs, att_s, hgo_s), wo[:ATT_WIDTH], wo[ATT_WIDTH:],
                           row(norm2_gain[l]), peer_w_query[l].astype(BF16), peer_sub_keys[l].astype(BF16))
    eidx, gate = (a.reshape(-1, PEER_SEL, ROUTE_TOKENS) for a in _route_topk(scores))
    n_exp = peer_expert_u.shape[1]
    uv = _pack_bf16_pair(peer_expert_u[l], peer_expert_v[l]).reshape(n_exp, SUBLANES, LANES)
    eidx_t = jnp.swapaxes(eidx, 1, 2).reshape(-1, PEER_SEL)
    y_p, y_s = _experts(eidx_t, gate, h2, x1, row(final_norm_gain), uv, bp * lp)
    y_prompt = y_p.reshape(bp, lp, D_MODEL)[:, N_META:]
    y_sample = y_s.reshape(bs, dseq, D_MODEL)

    def states(s, batch, length):
        k, v, logf, st = s
        return (k.reshape(1, batch, length, ATT_KV_HEADS, ATT_HEAD_DIM),
                v.reshape(1, batch, length, ATT_KV_HEADS, ATT_HEAD_DIM),
                logf.reshape(1, batch, length, ATT_HEADS),
                st.reshape(1, batch, HG_HEADS, HG_DIM, HG_DIM))

    return (y_prompt, y_sample) + states(sp, bp, lp) + states(ss, bs, dseq)
```

```python
import functools

import jax
import jax.numpy as jnp
from jax import lax
from jax.experimental import pallas as pl
from jax.experimental.pallas import tpu as pltpu

F32 = jnp.float32
BF16 = jnp.bfloat16
EPS = 1e-6
N_META = 16

D_MODEL = 1024
ATT_HEADS = 8
ATT_KV_HEADS = 4
ATT_HEAD_DIM = 64
ATT_GROUP = ATT_HEADS // ATT_KV_HEADS
ATT_WIDTH = ATT_HEADS * ATT_HEAD_DIM
ATT_KV_WIDTH = ATT_KV_HEADS * ATT_HEAD_DIM
HG_HEADS = 4
HG_DIM = 128
HG_WIDTH = HG_HEADS * HG_DIM
HG_PER_STEP = 4
PEER_HEADS = 8
PEER_N_KEYS = 128
PEER_TOPK = 16
PEER_HALF = 128
PEER_SEL = PEER_HEADS * PEER_TOPK

LANES = 128
SUBLANES = 8
VMEM_LIMIT = 48 * 1024 * 1024

TOKEN_BLOCK_PROJ = 512
ROUTE_TOKENS = SUBLANES * LANES
TOKEN_BLOCK_EXPERT = 256
GATHER_GROUP = 8
GATHER_SETS = 4


def _params(*sem):
    return pltpu.CompilerParams(dimension_semantics=sem, vmem_limit_bytes=VMEM_LIMIT)


def _largest_block(n, cap, mult):
    best = None
    for d in range(mult, cap + 1, mult):
        if n % d == 0:
            best = d
    assert best is not None, (n, cap, mult)
    return best


def _split3(x):
    hi = x.astype(BF16)
    r1 = x - hi.astype(F32)
    mid = r1.astype(BF16)
    lo = (r1 - mid.astype(F32)).astype(BF16)
    return hi, mid, lo


def _tri_cumsum(tri, x):
    hi, mid, lo = _split3(x)
    d = lambda a: jnp.dot(tri, a, preferred_element_type=F32)
    return (d(lo) + d(mid)) + d(hi)


def _lower_tri(c):
    r = lax.broadcasted_iota(jnp.int32, (c, c), 0)
    s = lax.broadcasted_iota(jnp.int32, (c, c), 1)
    return jnp.where(s <= r, 1.0, 0.0).astype(BF16)


def _rms(x, gain):
    return x * lax.rsqrt(jnp.mean(x * x, axis=-1, keepdims=True) + EPS) * gain


def _inproj_body(x_ref, g_ref, wqkv_ref, wfg_ref, bfg_ref, wh_ref,
                 q_ref, k_ref, v_ref, logf_ref, hgrp_ref):
    hb = _rms(x_ref[...], g_ref[...]).astype(BF16)
    qkv = jnp.dot(hb, wqkv_ref[...], preferred_element_type=F32)
    q_ref[...] = (qkv[:, :ATT_WIDTH] * (ATT_HEAD_DIM ** -0.5)).astype(BF16)
    k_ref[...] = qkv[:, ATT_WIDTH:ATT_WIDTH + ATT_KV_WIDTH]
    v_ref[...] = qkv[:, ATT_WIDTH + ATT_KV_WIDTH:]
    fg = jnp.dot(hb, wfg_ref[...], preferred_element_type=F32)[:, :ATT_HEADS] + bfg_ref[...]
    logf_ref[...] = jnp.minimum(fg, 0.0) - jnp.log1p(jnp.exp(-jnp.abs(fg)))
    hgrp_ref[...] = jnp.dot(hb, wh_ref[...], preferred_element_type=F32)


def _inproj(x, gain, wqkv, wfg, bfg, wh):
    n = x.shape[0]
    tb = _largest_block(n, TOKEN_BLOCK_PROJ, SUBLANES)
    row = lambda w: pl.BlockSpec((tb, w), lambda i: (i, 0))
    full = lambda a: pl.BlockSpec(a.shape, lambda i: (0,) * a.ndim)
    return pl.pallas_call(
        _inproj_body,
        grid=(n // tb,),
        in_specs=[row(D_MODEL), full(gain), full(wqkv), full(wfg), full(bfg), full(wh)],
        out_specs=[row(ATT_WIDTH), row(ATT_KV_WIDTH), row(ATT_KV_WIDTH), row(ATT_HEADS), row(4 * HG_WIDTH)],
        out_shape=[jax.ShapeDtypeStruct((n, ATT_WIDTH), BF16),
                   jax.ShapeDtypeStruct((n, ATT_KV_WIDTH), F32),
                   jax.ShapeDtypeStruct((n, ATT_KV_WIDTH), F32),
                   jax.ShapeDtypeStruct((n, ATT_HEADS), F32),
                   jax.ShapeDtypeStruct((n, 4 * HG_WIDTH), F32)],
        compiler_params=_params("parallel"),
    )(x, gain, wqkv, wfg, bfg, wh)


def _cumsum_body(x_ref, o_ref, *, cb):
    length, width = x_ref.shape[1], x_ref.shape[2]
    tri = _lower_tri(cb)
    carry = jnp.zeros((1, width), F32)
    for j in range(length // cb):
        f = _tri_cumsum(tri, x_ref[0, j * cb:(j + 1) * cb, :]) + carry
        o_ref[0, j * cb:(j + 1) * cb, :] = f
        carry = f[cb - 1:cb, :]


def _cumsum(logf):
    b, length, h = logf.shape
    cb = _largest_block(length, 512, 8)
    spec = pl.BlockSpec((1, length, h), lambda i: (i, 0, 0))
    return pl.pallas_call(
        functools.partial(_cumsum_body, cb=cb),
        grid=(b,),
        in_specs=[spec],
        out_specs=spec,
        out_shape=jax.ShapeDtypeStruct(logf.shape, F32),
        compiler_params=_params("parallel"),
    )(logf)


def _attn_body(q_ref, k_ref, v_ref, fc_ref, fr_ref, o_ref, *, q_start):
    tq, kmax = q_ref.shape[0], k_ref.shape[1]
    qpos = q_start + lax.broadcasted_iota(jnp.int32, (tq, kmax), 0)
    kpos = lax.broadcasted_iota(jnp.int32, (tq, kmax), 1)
    mask = kpos <= qpos
    kb = k_ref[0].astype(BF16)
    vb = v_ref[0].astype(BF16)
    q = q_ref[...]
    fc = fc_ref[...]
    fr = fr_ref[0]
    outs = []
    for h in range(ATT_HEADS):
        g = h // ATT_GROUP
        hs = slice(h * ATT_HEAD_DIM, (h + 1) * ATT_HEAD_DIM)
        gs = slice(g * ATT_HEAD_DIM, (g + 1) * ATT_HEAD_DIM)
        s = lax.dot_general(q[:, hs], kb[:, gs], (((1,), (1,)), ((), ())), preferred_element_type=F32)
        s = s + fc[:, h:h + 1] - fr[h:h + 1, :]
        s = jnp.where(mask, s, -jnp.inf)
        p = jnp.exp(s - jnp.max(s, axis=-1, keepdims=True))
        l = jnp.sum(p, axis=-1, keepdims=True)
        o = jnp.dot(p.astype(BF16), vb[:, gs], preferred_element_type=F32)
        outs.append(o / l)
    o_ref[0] = jnp.concatenate(outs, axis=-1).astype(BF16)


def _attn(q, k, v, f_col, f_row, *, batch, q_off):
    t_q = q.shape[0] // batch
    lk = k.shape[1]
    tq = _largest_block(t_q, 384, 8)
    nq = t_q // tq
    outs = []
    for c in range(nq):
        q_start = q_off + c * tq
        kmax = min(lk, -(-(q_start + tq) // LANES) * LANES)
        qspec = lambda w, c=c: pl.BlockSpec((tq, w), lambda b: (b * nq + c, 0))
        kspec = pl.BlockSpec((1, kmax, ATT_KV_WIDTH), lambda b: (b, 0, 0))
        outs.append(pl.pallas_call(
            functools.partial(_attn_body, q_start=q_start),
            grid=(batch,),
            in_specs=[qspec(ATT_WIDTH), kspec, kspec, qspec(ATT_HEADS),
                      pl.BlockSpec((1, ATT_HEADS, kmax), lambda b: (b, 0, 0))],
            out_specs=pl.BlockSpec((1, tq, ATT_WIDTH), lambda b: (b, 0, 0)),
            out_shape=jax.ShapeDtypeStruct((batch, tq, ATT_WIDTH), BF16),
            compiler_params=_params("parallel"),
        )(q, k, v, f_col, f_row))
    return jnp.concatenate(outs, axis=1).reshape(batch * t_q, ATT_WIDTH)


def _hgrn_body(hq_ref, hf_ref, hi_ref, hg_ref, lbl_ref, gain_ref, s0_ref, o_ref, st_ref,
               oi_s, *, chunk, layer):
    length = hq_ref.shape[0]
    lbl = lbl_ref[...]
    e = jnp.exp(lbl - jnp.max(lbl, axis=0, keepdims=True))
    lb_all = jnp.sum(e[:layer + 1, :], axis=0, keepdims=True) / jnp.sum(e, axis=0, keepdims=True)
    gain_all = gain_ref[...]
    tri = _lower_tri(chunk)
    rows = lax.broadcasted_iota(jnp.int32, (chunk, 1), 0)
    st_ref[...] = s0_ref[...]

    def chunk_step(c, carry):
        r0 = pl.multiple_of(c * chunk, 16)
        for hh in range(HG_PER_STEP):
            ls = slice(hh * HG_DIM, (hh + 1) * HG_DIM)
            lb = lb_all[:, ls]
            st = st_ref[0, hh]
            f = lb + (1.0 - lb) * jax.nn.sigmoid(hf_ref[pl.ds(r0, chunk), ls])
            kk = 1.0 - f
            hq = hq_ref[pl.ds(r0, chunk), ls]
            qq = hq * jax.nn.sigmoid(hq)
            iv = hi_ref[pl.ds(r0, chunk), ls]
            g = _tri_cumsum(tri, jnp.log(f))
            o_inter = lax.dot_general((qq * jnp.exp(g)).astype(BF16), st.astype(BF16),
                                      (((1,), (1,)), ((), ())), preferred_element_type=F32)
            for t in range(chunk):
                n = (t // SUBLANES + 1) * SUBLANES
                d = jnp.where(rows[:n] <= t, g[t:t + 1, :] - g[:n, :], -jnp.inf)
                a = jnp.exp(d) * kk[:n, :] * qq[t:t + 1, :]
                sc = jnp.sum(a, axis=1, keepdims=True)
                oi_s[hh, t:t + 1, :] = jnp.sum(sc * iv[:n, :], axis=0, keepdims=True)
            o = o_inter + oi_s[hh]
            on = _rms(o, gain_all[:, ls])
            hg = hg_ref[pl.ds(r0, chunk), ls]
            o_ref[pl.ds(r0, chunk), ls] = (on * (hg * jax.nn.sigmoid(hg))).astype(BF16)
            g_end = g[chunk - 1:chunk, :]
            kd = kk * jnp.exp(g_end - g)
            upd = lax.dot_general(iv.astype(BF16), kd.astype(BF16),
                                  (((0,), (0,)), ((), ())), preferred_element_type=F32)
            st_ref[0, hh] = st * jnp.exp(g_end) + upd
        return carry

    lax.fori_loop(0, length // chunk, chunk_step, 0)


def _hgrn(hgrp, lb_logits, gain, s0t, *, batch, layer):
    length = hgrp.shape[0] // batch
    chunk = _largest_block(length, 64, 16)
    width = HG_PER_STEP * HG_DIM
    steps = HG_HEADS // HG_PER_STEP
    col = lambda off: pl.BlockSpec((length, width), lambda b, h: (b, off * steps + h))
    par = lambda a: pl.BlockSpec((a.shape[0], width), lambda b, h: (0, h))
    st = pl.BlockSpec((1, HG_PER_STEP, HG_DIM, HG_DIM), lambda b, h: (b, h, 0, 0))
    buf = pltpu.VMEM((HG_PER_STEP, chunk, HG_DIM), F32)
    return pl.pallas_call(
        functools.partial(_hgrn_body, chunk=chunk, layer=layer),
        grid=(batch, steps),
        in_specs=[col(0), col(1), col(2), col(3), par(lb_logits), par(gain), st],
        out_specs=[pl.BlockSpec((length, width), lambda b, h: (b, h)), st],
        out_shape=[jax.ShapeDtypeStruct((batch * length, HG_WIDTH), BF16),
                   jax.ShapeDtypeStruct(s0t.shape, F32)],
        scratch_shapes=[buf],
        compiler_params=_params("parallel", "parallel"),
    )(hgrp, hgrp, hgrp, hgrp, lb_logits, gain, s0t)


def _proj_body(xa_ref, atta_ref, hgoa_ref, xb_ref, attb_ref, hgob_ref, woa_ref, wob_ref, g2_ref, wq_ref, keys_ref,
               x1_ref, h2_ref, s_ref, *, steps_a):
    first = pl.program_id(0) < steps_a
    pick = lambda a_ref, b_ref: jnp.where(first, a_ref[...], b_ref[...])
    x1 = (pick(xa_ref, xb_ref) + jnp.dot(pick(atta_ref, attb_ref), woa_ref[...], preferred_element_type=F32)
          + jnp.dot(pick(hgoa_ref, hgob_ref), wob_ref[...], preferred_element_type=F32))
    x1_ref[...] = x1
    h2 = _rms(x1, g2_ref[...])
    h2_ref[...] = h2
    qp = jnp.dot(h2.astype(BF16), wq_ref[...], preferred_element_type=F32)
    for h in range(PEER_HEADS):
        for c in range(2):
            col = (2 * h + c) * PEER_HALF
            s_ref[2 * h + c] = lax.dot_general(keys_ref[h, c], qp[:, col:col + PEER_HALF].astype(BF16),
                                               (((1,), (1,)), ((), ())), preferred_element_type=F32)


def _proj(group_a, group_b, woa, wob, g2, wq, keys):
    na, nb = group_a[0].shape[0], group_b[0].shape[0]
    n = na + nb
    tb = min(_largest_block(na, TOKEN_BLOCK_PROJ, LANES), _largest_block(nb, TOKEN_BLOCK_PROJ, LANES))
    assert na % tb == 0 and nb % tb == 0
    steps_a = na // tb
    row = lambda w: pl.BlockSpec((tb, w), lambda i: (i, 0))
    row_a = lambda w: pl.BlockSpec((tb, w), lambda i: (jnp.minimum(i, steps_a - 1), 0))
    row_b = lambda w: pl.BlockSpec((tb, w), lambda i: (jnp.maximum(i - steps_a, 0), 0))
    full = lambda a: pl.BlockSpec(a.shape, lambda i: (0,) * a.ndim)
    widths = (D_MODEL, ATT_WIDTH, HG_WIDTH)
    return pl.pallas_call(
        functools.partial(_proj_body, steps_a=steps_a),
        grid=(n // tb,),
        in_specs=[row_a(w) for w in widths] + [row_b(w) for w in widths]
                 + [full(woa), full(wob), full(g2), full(wq), full(keys)],
        out_specs=[row(D_MODEL), row(D_MODEL), pl.BlockSpec((2 * PEER_HEADS, PEER_N_KEYS, tb), lambda i: (0, 0, i))],
        out_shape=[jax.ShapeDtypeStruct((n, D_MODEL), F32),
                   jax.ShapeDtypeStruct((n, D_MODEL), F32),
                   jax.ShapeDtypeStruct((2 * PEER_HEADS, PEER_N_KEYS, n), F32)],
        compiler_params=_params("arbitrary"),
    )(*group_a, *group_b, woa, wob, g2, wq, keys)


def _stream_topk(w_ref, n, k_out, emit, payload_ref=None):
    n_acc = 4
    shape = w_ref.shape[1:]

    def step(j, prev):
        accs = [(jnp.full(shape, -jnp.inf, F32), jnp.full(shape, n, jnp.int32), jnp.full(shape, -1, jnp.int32))
                for _ in range(n_acc)]
        for k in range(n):
            v = jnp.where(prev == k, -jnp.inf, w_ref[k])
            w_ref[k] = v
            m, pos, pay = accs[k % n_acc]
            new = v > m
            pay = pay if payload_ref is None else jnp.where(new, payload_ref[k], pay)
            accs[k % n_acc] = (jnp.where(new, v, m), jnp.where(new, k, pos), pay)
        while len(accs) > 1:
            a, b = accs[0], accs[1]
            take = (b[0] > a[0]) | ((b[0] == a[0]) & (b[1] < a[1]))
            accs = accs[2:] + [tuple(jnp.where(take, y, x) for x, y in zip(a, b))]
        m, pos, pay = accs[0]
        emit(j, m, pos if payload_ref is None else pay)
        return pos

    lax.fori_loop(0, k_out, step, jnp.full(shape, n, jnp.int32))


def _pair_list(k):
    return [(i, j) for i in range(k) for j in range(k // (i + 1))]


def _topk_body(s_ref, eidx_ref, gate_ref, stage_s, w_s, va_s, ia_s, vb_s, ib_s, cand_s, cid_s, ts_s):
    tiles = ROUTE_TOKENS // LANES
    for c, (v_s, i_s) in enumerate(((va_s, ia_s), (vb_s, ib_s))):
        for lt in range(tiles):
            stage_s[lt * PEER_N_KEYS:(lt + 1) * PEER_N_KEYS, :] = s_ref[c, :, lt * LANES:(lt + 1) * LANES]
        for k in range(PEER_N_KEYS):
            w_s[k] = stage_s[pl.ds(k, tiles, stride=PEER_N_KEYS), :]

        def emit(j, m, pos, v_s=v_s, i_s=i_s):
            v_s[j] = m
            i_s[j] = pos

        _stream_topk(w_s, PEER_N_KEYS, PEER_TOPK, emit)
    pairs = _pair_list(PEER_TOPK)
    for p, (i, j) in enumerate(pairs):
        cand_s[p] = va_s[i] + vb_s[j]
        cid_s[p] = ia_s[i] * PEER_N_KEYS + ib_s[j]

    def emit_final(j, m, pay):
        ts_s[j] = m
        eidx_ref[0, j] = pay

    _stream_topk(cand_s, len(pairs), PEER_TOPK, emit_final, payload_ref=cid_s)
    ex = [jnp.exp(ts_s[j] - ts_s[0]) for j in range(PEER_TOPK)]
    tot = ex[0]
    for e in ex[1:]:
        tot = tot + e
    for j in range(PEER_TOPK):
        gate_ref[0, j] = ex[j] / tot


def _route_topk(scores):
    n = scores.shape[2]
    assert n % ROUTE_TOKENS == 0 and ROUTE_TOKENS // LANES == SUBLANES
    nb = n // ROUTE_TOKENS
    n_pairs = len(_pair_list(PEER_TOPK))
    tile = lambda rows, dt: pltpu.VMEM((rows, SUBLANES, LANES), dt)
    sel = pl.BlockSpec((1, PEER_TOPK, SUBLANES, LANES), lambda i, h: (i, h, 0, 0))
    return pl.pallas_call(
        _topk_body,
        grid=(nb, PEER_HEADS),
        in_specs=[pl.BlockSpec((2, PEER_N_KEYS, ROUTE_TOKENS), lambda i, h: (h, 0, i))],
        out_specs=[sel, sel],
        out_shape=[jax.ShapeDtypeStruct((nb, PEER_SEL, SUBLANES, LANES), jnp.int32),
                   jax.ShapeDtypeStruct((nb, PEER_SEL, SUBLANES, LANES), F32)],
        scratch_shapes=[pltpu.VMEM((SUBLANES * PEER_N_KEYS, LANES), F32), tile(PEER_N_KEYS, F32),
                        tile(PEER_TOPK, F32), tile(PEER_TOPK, jnp.int32),
                        tile(PEER_TOPK, F32), tile(PEER_TOPK, jnp.int32),
                        tile(n_pairs, F32), tile(n_pairs, jnp.int32), tile(PEER_TOPK, F32)],
        compiler_params=_params("parallel", "parallel"),
    )(scores)


def _pack_bf16_pair(hi, lo):
    bits = lambda a: lax.bitcast_convert_type(a.astype(BF16), jnp.uint16).astype(jnp.uint32)
    return (bits(hi) << 16) | bits(lo)


def _packed_hi(word):
    return lax.bitcast_convert_type(word & jnp.uint32(0xFFFF0000), F32)


def _packed_lo(word):
    return lax.bitcast_convert_type(word << 16, F32)


def _experts_body(eidx_ref, enext_ref, gate_ref, h2_ref, x1_ref, gf_ref, uv_ref, ya_ref, yb_ref,
                  idx_s, buf0, buf1, buf2, buf3, idx_sem, row_sem, *, tb, steps_a):
    bufs = (buf0, buf1, buf2, buf3)
    step, last = pl.program_id(0), pl.num_programs(0) - 1
    n_groups = tb // GATHER_GROUP
    ahead = GATHER_SETS - 1
    head = ahead * GATHER_GROUP
    idx_cps = (pltpu.make_async_copy(eidx_ref, idx_s.at[pl.ds(0, tb)], idx_sem.at[0]),
               pltpu.make_async_copy(enext_ref.at[pl.ds(0, head)], idx_s.at[pl.ds(tb, head)], idx_sem.at[1]))
    for cp in idx_cps:
        cp.start()
    for cp in idx_cps:
        cp.wait()

    def issue_group(g, slot_set):
        for j in range(GATHER_GROUP):
            t = g * GATHER_GROUP + j
            for r in range(PEER_SEL):
                pltpu.make_async_copy(uv_ref.at[idx_s[t, r]],
                                      bufs[slot_set].at[j, pl.ds(r * SUBLANES, SUBLANES), :],
                                      row_sem.at[slot_set, j]).start(priority=r % 2)

    def wait(slot_set, j):
        dst = bufs[slot_set].at[j]
        pltpu.make_async_copy(dst, dst, row_sem.at[slot_set, j]).wait()

    gf = gf_ref[...]
    lane_t = lax.broadcasted_iota(jnp.int32, (PEER_SEL, tb), 1)
    in_a = jnp.broadcast_to(step < steps_a, (1, D_MODEL))

    def group(g, slot_set):
        for j in range(GATHER_GROUP):
            wait(slot_set, j)
        issue_group(g + ahead, (slot_set + ahead) % GATHER_SETS)
        buf = bufs[slot_set]
        for j in range(GATHER_GROUP):
            t = g * GATHER_GROUP + j
            xrow = h2_ref[pl.ds(t, 1), :]
            tiles = [buf[j, pl.ds(s, PEER_SEL, stride=SUBLANES), :] for s in range(SUBLANES)]
            p = _packed_hi(tiles[0]) * xrow[:, :LANES]
            for s in range(1, SUBLANES):
                p = p + _packed_hi(tiles[s]) * xrow[:, s * LANES:(s + 1) * LANES]
            hcol = jnp.sum(p, axis=1, keepdims=True)
            act = 0.5 * hcol * (1.0 + lax.erf(hcol * (2.0 ** -0.5)))
            gcol = jnp.sum(jnp.where(lane_t == t, gate_ref[0], 0.0), axis=1, keepdims=True)
            w = gcol * act
            o = [jnp.sum(w * _packed_lo(tiles[s]), axis=0, keepdims=True) for s in range(SUBLANES)]
            orow = jnp.concatenate(o, axis=1)
            yrow = _rms(x1_ref[pl.ds(t, 1), :] + orow, gf)
            pltpu.store(ya_ref.at[pl.ds(t, 1), :], yrow, mask=in_a)
            pltpu.store(yb_ref.at[pl.ds(t, 1), :], yrow, mask=~in_a)

    def first_head(_, carry):
        for s in range(ahead):
            issue_group(s, s)
        return carry

    lax.fori_loop(0, (step == 0).astype(jnp.int32), first_head, 0)

    def main(k, carry):
        for s in range(GATHER_SETS):
            group(k * GATHER_SETS + s, s)
        return carry

    lax.fori_loop(0, n_groups // GATHER_SETS, main, 0)

    @pl.when(step == last)
    def _():
        for s in range(ahead):
            for j in range(GATHER_GROUP):
                wait(s, j)


def _experts(eidx, gate, h2, x1, gf, uv, n_a):
    n = h2.shape[0]
    tb = TOKEN_BLOCK_EXPERT
    tbr = gate.shape[2]
    assert GATHER_SETS == 4 and tbr % tb == 0 and tb % (GATHER_GROUP * GATHER_SETS) == 0
    assert n_a % tb == 0 and 0 < n_a < n
    per, nb, steps_a = tbr // tb, n // tb, n_a // tb
    head = (GATHER_SETS - 1) * GATHER_GROUP
    row = lambda w: pl.BlockSpec((tb, w), lambda i: (i, 0))
    nxt = pl.BlockSpec((tb, PEER_SEL), lambda i: (jnp.minimum(i + 1, nb - 1), 0))
    sel = pl.BlockSpec((1, PEER_SEL, tb), lambda i: (i // per, 0, i % per))
    slots = pltpu.VMEM((GATHER_GROUP, PEER_SEL * SUBLANES, LANES), jnp.uint32)
    return pl.pallas_call(
        functools.partial(_experts_body, tb=tb, steps_a=steps_a),
        grid=(nb,),
        in_specs=[row(PEER_SEL), nxt, sel, row(D_MODEL), row(D_MODEL), pl.BlockSpec(gf.shape, lambda i: (0, 0)),
                  pl.BlockSpec(memory_space=pl.ANY)],
        out_specs=[pl.BlockSpec((tb, D_MODEL), lambda i: (jnp.minimum(i, steps_a - 1), 0)),
                   pl.BlockSpec((tb, D_MODEL), lambda i: (jnp.maximum(i - steps_a, 0), 0))],
        out_shape=[jax.ShapeDtypeStruct((n_a, D_MODEL), F32), jax.ShapeDtypeStruct((n - n_a, D_MODEL), F32)],
        scratch_shapes=[pltpu.SMEM((tb + head, PEER_SEL), jnp.int32), slots, slots, slots, slots,
                        pltpu.SemaphoreType.DMA((2,)),
                        pltpu.SemaphoreType.DMA((GATHER_SETS, GATHER_GROUP))],
        compiler_params=_params("arbitrary"),
    )(eidx, eidx, gate, h2, x1, gf, uv)


def _mixer(x, batch, params, layer, cache=None):
    q, k, v, logf, hgrp = _inproj(x, params["g1"], params["wqkv"], params["wfg"], params["bfg"], params["wh"])
    length = x.shape[0] // batch
    if cache is None:
        k_all = k.reshape(batch, length, ATT_KV_WIDTH)
        v_all = v.reshape(batch, length, ATT_KV_WIDTH)
        logf_all, q_off = logf.reshape(batch, length, ATT_HEADS), 0
        s0t = jnp.zeros((batch, HG_HEADS, HG_DIM, HG_DIM), F32)
    else:
        cache_k, cache_v, cache_logf, state = cache
        past = cache_k.shape[1]
        cat = lambda c, new: jnp.concatenate(
            [c.reshape(batch, past, -1).astype(F32), new.reshape(batch, length, -1)], axis=1)
        k_all, v_all = cat(cache_k, k), cat(cache_v, v)
        logf_all, q_off = cat(cache_logf, logf), past
        s0t = jnp.swapaxes(state.astype(F32), -1, -2)
    f_all = _cumsum(logf_all)
    f_col = f_all[:, q_off:].reshape(batch * length, ATT_HEADS)
    f_row = jnp.swapaxes(f_all, 1, 2)
    att = _attn(q, k_all, v_all, f_col, f_row, batch=batch, q_off=q_off)
    hgo, st = _hgrn(hgrp, params["lb_logits"], params["hg_gain"], s0t, batch=batch, layer=layer)
    return att, hgo, (k, v, logf, jnp.swapaxes(st, -1, -2))


def kernel(x_prompt, x_sample, cache_k, cache_v, cache_logf, state_hgrn, meta_tokens,
           w_in, b_forget, hg_lb_logits, hg_norm_gain, w_out, norm1_gain, norm2_gain,
           peer_w_query, peer_sub_keys, peer_expert_u, peer_expert_v, final_norm_gain):
    depth = w_in.shape[0]
    assert depth == 1, "single-layer trunk: the PEER stage fuses the final norm"
    assert D_MODEL == SUBLANES * LANES, "an expert's packed u|v row must fill exactly one (SUBLANES, LANES) slab"
    bp, seq, _ = x_prompt.shape
    bs, dseq, _ = x_sample.shape
    lp = N_META + seq
    meta = jnp.broadcast_to(meta_tokens.astype(x_prompt.dtype)[None], (bp, N_META, D_MODEL))
    xp = jnp.concatenate([meta, x_prompt], axis=1).reshape(bp * lp, D_MODEL)
    xs = x_sample.reshape(bs * dseq, D_MODEL)

    l = 0
    o_fg = ATT_WIDTH + 2 * ATT_KV_WIDTH
    o_h = o_fg + ATT_HEADS
    row = lambda a: a.reshape(1, -1).astype(F32)
    params = {
        "g1": row(norm1_gain[l]),
        "wqkv": w_in[l][:, :o_fg].astype(BF16),
        "wfg": jnp.pad(w_in[l][:, o_fg:o_h], ((0, 0), (0, LANES - ATT_HEADS))).astype(BF16),
        "bfg": row(b_forget[l]),
        "wh": w_in[l][:, o_h:].astype(BF16),
        "lb_logits": hg_lb_logits.astype(F32),
        "hg_gain": row(hg_norm_gain[l]),
    }
    att_p, hgo_p, sp = _mixer(xp, bp, params, l)
    att_s, hgo_s, ss = _mixer(xs, bs, params, l,
                              cache=(cache_k[l], cache_v[l], cache_logf[l], state_hgrn[l]))

    wo = w_out[l].astype(BF16)
    x1, h2, scores = _proj((xp, att_p, hgo_p), (xs, att_s, hgo_s), wo[:ATT_WIDTH], wo[ATT_WIDTH:],
                           row(norm2_gain[l]), peer_w_query[l].astype(BF16), peer_sub_keys[l].astype(BF16))
    eidx, gate = (a.reshape(-1, PEER_SEL, ROUTE_TOKENS) for a in _route_topk(scores))
    n_exp = peer_expert_u.shape[1]
    uv = _pack_bf16_pair(peer_expert_u[l], peer_expert_v[l]).reshape(n_exp, SUBLANES, LANES)
    eidx_t = jnp.swapaxes(eidx, 1, 2).reshape(-1, PEER_SEL)
    y_p, y_s = _experts(eidx_t, gate, h2, x1, row(final_norm_gain), uv, bp * lp)
    y_prompt = y_p.reshape(bp, lp, D_MODEL)[:, N_META:]
    y_sample = y_s.reshape(bs, dseq, D_MODEL)

    def states(s, batch, length):
        k, v, logf, st = s
        return (k.reshape(1, batch, length, ATT_KV_HEADS, ATT_HEAD_DIM),
                v.reshape(1, batch, length, ATT_KV_HEADS, ATT_HEAD_DIM),
                logf.reshape(1, batch, length, ATT_HEADS),
                st.reshape(1, batch, HG_HEADS, HG_DIM, HG_DIM))

    return (y_prompt, y_sample) + states(sp, bp, lp) + states(ss, bs, dseq)
```

```python
import functools

import jax
import jax.numpy as jnp
from jax import lax
from jax.experimental import pallas as pl
from jax.experimental.pallas import tpu as pltpu

F32 = jnp.float32
BF16 = jnp.bfloat16
EPS = 1e-6
N_META = 16

D_MODEL = 1024
ATT_HEADS = 8
ATT_KV_HEADS = 4
ATT_HEAD_DIM = 64
ATT_GROUP = ATT_HEADS // ATT_KV_HEADS
ATT_WIDTH = ATT_HEADS * ATT_HEAD_DIM
ATT_KV_WIDTH = ATT_KV_HEADS * ATT_HEAD_DIM
HG_HEADS = 4
HG_DIM = 128
HG_WIDTH = HG_HEADS * HG_DIM
HG_PER_STEP = 4
PEER_HEADS = 8
PEER_N_KEYS = 128
PEER_TOPK = 16
PEER_HALF = 128
PEER_SEL = PEER_HEADS * PEER_TOPK

LANES = 128
SUBLANES = 8
VMEM_LIMIT = 48 * 1024 * 1024

TOKEN_BLOCK_PROJ = 512
ROUTE_TOKENS = SUBLANES * LANES
TOKEN_BLOCK_EXPERT = 256
GATHER_GROUP = 8
GATHER_SETS = 4


def _params(*sem):
    return pltpu.CompilerParams(dimension_semantics=sem, vmem_limit_bytes=VMEM_LIMIT)


def _largest_block(n, cap, mult):
    best = None
    for d in range(mult, cap + 1, mult):
        if n % d == 0:
            best = d
    assert best is not None, (n, cap, mult)
    return best


def _split3(x):
    hi = x.astype(BF16)
    r1 = x - hi.astype(F32)
    mid = r1.astype(BF16)
    lo = (r1 - mid.astype(F32)).astype(BF16)
    return hi, mid, lo


def _tri_cumsum(tri, x):
    hi, mid, lo = _split3(x)
    d = lambda a: jnp.dot(tri, a, preferred_element_type=F32)
    return (d(lo) + d(mid)) + d(hi)


def _lower_tri(c):
    r = lax.broadcasted_iota(jnp.int32, (c, c), 0)
    s = lax.broadcasted_iota(jnp.int32, (c, c), 1)
    return jnp.where(s <= r, 1.0, 0.0).astype(BF16)


def _rms(x, gain):
    return x * lax.rsqrt(jnp.mean(x * x, axis=-1, keepdims=True) + EPS) * gain


def _inproj_body(x_ref, g_ref, wqkv_ref, wfg_ref, bfg_ref, wh_ref,
                 q_ref, k_ref, v_ref, logf_ref, hgrp_ref):
    hb = _rms(x_ref[...], g_ref[...]).astype(BF16)
    qkv = jnp.dot(hb, wqkv_ref[...], preferred_element_type=F32)
    q_ref[...] = (qkv[:, :ATT_WIDTH] * (ATT_HEAD_DIM ** -0.5)).astype(BF16)
    k_ref[...] = qkv[:, ATT_WIDTH:ATT_WIDTH + ATT_KV_WIDTH]
    v_ref[...] = qkv[:, ATT_WIDTH + ATT_KV_WIDTH:]
    fg = jnp.dot(hb, wfg_ref[...], preferred_element_type=F32)[:, :ATT_HEADS] + bfg_ref[...]
    logf_ref[...] = jnp.minimum(fg, 0.0) - jnp.log1p(jnp.exp(-jnp.abs(fg)))
    hgrp_ref[...] = jnp.dot(hb, wh_ref[...], preferred_element_type=F32)


def _inproj(x, gain, wqkv, wfg, bfg, wh):
    n = x.shape[0]
    tb = _largest_block(n, TOKEN_BLOCK_PROJ, SUBLANES)
    row = lambda w: pl.BlockSpec((tb, w), lambda i: (i, 0))
    full = lambda a: pl.BlockSpec(a.shape, lambda i: (0,) * a.ndim)
    return pl.pallas_call(
        _inproj_body,
        grid=(n // tb,),
        in_specs=[row(D_MODEL), full(gain), full(wqkv), full(wfg), full(bfg), full(wh)],
        out_specs=[row(ATT_WIDTH), row(ATT_KV_WIDTH), row(ATT_KV_WIDTH), row(ATT_HEADS), row(4 * HG_WIDTH)],
        out_shape=[jax.ShapeDtypeStruct((n, ATT_WIDTH), BF16),
                   jax.ShapeDtypeStruct((n, ATT_KV_WIDTH), F32),
                   jax.ShapeDtypeStruct((n, ATT_KV_WIDTH), F32),
                   jax.ShapeDtypeStruct((n, ATT_HEADS), F32),
                   jax.ShapeDtypeStruct((n, 4 * HG_WIDTH), F32)],
        compiler_params=_params("parallel"),
    )(x, gain, wqkv, wfg, bfg, wh)


def _cumsum_body(x_ref, o_ref, *, cb):
    length, width = x_ref.shape[1], x_ref.shape[2]
    tri = _lower_tri(cb)
    carry = jnp.zeros((1, width), F32)
    for j in range(length // cb):
        f = _tri_cumsum(tri, x_ref[0, j * cb:(j + 1) * cb, :]) + carry
        o_ref[0, j * cb:(j + 1) * cb, :] = f
        carry = f[cb - 1:cb, :]


def _cumsum(logf):
    b, length, h = logf.shape
    cb = _largest_block(length, 512, 8)
    spec = pl.BlockSpec((1, length, h), lambda i: (i, 0, 0))
    return pl.pallas_call(
        functools.partial(_cumsum_body, cb=cb),
        grid=(b,),
        in_specs=[spec],
        out_specs=spec,
        out_shape=jax.ShapeDtypeStruct(logf.shape, F32),
        compiler_params=_params("parallel"),
    )(logf)


def _attn_body(q_ref, k_ref, v_ref, fc_ref, fr_ref, o_ref, *, q_start):
    tq, kmax = q_ref.shape[0], k_ref.shape[1]
    qpos = q_start + lax.broadcasted_iota(jnp.int32, (tq, kmax), 0)
    kpos = lax.broadcasted_iota(jnp.int32, (tq, kmax), 1)
    mask = kpos <= qpos
    kb = k_ref[0].astype(BF16)
    vb = v_ref[0].astype(BF16)
    q = q_ref[...]
    fc = fc_ref[...]
    fr = fr_ref[0]
    outs = []
    for h in range(ATT_HEADS):
        g = h // ATT_GROUP
        hs = slice(h * ATT_HEAD_DIM, (h + 1) * ATT_HEAD_DIM)
        gs = slice(g * ATT_HEAD_DIM, (g + 1) * ATT_HEAD_DIM)
        s = lax.dot_general(q[:, hs], kb[:, gs], (((1,), (1,)), ((), ())), preferred_element_type=F32)
        s = s + fc[:, h:h + 1] - fr[h:h + 1, :]
        s = jnp.where(mask, s, -jnp.inf)
        p = jnp.exp(s - jnp.max(s, axis=-1, keepdims=True))
        l = jnp.sum(p, axis=-1, keepdims=True)
        o = jnp.dot(p.astype(BF16), vb[:, gs], preferred_element_type=F32)
        outs.append(o / l)
    o_ref[0] = jnp.concatenate(outs, axis=-1).astype(BF16)


def _attn(q, k, v, f_col, f_row, *, batch, q_off):
    t_q = q.shape[0] // batch
    lk = k.shape[1]
    tq = _largest_block(t_q, 384, 8)
    nq = t_q // tq
    outs = []
    for c in range(nq):
        q_start = q_off + c * tq
        kmax = min(lk, -(-(q_start + tq) // LANES) * LANES)
        qspec = lambda w, c=c: pl.BlockSpec((tq, w), lambda b: (b * nq + c, 0))
        kspec = pl.BlockSpec((1, kmax, ATT_KV_WIDTH), lambda b: (b, 0, 0))
        outs.append(pl.pallas_call(
            functools.partial(_attn_body, q_start=q_start),
            grid=(batch,),
            in_specs=[qspec(ATT_WIDTH), kspec, kspec, qspec(ATT_HEADS),
                      pl.BlockSpec((1, ATT_HEADS, kmax), lambda b: (b, 0, 0))],
            out_specs=pl.BlockSpec((1, tq, ATT_WIDTH), lambda b: (b, 0, 0)),
            out_shape=jax.ShapeDtypeStruct((batch, tq, ATT_WIDTH), BF16),
            compiler_params=_params("parallel"),
        )(q, k, v, f_col, f_row))
    return jnp.concatenate(outs, axis=1).reshape(batch * t_q, ATT_WIDTH)


def _hgrn_body(hq_ref, hf_ref, hi_ref, hg_ref, lbl_ref, gain_ref, s0_ref, o_ref, st_ref,
               oi_s, *, chunk, layer):
    length = hq_ref.shape[0]
    lbl = lbl_ref[...]
    e = jnp.exp(lbl - jnp.max(lbl, axis=0, keepdims=True))
    lb_all = jnp.sum(e[:layer + 1, :], axis=0, keepdims=True) / jnp.sum(e, axis=0, keepdims=True)
    gain_all = gain_ref[...]
    tri = _lower_tri(chunk)
    rows = lax.broadcasted_iota(jnp.int32, (chunk, 1), 0)
    st_ref[...] = s0_ref[...]

    def chunk_step(c, carry):
        r0 = pl.multiple_of(c * chunk, 16)
        for hh in range(HG_PER_STEP):
            ls = slice(hh * HG_DIM, (hh + 1) * HG_DIM)
            lb = lb_all[:, ls]
            st = st_ref[0, hh]
            f = lb + (1.0 - lb) * jax.nn.sigmoid(hf_ref[pl.ds(r0, chunk), ls])
            kk = 1.0 - f
            hq = hq_ref[pl.ds(r0, chunk), ls]
            qq = hq * jax.nn.sigmoid(hq)
            iv = hi_ref[pl.ds(r0, chunk), ls]
            g = _tri_cumsum(tri, jnp.log(f))
            o_inter = lax.dot_general((qq * jnp.exp(g)).astype(BF16), st.astype(BF16),
                                      (((1,), (1,)), ((), ())), preferred_element_type=F32)
            for t in range(chunk):
                n = (t // SUBLANES + 1) * SUBLANES
                d = jnp.where(rows[:n] <= t, g[t:t + 1, :] - g[:n, :], -jnp.inf)
                a = jnp.exp(d) * kk[:n, :] * qq[t:t + 1, :]
                sc = jnp.sum(a, axis=1, keepdims=True)
                oi_s[hh, t:t + 1, :] = jnp.sum(sc * iv[:n, :], axis=0, keepdims=True)
            o = o_inter + oi_s[hh]
            on = _rms(o, gain_all[:, ls])
            hg = hg_ref[pl.ds(r0, chunk), ls]
            o_ref[pl.ds(r0, chunk), ls] = (on * (hg * jax.nn.sigmoid(hg))).astype(BF16)
            g_end = g[chunk - 1:chunk, :]
            kd = kk * jnp.exp(g_end - g)
            upd = lax.dot_general(iv.astype(BF16), kd.astype(BF16),
                                  (((0,), (0,)), ((), ())), preferred_element_type=F32)
            st_ref[0, hh] = st * jnp.exp(g_end) + upd
        return carry

    lax.fori_loop(0, length // chunk, chunk_step, 0)


def _hgrn(hgrp, lb_logits, gain, s0t, *, batch, layer):
    length = hgrp.shape[0] // batch
    chunk = _largest_block(length, 64, 16)
    width = HG_PER_STEP * HG_DIM
    steps = HG_HEADS // HG_PER_STEP
    col = lambda off: pl.BlockSpec((length, width), lambda b, h: (b, off * steps + h))
    par = lambda a: pl.BlockSpec((a.shape[0], width), lambda b, h: (0, h))
    st = pl.BlockSpec((1, HG_PER_STEP, HG_DIM, HG_DIM), lambda b, h: (b, h, 0, 0))
    buf = pltpu.VMEM((HG_PER_STEP, chunk, HG_DIM), F32)
    return pl.pallas_call(
        functools.partial(_hgrn_body, chunk=chunk, layer=layer),
        grid=(batch, steps),
        in_specs=[col(0), col(1), col(2), col(3), par(lb_logits), par(gain), st],
        out_specs=[pl.BlockSpec((length, width), lambda b, h: (b, h)), st],
        out_shape=[jax.ShapeDtypeStruct((batch * length, HG_WIDTH), BF16),
                   jax.ShapeDtypeStruct(s0t.shape, F32)],
        scratch_shapes=[buf],
        compiler_params=_params("parallel", "parallel"),
    )(hgrp, hgrp, hgrp, hgrp, lb_logits, gain, s0t)


def _proj_body(xa_ref, atta_ref, hgoa_ref, xb_ref, attb_ref, hgob_ref, woa_ref, wob_ref, g2_ref, wq_ref, keys_ref,
               x1_ref, h2_ref, s_ref, *, steps_a):
    first = pl.program_id(0) < steps_a
    pick = lambda a_ref, b_ref: jnp.where(first, a_ref[...], b_ref[...])
    x1 = (pick(xa_ref, xb_ref) + jnp.dot(pick(atta_ref, attb_ref), woa_ref[...], preferred_element_type=F32)
          + jnp.dot(pick(hgoa_ref, hgob_ref), wob_ref[...], preferred_element_type=F32))
    x1_ref[...] = x1
    h2 = _rms(x1, g2_ref[...])
    h2_ref[...] = h2
    qp = jnp.dot(h2.astype(BF16), wq_ref[...], preferred_element_type=F32)
    for h in range(PEER_HEADS):
        for c in range(2):
            col = (2 * h + c) * PEER_HALF
            s_ref[2 * h + c] = lax.dot_general(keys_ref[h, c], qp[:, col:col + PEER_HALF].astype(BF16),
                                               (((1,), (1,)), ((), ())), preferred_element_type=F32)


def _proj(group_a, group_b, woa, wob, g2, wq, keys):
    na, nb = group_a[0].shape[0], group_b[0].shape[0]
    n = na + nb
    tb = min(_largest_block(na, TOKEN_BLOCK_PROJ, LANES), _largest_block(nb, TOKEN_BLOCK_PROJ, LANES))
    assert na % tb == 0 and nb % tb == 0
    steps_a = na // tb
    row = lambda w: pl.BlockSpec((tb, w), lambda i: (i, 0))
    row_a = lambda w: pl.BlockSpec((tb, w), lambda i: (jnp.minimum(i, steps_a - 1), 0))
    row_b = lambda w: pl.BlockSpec((tb, w), lambda i: (jnp.maximum(i - steps_a, 0), 0))
    full = lambda a: pl.BlockSpec(a.shape, lambda i: (0,) * a.ndim)
    widths = (D_MODEL, ATT_WIDTH, HG_WIDTH)
    return pl.pallas_call(
        functools.partial(_proj_body, steps_a=steps_a),
        grid=(n // tb,),
        in_specs=[row_a(w) for w in widths] + [row_b(w) for w in widths]
                 + [full(woa), full(wob), full(g2), full(wq), full(keys)],
        out_specs=[row(D_MODEL), row(D_MODEL), pl.BlockSpec((2 * PEER_HEADS, PEER_N_KEYS, tb), lambda i: (0, 0, i))],
        out_shape=[jax.ShapeDtypeStruct((n, D_MODEL), F32),
                   jax.ShapeDtypeStruct((n, D_MODEL), F32),
                   jax.ShapeDtypeStruct((2 * PEER_HEADS, PEER_N_KEYS, n), F32)],
        compiler_params=_params("arbitrary"),
    )(*group_a, *group_b, woa, wob, g2, wq, keys)


def _stream_topk(w_ref, n, k_out, emit, payload_ref=None):
    n_acc = 4
    shape = w_ref.shape[1:]

    def step(j, prev):
        accs = [(jnp.full(shape, -jnp.inf, F32), jnp.full(shape, n, jnp.int32), jnp.full(shape, -1, jnp.int32))
                for _ in range(n_acc)]
        for k in range(n):
            v = jnp.where(prev == k, -jnp.inf, w_ref[k])
            w_ref[k] = v
            m, pos, pay = accs[k % n_acc]
            new = v > m
            pay = pay if payload_ref is None else jnp.where(new, payload_ref[k], pay)
            accs[k % n_acc] = (jnp.where(new, v, m), jnp.where(new, k, pos), pay)
        while len(accs) > 1:
            a, b = accs[0], accs[1]
            take = (b[0] > a[0]) | ((b[0] == a[0]) & (b[1] < a[1]))
            accs = accs[2:] + [tuple(jnp.where(take, y, x) for x, y in zip(a, b))]
        m, pos, pay = accs[0]
        emit(j, m, pos if payload_ref is None else pay)
        return pos

    lax.fori_loop(0, k_out, step, jnp.full(shape, n, jnp.int32))


def _pair_list(k):
    return [(i, j) for i in range(k) for j in range(k // (i + 1))]


def _topk_body(s_ref, eidx_ref, gate_ref, w_s, va_s, ia_s, vb_s, ib_s, cand_s, cid_s, ts_s):
    tiles = ROUTE_TOKENS // LANES
    for c, (v_s, i_s) in enumerate(((va_s, ia_s), (vb_s, ib_s))):
        for kg in range(PEER_N_KEYS // SUBLANES):
            ks = slice(kg * SUBLANES, (kg + 1) * SUBLANES)
            by_tile = jnp.stack([s_ref[c, ks, lt * LANES:(lt + 1) * LANES] for lt in range(tiles)], axis=0)
            w_s[ks] = jnp.swapaxes(by_tile, 0, 1)

        def emit(j, m, pos, v_s=v_s, i_s=i_s):
            v_s[j] = m
            i_s[j] = pos

        _stream_topk(w_s, PEER_N_KEYS, PEER_TOPK, emit)
    pairs = _pair_list(PEER_TOPK)
    for p, (i, j) in enumerate(pairs):
        cand_s[p] = va_s[i] + vb_s[j]
        cid_s[p] = ia_s[i] * PEER_N_KEYS + ib_s[j]

    def emit_final(j, m, pay):
        ts_s[j] = m
        eidx_ref[0, j] = pay

    _stream_topk(cand_s, len(pairs), PEER_TOPK, emit_final, payload_ref=cid_s)
    ex = [jnp.exp(ts_s[j] - ts_s[0]) for j in range(PEER_TOPK)]
    tot = ex[0]
    for e in ex[1:]:
        tot = tot + e
    for j in range(PEER_TOPK):
        gate_ref[0, j] = ex[j] / tot


def _route_topk(scores):
    n = scores.shape[2]
    assert n % ROUTE_TOKENS == 0 and ROUTE_TOKENS // LANES == SUBLANES
    nb = n // ROUTE_TOKENS
    n_pairs = len(_pair_list(PEER_TOPK))
    tile = lambda rows, dt: pltpu.VMEM((rows, SUBLANES, LANES), dt)
    sel = pl.BlockSpec((1, PEER_TOPK, SUBLANES, LANES), lambda i, h: (i, h, 0, 0))
    return pl.pallas_call(
        _topk_body,
        grid=(nb, PEER_HEADS),
        in_specs=[pl.BlockSpec((2, PEER_N_KEYS, ROUTE_TOKENS), lambda i, h: (h, 0, i))],
        out_specs=[sel, sel],
        out_shape=[jax.ShapeDtypeStruct((nb, PEER_SEL, SUBLANES, LANES), jnp.int32),
                   jax.ShapeDtypeStruct((nb, PEER_SEL, SUBLANES, LANES), F32)],
        scratch_shapes=[tile(PEER_N_KEYS, F32),
                        tile(PEER_TOPK, F32), tile(PEER_TOPK, jnp.int32),
                        tile(PEER_TOPK, F32), tile(PEER_TOPK, jnp.int32),
                        tile(n_pairs, F32), tile(n_pairs, jnp.int32), tile(PEER_TOPK, F32)],
        compiler_params=_params("parallel", "parallel"),
    )(scores)


def _pack_bf16_pair(hi, lo):
    bits = lambda a: lax.bitcast_convert_type(a.astype(BF16), jnp.uint16).astype(jnp.uint32)
    return (bits(hi) << 16) | bits(lo)


def _packed_hi(word):
    return lax.bitcast_convert_type(word & jnp.uint32(0xFFFF0000), F32)


def _packed_lo(word):
    return lax.bitcast_convert_type(word << 16, F32)


def _experts_body(eidx_ref, enext_ref, gate_ref, h2_ref, x1_ref, gf_ref, uv_ref, ya_ref, yb_ref,
                  idx_s, buf0, buf1, buf2, buf3, idx_sem, row_sem, *, tb, steps_a):
    bufs = (buf0, buf1, buf2, buf3)
    step, last = pl.program_id(0), pl.num_programs(0) - 1
    n_groups = tb // GATHER_GROUP
    ahead = GATHER_SETS - 1
    head = ahead * GATHER_GROUP
    idx_cps = (pltpu.make_async_copy(eidx_ref, idx_s.at[pl.ds(0, tb)], idx_sem.at[0]),
               pltpu.make_async_copy(enext_ref.at[pl.ds(0, head)], idx_s.at[pl.ds(tb, head)], idx_sem.at[1]))
    for cp in idx_cps:
        cp.start()
    for cp in idx_cps:
        cp.wait()

    def issue_group(g, slot_set):
        for j in range(GATHER_GROUP):
            t = g * GATHER_GROUP + j
            for r in range(PEER_SEL):
                pltpu.make_async_copy(uv_ref.at[idx_s[t, r]],
                                      bufs[slot_set].at[j, pl.ds(r * SUBLANES, SUBLANES), :],
                                      row_sem.at[slot_set, j]).start(priority=r % 2)

    def wait(slot_set, j):
        dst = bufs[slot_set].at[j]
        pltpu.make_async_copy(dst, dst, row_sem.at[slot_set, j]).wait()

    gf = gf_ref[...]
    lane_t = lax.broadcasted_iota(jnp.int32, (PEER_SEL, tb), 1)
    in_a = jnp.broadcast_to(step < steps_a, (1, D_MODEL))

    def group(g, slot_set):
        for j in range(GATHER_GROUP):
            wait(slot_set, j)
        issue_group(g + ahead, (slot_set + ahead) % GATHER_SETS)
        buf = bufs[slot_set]
        for j in range(GATHER_GROUP):
            t = g * GATHER_GROUP + j
            xrow = h2_ref[pl.ds(t, 1), :]
            tiles = [buf[j, pl.ds(s, PEER_SEL, stride=SUBLANES), :] for s in range(SUBLANES)]
            p = _packed_hi(tiles[0]) * xrow[:, :LANES]
            for s in range(1, SUBLANES):
                p = p + _packed_hi(tiles[s]) * xrow[:, s * LANES:(s + 1) * LANES]
            hcol = jnp.sum(p, axis=1, keepdims=True)
            act = 0.5 * hcol * (1.0 + lax.erf(hcol * (2.0 ** -0.5)))
            gcol = jnp.sum(jnp.where(lane_t == t, gate_ref[0], 0.0), axis=1, keepdims=True)
            w = gcol * act
            o = [jnp.sum(w * _packed_lo(tiles[s]), axis=0, keepdims=True) for s in range(SUBLANES)]
            orow = jnp.concatenate(o, axis=1)
            yrow = _rms(x1_ref[pl.ds(t, 1), :] + orow, gf)
            pltpu.store(ya_ref.at[pl.ds(t, 1), :], yrow, mask=in_a)
            pltpu.store(yb_ref.at[pl.ds(t, 1), :], yrow, mask=~in_a)

    def first_head(_, carry):
        for s in range(ahead):
            issue_group(s, s)
        return carry

    lax.fori_loop(0, (step == 0).astype(jnp.int32), first_head, 0)

    def main(k, carry):
        for s in range(GATHER_SETS):
            group(k * GATHER_SETS + s, s)
        return carry

    lax.fori_loop(0, n_groups // GATHER_SETS, main, 0)

    @pl.when(step == last)
    def _():
        for s in range(ahead):
            for j in range(GATHER_GROUP):
                wait(s, j)


def _experts(eidx, gate, h2, x1, gf, uv, n_a):
    n = h2.shape[0]
    tb = TOKEN_BLOCK_EXPERT
    tbr = gate.shape[2]
    assert GATHER_SETS == 4 and tbr % tb == 0 and tb % (GATHER_GROUP * GATHER_SETS) == 0
    assert n_a % tb == 0 and 0 < n_a < n
    per, nb, steps_a = tbr // tb, n // tb, n_a // tb
    head = (GATHER_SETS - 1) * GATHER_GROUP
    row = lambda w: pl.BlockSpec((tb, w), lambda i: (i, 0))
    nxt = pl.BlockSpec((tb, PEER_SEL), lambda i: (jnp.minimum(i + 1, nb - 1), 0))
    sel = pl.BlockSpec((1, PEER_SEL, tb), lambda i: (i // per, 0, i % per))
    slots = pltpu.VMEM((GATHER_GROUP, PEER_SEL * SUBLANES, LANES), jnp.uint32)
    return pl.pallas_call(
        functools.partial(_experts_body, tb=tb, steps_a=steps_a),
        grid=(nb,),
        in_specs=[row(PEER_SEL), nxt, sel, row(D_MODEL), row(D_MODEL), pl.BlockSpec(gf.shape, lambda i: (0, 0)),
                  pl.BlockSpec(memory_space=pl.ANY)],
        out_specs=[pl.BlockSpec((tb, D_MODEL), lambda i: (jnp.minimum(i, steps_a - 1), 0)),
                   pl.BlockSpec((tb, D_MODEL), lambda i: (jnp.maximum(i - steps_a, 0), 0))],
        out_shape=[jax.ShapeDtypeStruct((n_a, D_MODEL), F32), jax.ShapeDtypeStruct((n - n_a, D_MODEL), F32)],
        scratch_shapes=[pltpu.SMEM((tb + head, PEER_SEL), jnp.int32), slots, slots, slots, slots,
                        pltpu.SemaphoreType.DMA((2,)),
                        pltpu.SemaphoreType.DMA((GATHER_SETS, GATHER_GROUP))],
        compiler_params=_params("arbitrary"),
    )(eidx, eidx, gate, h2, x1, gf, uv)


def _mixer(x, batch, params, layer, cache=None):
    q, k, v, logf, hgrp = _inproj(x, params["g1"], params["wqkv"], params["wfg"], params["bfg"], params["wh"])
    length = x.shape[0] // batch
    if cache is None:
        k_all = k.reshape(batch, length, ATT_KV_WIDTH)
        v_all = v.reshape(batch, length, ATT_KV_WIDTH)
        logf_all, q_off = logf.reshape(batch, length, ATT_HEADS), 0
        s0t = jnp.zeros((batch, HG_HEADS, HG_DIM, HG_DIM), F32)
    else:
        cache_k, cache_v, cache_logf, state = cache
        past = cache_k.shape[1]
        cat = lambda c, new: jnp.concatenate(
            [c.reshape(batch, past, -1).astype(F32), new.reshape(batch, length, -1)], axis=1)
        k_all, v_all = cat(cache_k, k), cat(cache_v, v)
        logf_all, q_off = cat(cache_logf, logf), past
        s0t = jnp.swapaxes(state.astype(F32), -1, -2)
    f_all = _cumsum(logf_all)
    f_col = f_all[:, q_off:].reshape(batch * length, ATT_HEADS)
    f_row = jnp.swapaxes(f_all, 1, 2)
    att = _attn(q, k_all, v_all, f_col, f_row, batch=batch, q_off=q_off)
    hgo, st = _hgrn(hgrp, params["lb_logits"], params["hg_gain"], s0t, batch=batch, layer=layer)
    return att, hgo, (k, v, logf, jnp.swapaxes(st, -1, -2))


def kernel(x_prompt, x_sample, cache_k, cache_v, cache_logf, state_hgrn, meta_tokens,
           w_in, b_forget, hg_lb_logits, hg_norm_gain, w_out, norm1_gain, norm2_gain,
           peer_w_query, peer_sub_keys, peer_expert_u, peer_expert_v, final_norm_gain):
    depth = w_in.shape[0]
    assert depth == 1, "single-layer trunk: the PEER stage fuses the final norm"
    assert D_MODEL == SUBLANES * LANES, "an expert's packed u|v row must fill exactly one (SUBLANES, LANES) slab"
    bp, seq, _ = x_prompt.shape
    bs, dseq, _ = x_sample.shape
    lp = N_META + seq
    meta = jnp.broadcast_to(meta_tokens.astype(x_prompt.dtype)[None], (bp, N_META, D_MODEL))
    xp = jnp.concatenate([meta, x_prompt], axis=1).reshape(bp * lp, D_MODEL)
    xs = x_sample.reshape(bs * dseq, D_MODEL)

    l = 0
    o_fg = ATT_WIDTH + 2 * ATT_KV_WIDTH
    o_h = o_fg + ATT_HEADS
    row = lambda a: a.reshape(1, -1).astype(F32)
    params = {
        "g1": row(norm1_gain[l]),
        "wqkv": w_in[l][:, :o_fg].astype(BF16),
        "wfg": jnp.pad(w_in[l][:, o_fg:o_h], ((0, 0), (0, LANES - ATT_HEADS))).astype(BF16),
        "bfg": row(b_forget[l]),
        "wh": w_in[l][:, o_h:].astype(BF16),
        "lb_logits": hg_lb_logits.astype(F32),
        "hg_gain": row(hg_norm_gain[l]),
    }
    att_p, hgo_p, sp = _mixer(xp, bp, params, l)
    att_s, hgo_s, ss = _mixer(xs, bs, params, l,
                              cache=(cache_k[l], cache_v[l], cache_logf[l], state_hgrn[l]))

    wo = w_out[l].astype(BF16)
    x1, h2, scores = _proj((xp, att_p, hgo_p), (xs, att_s, hgo_s), wo[:ATT_WIDTH], wo[ATT_WIDTH:],
                           row(norm2_gain[l]), peer_w_query[l].astype(BF16), peer_sub_keys[l].astype(BF16))
    eidx, gate = (a.reshape(-1, PEER_SEL, ROUTE_TOKENS) for a in _route_topk(scores))
    n_exp = peer_expert_u.shape[1]
    uv = _pack_bf16_pair(peer_expert_u[l], peer_expert_v[l]).reshape(n_exp, SUBLANES, LANES)
    eidx_t = jnp.swapaxes(eidx, 1, 2).reshape(-1, PEER_SEL)
    y_p, y_s = _experts(eidx_t, gate, h2, x1, row(final_norm_gain), uv, bp * lp)
    y_prompt = y_p.reshape(bp, lp, D_MODEL)[:, N_META:]
    y_sample = y_s.reshape(bs, dseq, D_MODEL)

    def states(s, batch, length):
        k, v, logf, st = s
        return (k.reshape(1, batch, length, ATT_KV_HEADS, ATT_HEAD_DIM),
                v.reshape(1, batch, length, ATT_KV_HEADS, ATT_HEAD_DIM),
                logf.reshape(1, batch, length, ATT_HEADS),
                st.reshape(1, batch, HG_HEADS, HG_DIM, HG_DIM))

    return (y_prompt, y_sample) + states(sp, bp, lp) + states(ss, bs, dseq)
```
